```python
import jax, jax.numpy as jnp
from jax import lax
import numpy as np

D_MODEL = 2048
BATCH = 1
SEQ = 16384
DEPTH = 2

CHUNK = 64
QBLOCK = 128
N_BRANCHES = 4
BRANCH_W = D_MODEL // 4

HG_HEADS = 4
RW_HEADS = 8
RW_DIM = BRANCH_W // RW_HEADS
RW_DECAY_RANK = 32
RW_ICLR_RANK = 32
RW_GATE_RANK = 96
SB_HEADS = 4
SB_DIM = BRANCH_W // SB_HEADS
GLA_HEADS = 4
GLA_DK = 64
GLA_DV = 128
GLA_GATE_RANK = 16
GLA_TAU = 16.0

N_EXPERTS = 64
TOP_K = 8
N_GROUPS = 8
TOPK_GROUPS = 4
EXPERT_W = 512
ROUTED_SCALE = 2.5
DISPATCH_BLOCK = 128

DEEPNORM_ALPHA = (2 * DEPTH) ** 0.25
DEEPNORM_BETA = (8 * DEPTH) ** -0.25
LN_EPS = 1e-5
RW_GN_EPS = 64e-5

HG_SIZES = (BRANCH_W, BRANCH_W, BRANCH_W, BRANCH_W)
RW_SIZES = (BRANCH_W, BRANCH_W, BRANCH_W, RW_DECAY_RANK, RW_ICLR_RANK, RW_GATE_RANK)
SB_SIZES = (BRANCH_W, BRANCH_W, BRANCH_W)
GLA_SIZES = (GLA_HEADS * GLA_DK, GLA_HEADS * GLA_DK, GLA_HEADS * GLA_DV, GLA_GATE_RANK, BRANCH_W)
GATE_SIZE = N_BRANCHES * D_MODEL
BRANCH_SIZES = (sum(HG_SIZES), sum(RW_SIZES), sum(SB_SIZES), sum(GLA_SIZES), GATE_SIZE)
IN_WIDTH = sum(BRANCH_SIZES)

kernel_name = 'hybrid_gated_mixers_moe_deepnorm'


def split_cols(a, sizes):
    return jnp.split(a, np.cumsum(sizes)[:-1].tolist(), axis=-1)


def split_heads(a, n_heads):
    b, t, c = a.shape
    return a.reshape(b, t, n_heads, c // n_heads).transpose(0, 2, 1, 3)


def merge_heads(a):
    b, h, t, d = a.shape
    return a.transpose(0, 2, 1, 3).reshape(b, t, h * d)


def layer_norm(x, w, b):
    x32 = x.astype(jnp.float32)
    mu = jnp.mean(x32, axis=-1, keepdims=True)
    var = jnp.mean(jnp.square(x32 - mu), axis=-1, keepdims=True)
    return ((x32 - mu) * lax.rsqrt(var + LN_EPS) * w + b).astype(x.dtype)


def head_rms_norm(o, gain):
    o32 = o.astype(jnp.float32)
    o32 = o32 * lax.rsqrt(jnp.mean(jnp.square(o32), axis=-1, keepdims=True) + LN_EPS)
    return (merge_heads(o32) * gain).astype(o.dtype)


def head_group_norm(o, w, b, eps):
    o32 = o.astype(jnp.float32)
    mu = jnp.mean(o32, axis=-1, keepdims=True)
    var = jnp.mean(jnp.square(o32 - mu), axis=-1, keepdims=True)
    return merge_heads((o32 - mu) * lax.rsqrt(var + eps)) * w + b


def shift_lerp(z, mu):
    prev = jnp.pad(z, ((0, 0), (1, 0), (0, 0)))[:, :-1]
    return z + (prev - z) * mu


def chunked_gated_linear_attention(q, k, v, log_decay):
    b, h, t, dk = q.shape
    dv = v.shape[-1]
    n = t // CHUNK
    f32 = jnp.float32

    def chunks(a):
        return jnp.moveaxis(a.astype(f32).reshape(b, h, n, CHUNK, a.shape[-1]), 2, 0)

    qc, kc, vc = chunks(q), chunks(k), chunks(v)
    bc = jnp.cumsum(chunks(log_decay), axis=3)
    causal = jnp.tril(jnp.ones((CHUNK, CHUNK), bool))[:, :, None]

    def step(state, inp):
        q_i, k_i, v_i, b_i = inp
        o_inter = jnp.einsum('bhtk,bhkv->bhtv', q_i * jnp.exp(b_i), state)
        rel = jnp.where(causal, b_i[:, :, :, None, :] - b_i[:, :, None, :, :], -jnp.inf)
        scores = jnp.einsum('bhtsk,bhsk->bhts', q_i[:, :, :, None, :] * jnp.exp(rel), k_i)
        o_intra = jnp.einsum('bhts,bhsv->bhtv', scores, v_i)
        b_last = b_i[:, :, -1, :]
        state = jnp.exp(b_last)[..., None] * state + jnp.einsum(
            'bhsk,bhsv->bhkv', k_i * jnp.exp(b_last[:, :, None, :] - b_i), v_i)
        return state, o_inter + o_intra

    s0 = jnp.zeros((b, h, dk, dv), f32)
    _, o = lax.scan(step, s0, (qc, kc, vc, bc))
    return jnp.moveaxis(o, 0, 2).reshape(b, h, t, dv).astype(v.dtype)


def rwkv7_recurrence(r, decay, k, v, a, bvec):
    bsz, h, t, n = r.shape

    def step(state, inp):
        r_t, w_t, k_t, v_t, a_t, b_t = inp
        sa = jnp.einsum('bhvk,bhk->bhv', state, a_t)
        state = (state * w_t[:, :, None, :] + sa[..., None] * b_t[:, :, None, :]
                 + v_t[..., None] * k_t[:, :, None, :])
        return state, jnp.einsum('bhvk,bhk->bhv', state, r_t)

    seq = tuple(jnp.moveaxis(a_.astype(jnp.float32), 2, 0) for a_ in (r, decay, k, v, a, bvec))
    s0 = jnp.zeros((bsz, h, n, n), jnp.float32)
    _, o = lax.scan(step, s0, seq)
    return jnp.moveaxis(o, 0, 2)


def stick_breaking_attention(q, k, v):
    b, h, t, d = q.shape
    nq = t // QBLOCK
    qb = jnp.moveaxis(q.reshape(b, h, nq, QBLOCK, d), 2, 0)
    kpos = jnp.arange(t)
    scale = d ** -0.5

    def one_block(args):
        q_i, blk = args
        qpos = blk * QBLOCK + jnp.arange(QBLOCK)
        z = jnp.einsum('bhqd,bhkd->bhqk', q_i, k).astype(jnp.float32) * scale
        strict = kpos[None, :] < qpos[:, None]
        log_keep = jnp.where(strict, jax.nn.log_sigmoid(-z), 0.0)
        later = lax.cumsum(log_keep, axis=3, reverse=True) - log_keep
        weights = jnp.where(strict, jnp.exp(jax.nn.log_sigmoid(z) + later), 0.0)
        return jnp.einsum('bhqk,bhkd->bhqd', weights.astype(v.dtype), v)

    o = lax.map(one_block, (qb, jnp.arange(nq)))
    return jnp.moveaxis(o, 0, 2).reshape(b, h, t, d)


def hgrn2_branch(z, lower_bound, norm_w):
    q, f_logit, i, g = split_cols(z, HG_SIZES)
    one_minus_f = (1.0 - lower_bound) * jax.nn.sigmoid(-f_logit.astype(jnp.float32))
    log_f = jnp.log1p(-one_minus_f)
    o = chunked_gated_linear_attention(
        split_heads(jax.nn.silu(q), HG_HEADS), split_heads(one_minus_f, HG_HEADS),
        split_heads(i, HG_HEADS), split_heads(log_f, HG_HEADS))
    return head_rms_norm(o, norm_w) * jax.nn.silu(g)


def rwkv7_branch(z, mu, w0, w2, a0, a2, g2, k_k, k_a, r_k, ln_w, ln_b):
    f32 = jnp.float32
    r, k, v, lw, la, lg = split_cols(shift_lerp(z, mu), RW_SIZES)
    w_raw = -jax.nn.softplus(-(w0 + jnp.tanh(lw) @ w2).astype(f32)) - 0.5
    decay = jnp.exp(-jnp.exp(w_raw))
    iclr = jax.nn.sigmoid((a0 + la @ a2).astype(f32))
    gate = jax.nn.sigmoid(lg) @ g2
    kk = split_heads((k * k_k).astype(f32), RW_HEADS)
    kk = kk / jnp.maximum(jnp.linalg.norm(kk, axis=-1, keepdims=True), 1e-12)
    k_mod = k.astype(f32) * (1.0 + (iclr - 1.0) * k_a)
    r_h = split_heads(r.astype(f32), RW_HEADS)
    k_h = split_heads(k_mod, RW_HEADS)
    v_h = split_heads(v.astype(f32), RW_HEADS)
    o = rwkv7_recurrence(r_h, split_heads(decay, RW_HEADS), k_h, v_h,
                         -kk, kk * split_heads(iclr, RW_HEADS))
    o = head_group_norm(o, ln_w, ln_b, RW_GN_EPS)
    bonus = merge_heads(jnp.sum(r_h * k_h * r_k.reshape(RW_HEADS, 1, RW_DIM), axis=-1, keepdims=True) * v_h)
    return ((o + bonus) * gate).astype(z.dtype)


def stick_breaking_branch(z):
    q, k, v = split_cols(z, SB_SIZES)
    o = stick_breaking_attention(split_heads(q, SB_HEADS), split_heads(k, SB_HEADS), split_heads(v, SB_HEADS))
    return merge_heads(o)


def gla_branch(z, g2, gb, norm_w):
    q, k, v, lg, r = split_cols(z, GLA_SIZES)
    log_alpha = jax.nn.log_sigmoid((lg @ g2 + gb).astype(jnp.float32)) / GLA_TAU
    o = chunked_gated_linear_attention(
        split_heads(q, GLA_HEADS) * GLA_DK ** -0.5, split_heads(k, GLA_HEADS),
        split_heads(v, GLA_HEADS), split_heads(log_alpha, GLA_HEADS))
    return head_rms_norm(o, norm_w) * jax.nn.silu(r)


def hybrid_token_mixing(x, w_in, lower_bound, hg_norm_w, rw_mu, rw_w0, rw_w2, rw_a0, rw_a2,
                        rw_g2, rw_kk, rw_ka, rw_rk, rw_ln_w, rw_ln_b, gla_g2, gla_gb,
                        gla_norm_w, w_br, w_out):
    z = x @ w_in
    z_hg, z_rw, z_sb, z_gla, gate_logits = split_cols(z, BRANCH_SIZES)
    branch_outs = (
        hgrn2_branch(z_hg, lower_bound, hg_norm_w),
        rwkv7_branch(z_rw, rw_mu, rw_w0, rw_w2, rw_a0, rw_a2, rw_g2, rw_kk, rw_ka, rw_rk, rw_ln_w, rw_ln_b),
        stick_breaking_branch(z_sb),
        gla_branch(z_gla, gla_g2, gla_gb, gla_norm_w),
    )
    merged = jnp.zeros_like(x)
    for g in range(N_BRANCHES):
        gate = jax.nn.sigmoid(gate_logits[..., g * D_MODEL:(g + 1) * D_MODEL])
        merged = merged + gate * (branch_outs[g] @ w_br[g])
    return (merged @ w_out).astype(x.dtype)


def moe_ffn(x, w_router, router_bias, we_gate, we_up, we_down, ws_gate, ws_up, ws_down):
    b, t, d = x.shape
    n_tok = b * t
    xt = x.reshape(n_tok, d)
    f32 = jnp.float32
    scores = jax.nn.sigmoid(jnp.dot(xt.astype(f32), w_router.astype(f32)))
    biased = scores + router_bias.astype(f32)
    group_score = lax.top_k(biased.reshape(n_tok, N_GROUPS, N_EXPERTS // N_GROUPS), 2)[0].sum(-1)
    _, top_groups = lax.top_k(group_score, TOPK_GROUPS)
    group_ok = jnp.any(top_groups[:, :, None] == jnp.arange(N_GROUPS)[None, None, :], axis=1)
    expert_ok = jnp.repeat(group_ok, N_EXPERTS // N_GROUPS, axis=1)
    _, idx = lax.top_k(jnp.where(expert_ok, biased, -jnp.inf), TOP_K)
    w_sel = jnp.take_along_axis(scores, idx, axis=1)
    w_sel = ROUTED_SCALE * w_sel / jnp.sum(w_sel, axis=1, keepdims=True)

    n_slots = n_tok * TOP_K
    flat_e = idx.reshape(-1)
    order = jnp.argsort(flat_e)
    e_sorted = flat_e[order]
    tok_sorted = (order // TOP_K).astype(jnp.int32)
    w_sorted = w_sel.reshape(-1)[order]
    counts = jnp.bincount(flat_e, length=N_EXPERTS)
    padded = (counts + DISPATCH_BLOCK - 1) // DISPATCH_BLOCK * DISPATCH_BLOCK
    pad_end = jnp.cumsum(padded)
    pad_start = pad_end - padded
    start = jnp.cumsum(counts) - counts
    dest = pad_start[e_sorted] + jnp.arange(n_slots) - start[e_sorted]
    n_blocks = -(-n_slots // DISPATCH_BLOCK) + N_EXPERTS
    rows = n_blocks * DISPATCH_BLOCK
    row_tok = jnp.zeros((rows,), jnp.int32).at[dest].set(tok_sorted)
    row_w = jnp.zeros((rows,), f32).at[dest].set(w_sorted)
    block_expert = jnp.minimum(
        jnp.searchsorted(pad_end, jnp.arange(n_blocks) * DISPATCH_BLOCK, side='right'), N_EXPERTS - 1)

    def block_step(acc, inp):
        tok_b, w_b, e = inp
        xb = xt[tok_b]
        hb = jax.nn.silu(xb @ we_gate[e]) * (xb @ we_up[e])
        yb = (hb @ we_down[e]) * w_b[:, None].astype(xt.dtype)
        return acc.at[tok_b].add(yb.astype(acc.dtype)), None

    routed, _ = lax.scan(block_step, jnp.zeros_like(xt),
                         (row_tok.reshape(n_blocks, DISPATCH_BLOCK),
                          row_w.reshape(n_blocks, DISPATCH_BLOCK), block_expert))
    shared = (jax.nn.silu(xt @ ws_gate) * (xt @ ws_up)) @ ws_down
    return (routed + shared).reshape(b, t, d).astype(x.dtype)


def setup_inputs(seed: int = 0) -> dict:
    key = jax.random.key(seed)
    ks = iter(jax.random.split(key, 48))
    L, D, W, E, F = DEPTH, D_MODEL, BRANCH_W, N_EXPERTS, EXPERT_W

    def nrm(shape, scale):
        return scale * jax.random.normal(next(ks), shape, jnp.float32)

    def gain(shape):
        return 1.0 + nrm(shape, 0.05)

    return {
        'x': nrm((BATCH, SEQ, D), 1.0),
        'w_in': nrm((L, D, IN_WIDTH), D ** -0.5),
        'hg_lb_logits': nrm((L, W), 0.5),
        'hg_norm_w': gain((L, W)),
        'rw_mu': jax.random.uniform(next(ks), (L, sum(RW_SIZES)), jnp.float32),
        'rw_w0': -2.0 + nrm((L, W), 0.5),
        'rw_w2': nrm((L, RW_DECAY_RANK, W), RW_DECAY_RANK ** -0.5),
        'rw_a0': nrm((L, W), 0.1),
        'rw_a2': nrm((L, RW_ICLR_RANK, W), RW_ICLR_RANK ** -0.5),
        'rw_g2': nrm((L, RW_GATE_RANK, W), RW_GATE_RANK ** -0.5),
        'rw_kk': 0.85 + nrm((L, W), 0.05),
        'rw_ka': 1.0 + nrm((L, W), 0.05),
        'rw_rk': nrm((L, W), 0.1),
        'rw_ln_w': gain((L, W)),
        'rw_ln_b': nrm((L, W), 0.02),
        'gla_g2': nrm((L, GLA_GATE_RANK, GLA_HEADS * GLA_DK), GLA_GATE_RANK ** -0.5),
        'gla_gb': nrm((L, GLA_HEADS * GLA_DK), 0.1),
        'gla_norm_w': gain((L, W)),
        'w_br': nrm((L, N_BRANCHES, W, D), W ** -0.5),
        'w_out': nrm((L, D, D), DEEPNORM_BETA * D ** -0.5),
        'ln1_w': gain((L, D)),
        'ln1_b': nrm((L, D), 0.02),
        'w_router': nrm((L, D, E), D ** -0.5),
        'router_bias': nrm((L, E), 0.01),
        'we_gate': nrm((L, E, D, F), D ** -0.5),
        'we_up': nrm((L, E, D, F), D ** -0.5),
        'we_down': nrm((L, E, F, D), DEEPNORM_BETA * F ** -0.5),
        'ws_gate': nrm((L, D, F), D ** -0.5),
        'ws_up': nrm((L, D, F), D ** -0.5),
        'ws_down': nrm((L, F, D), DEEPNORM_BETA * F ** -0.5),
        'ln2_w': gain((L, D)),
        'ln2_b': nrm((L, D), 0.02),
    }


def reference(x, w_in, hg_lb_logits, hg_norm_w, rw_mu, rw_w0, rw_w2, rw_a0, rw_a2, rw_g2,
              rw_kk, rw_ka, rw_rk, rw_ln_w, rw_ln_b, gla_g2, gla_gb, gla_norm_w, w_br, w_out,
              ln1_w, ln1_b, w_router, router_bias, we_gate, we_up, we_down, ws_gate, ws_up,
              ws_down, ln2_w, ln2_b):
    p = jax.nn.softmax(hg_lb_logits.astype(jnp.float32), axis=0)
    cum = jnp.cumsum(p, axis=0)
    lower_bounds = cum - cum[0:1]
    for l in range(DEPTH):
        mix = hybrid_token_mixing(x, w_in[l], lower_bounds[l], hg_norm_w[l], rw_mu[l], rw_w0[l],
                                  rw_w2[l], rw_a0[l], rw_a2[l], rw_g2[l], rw_kk[l], rw_ka[l],
                                  rw_rk[l], rw_ln_w[l], rw_ln_b[l], gla_g2[l], gla_gb[l],
                                  gla_norm_w[l], w_br[l], w_out[l])
        x = layer_norm(DEEPNORM_ALPHA * x + mix, ln1_w[l], ln1_b[l])
        ffn = moe_ffn(x, w_router[l], router_bias[l], we_gate[l], we_up[l], we_down[l],
                      ws_gate[l], ws_up[l], ws_down[l])
        x = layer_norm(DEEPNORM_ALPHA * x + ffn, ln2_w[l], ln2_b[l])
    return x
```

```python
import functools

import jax
import jax.numpy as jnp
import numpy as np
from jax import lax
from jax.experimental import pallas as pl
from jax.experimental.pallas import tpu as pltpu

F32 = jnp.float32
BF16 = jnp.bfloat16
I32 = jnp.int32
U32 = jnp.uint32
HI = lax.Precision.HIGHEST

D_MODEL = 2048
DEPTH = 2
BRANCH_W = 512
N_BRANCHES = 4
CHUNK = 64
SUB = 16
N_SUB = CHUNK // SUB
LANES = 128
HEAD_PAD = 128
RW_HEADS = 8
RW_DIM = 64
RW_PAIRS = RW_HEADS // 2
GLA_DK = 64
GLA_TAU = 16.0
SB_DIM = 128
SB_BLOCK = 256
SB_SKIP = 120.0
N_EXPERTS = 64
TOP_K = 8
N_GROUPS = 8
TOPK_GROUPS = 4
EXPERT_W = 512
ROUTED_SCALE = 2.5
ROW_BLOCK = 256
DEEPNORM_ALPHA = (2 * DEPTH) ** 0.25
LN_EPS = 1e-5
RW_GN_EPS = 64e-5
NEG_BIG = -1e30

HG_OFF = 0
RW_OFF = 2048
SB_OFF = 3744
GLA_OFF = 5280
GATE_OFF = 6832
RW_Z = 1920
GLA_Z = 2176

_NT = (((1,), (1,)), ((), ()))
_TN = (((0,), (0,)), ((), ()))


def _cparams(sem, vmem_mb=48):
    return pltpu.CompilerParams(dimension_semantics=sem, vmem_limit_bytes=vmem_mb << 20)


def _sigmoid(x):
    return 1.0 / (1.0 + jnp.exp(-x))


def _log_sigmoid(x):
    return jnp.minimum(x, 0.0) - jnp.log1p(jnp.exp(-jnp.abs(x)))


def _silu(x):
    return x * _sigmoid(x)


def _layer_norm(y, w, b):
    mu = jnp.mean(y, axis=-1, keepdims=True)
    d = y - mu
    var = jnp.mean(d * d, axis=-1, keepdims=True)
    return d * lax.rsqrt(var + LN_EPS) * w + b


def _pack_pair(lo, hi):
    lo_b = lax.bitcast_convert_type(lo.astype(BF16).astype(F32), U32) >> 16
    hi_b = lax.bitcast_convert_type(hi.astype(BF16).astype(F32), U32) & jnp.uint32(0xFFFF0000)
    return lo_b | hi_b


def _unpack_pair(u):
    lo = lax.bitcast_convert_type(u << 16, F32)
    hi = lax.bitcast_convert_type(u & jnp.uint32(0xFFFF0000), F32)
    return lo, hi


def _dot2(x, ones_bf16):
    hi = x.astype(BF16)
    lo = (x - hi.astype(F32)).astype(BF16)
    return (jnp.dot(hi, ones_bf16, preferred_element_type=F32)
            + jnp.dot(lo, ones_bf16, preferred_element_type=F32))


def _mm_body(x_ref, w_ref, o_ref, *, act):
    acc = jnp.dot(x_ref[...], w_ref[...], preferred_element_type=F32)
    if act == "sigmoid":
        acc = _sigmoid(acc)
    o_ref[...] = acc.astype(o_ref.dtype)


def _matmul(x, w, out_dtype, act=None, tm=512, tn=None):
    m, k = x.shape
    n = w.shape[1]
    tm = min(tm, m)
    tn = n if tn is None else tn
    return pl.pallas_call(
        functools.partial(_mm_body, act=act),
        grid=(m // tm, n // tn),
        in_specs=[pl.BlockSpec((tm, k), lambda i, j: (i, 0)),
                  pl.BlockSpec((k, tn), lambda i, j: (0, j))],
        out_specs=pl.BlockSpec((tm, tn), lambda i, j: (i, j)),
        out_shape=jax.ShapeDtypeStruct((m, n), out_dtype),
        compiler_params=_cparams(("parallel", "parallel")),
        name="proj",
    )(x, w)


def _gated_chunk(q, k, v, g, st, tril):
    b = jnp.dot(tril, g, precision=HI, preferred_element_type=F32)
    blast = b[CHUNK - 1:CHUNK, :]
    vb = v.astype(BF16)
    qe = (q * jnp.exp(b)).astype(BF16)
    o = lax.dot_general(qe, st.astype(BF16), _NT, preferred_element_type=F32)
    o_parts = [o[i * SUB:(i + 1) * SUB] for i in range(N_SUB)]
    ends = [b[(j + 1) * SUB - 1:(j + 1) * SUB, :] for j in range(N_SUB)]
    khat = [k[j * SUB:(j + 1) * SUB] * jnp.exp(ends[j] - b[j * SUB:(j + 1) * SUB]) for j in range(N_SUB)]

    for j in range(N_SUB - 1):
        lo = (j + 1) * SUB
        qs = (q[lo:] * jnp.exp(b[lo:] - ends[j])).astype(BF16)
        a = lax.dot_general(qs, khat[j].astype(BF16), _NT, preferred_element_type=F32)
        pv = jnp.dot(a.astype(BF16), vb[j * SUB:(j + 1) * SUB], preferred_element_type=F32)
        for i in range(j + 1, N_SUB):
            o_parts[i] = o_parts[i] + pv[(i - j - 1) * SUB:(i - j) * SUB]

    lane = lax.broadcasted_iota(I32, (SUB, LANES), 1)
    trow = lax.broadcasted_iota(I32, (SUB, 1), 0)
    for i in range(N_SUB):
        bi = b[i * SUB:(i + 1) * SUB]
        qi = q[i * SUB:(i + 1) * SUB]
        ki = k[i * SUB:(i + 1) * SUB]
        d = jnp.zeros((SUB, LANES), F32)
        for s in range(SUB):
            e = jnp.exp(jnp.where(trow >= s, bi - bi[s:s + 1, :], NEG_BIG))
            col = jnp.sum(qi * e * ki[s:s + 1, :], axis=-1, keepdims=True)
            d = jnp.where(lane == s, col, d)
        pv = jnp.dot(d[:, :SUB].astype(BF16), vb[i * SUB:(i + 1) * SUB], preferred_element_type=F32)
        o_parts[i] = o_parts[i] + pv

    o = jnp.concatenate(o_parts, axis=0)
    kd = jnp.concatenate([khat[j] * jnp.exp(blast - ends[j]) for j in range(N_SUB)], axis=0)
    st_new = st * jnp.exp(blast) + lax.dot_general(vb, kd.astype(BF16), _TN, preferred_element_type=F32)
    return o, st_new


def _gated_body(z_ref, aux_ref, nw_ref, g2_ref, tril_ref, o_ref, st_ref, *, mode, rows):
    @pl.when(pl.program_id(0) == 0)
    def _():
        st_ref[...] = jnp.zeros_like(st_ref)

    tril = tril_ref[...]

    def chunk(c, carry):
        r0 = pl.multiple_of(c * CHUNK, CHUNK)
        rs = pl.ds(r0, CHUNK)
        if mode == "gla":
            lg = z_ref[rs, 4 * BRANCH_W:4 * BRANCH_W + LANES]
        for h in range(4):
            sl = slice(h * HEAD_PAD, (h + 1) * HEAD_PAD)
            zq = z_ref[rs, h * HEAD_PAD:(h + 1) * HEAD_PAD]
            zk = z_ref[rs, BRANCH_W + h * HEAD_PAD:BRANCH_W + (h + 1) * HEAD_PAD]
            v = z_ref[rs, 2 * BRANCH_W + h * HEAD_PAD:2 * BRANCH_W + (h + 1) * HEAD_PAD]
            gate = z_ref[rs, 3 * BRANCH_W + h * HEAD_PAD:3 * BRANCH_W + (h + 1) * HEAD_PAD]
            if mode == "hgrn2":
                q = _silu(zq)
                k = (1.0 - aux_ref[0:1, sl]) * _sigmoid(-zk)
                g = jnp.log1p(-k)
            else:
                q = zq * (GLA_DK ** -0.5)
                k = zk
                la = jnp.dot(lg, g2_ref[:, sl], precision=HI, preferred_element_type=F32) + aux_ref[0:1, sl]
                g = _log_sigmoid(la) * (1.0 / GLA_TAU)
            o, st_new = _gated_chunk(q, k, v, g, st_ref[h], tril)
            st_ref[h] = st_new
            o = o * lax.rsqrt(jnp.mean(o * o, axis=-1, keepdims=True) + LN_EPS)
            o = o * nw_ref[0:1, sl] * _silu(gate)
            o_ref[rs, sl] = o.astype(o_ref.dtype)
        return carry

    lax.fori_loop(0, rows // CHUNK, chunk, 0)


def _gated_mixer(z, aux, norm_w, g2, mode, rows=256):
    t, wz = z.shape
    rows = min(rows, t)
    tril = jnp.asarray(np.tril(np.ones((CHUNK, CHUNK), np.float32)))
    return pl.pallas_call(
        functools.partial(_gated_body, mode=mode, rows=rows),
        grid=(t // rows,),
        in_specs=[pl.BlockSpec((rows, wz), lambda i: (i, 0)),
                  pl.BlockSpec((1, BRANCH_W), lambda i: (0, 0)),
                  pl.BlockSpec((1, BRANCH_W), lambda i: (0, 0)),
                  pl.BlockSpec(g2.shape, lambda i: (0, 0)),
                  pl.BlockSpec((CHUNK, CHUNK), lambda i: (0, 0))],
        out_specs=pl.BlockSpec((rows, BRANCH_W), lambda i: (i, 0)),
        out_shape=jax.ShapeDtypeStruct((t, BRANCH_W), BF16),
        scratch_shapes=[pltpu.VMEM((4, HEAD_PAD, HEAD_PAD), F32)],
        compiler_params=_cparams(("arbitrary",)),
        name="gated_" + mode,
    )(z, aux, norm_w, g2, tril)


def _rwkv_body(z_ref, mu_ref, vec_ref, w2_ref, a2_ref, g2_ref, tril_ref, ones2_ref, o_ref,
               prev_ref, st_ref, *, rows):
    @pl.when(pl.program_id(0) == 0)
    def _():
        prev_ref[...] = jnp.zeros_like(prev_ref)
        st_ref[...] = jnp.zeros_like(st_ref)

    tril = tril_ref[...]
    ones2 = ones2_ref[...]
    row128 = lax.broadcasted_iota(I32, (2 * CHUNK, 2 * CHUNK), 0)
    col128 = lax.broadcasted_iota(I32, (2 * CHUNK, 2 * CHUNK), 1)
    rt = jnp.where(row128 >= CHUNK, row128 - CHUNK, row128)
    ct = jnp.where(col128 >= CHUNK, col128 - CHUNK, col128)
    strict = rt > ct
    incl = rt >= ct
    eye = row128 == col128
    head0 = lax.broadcasted_iota(I32, (1, LANES), 1) < RW_DIM
    first_row = lax.broadcasted_iota(I32, (CHUNK, 1), 0) == 0

    def stack(x):
        return jnp.concatenate([jnp.where(head0, x, 0.0), jnp.where(head0, 0.0, x)], axis=0)

    def chunk(c, carry):
        r0 = pl.multiple_of(c * CHUNK, CHUNK)
        rs = pl.ds(r0, CHUNK)
        z = z_ref[rs, :]
        zprev = jnp.where(first_row, prev_ref[...], pltpu.roll(z, 1, axis=0))
        prev_ref[...] = z[CHUNK - 1:CHUNK, :]
        zs = z + (zprev - z) * mu_ref[...]
        lw = zs[:, 3 * BRANCH_W:3 * BRANCH_W + LANES]
        la = zs[:, 3 * BRANCH_W + LANES:3 * BRANCH_W + 2 * LANES]
        lg = zs[:, 3 * BRANCH_W + 2 * LANES:3 * BRANCH_W + 3 * LANES]
        wl = -(vec_ref[0:1, :] + jnp.dot(jnp.tanh(lw), w2_ref[...], precision=HI, preferred_element_type=F32))
        w_raw = -(jnp.maximum(wl, 0.0) + jnp.log1p(jnp.exp(-jnp.abs(wl)))) - 0.5
        logw_all = -jnp.exp(w_raw)
        iclr_all = _sigmoid(vec_ref[1:2, :] + jnp.dot(la, a2_ref[...], precision=HI, preferred_element_type=F32))
        gate_all = jnp.dot(_sigmoid(lg), g2_ref[...], precision=HI, preferred_element_type=F32)
        cum_all = jnp.dot(tril, logw_all, precision=HI, preferred_element_type=F32)

        for p in range(RW_PAIRS):
            sl = slice(p * LANES, (p + 1) * LANES)
            r = zs[:, p * LANES:(p + 1) * LANES]
            k = zs[:, BRANCH_W + p * LANES:BRANCH_W + (p + 1) * LANES]
            v = zs[:, 2 * BRANCH_W + p * LANES:2 * BRANCH_W + (p + 1) * LANES]
            logw, iclr, cum = logw_all[:, sl], iclr_all[:, sl], cum_all[:, sl]
            kkr = k * vec_ref[2:3, sl]
            kk = kkr / jnp.maximum(jnp.sqrt(_dot2(kkr * kkr, ones2)), 1e-12)
            kmod = k * (1.0 + (iclr - 1.0) * vec_ref[3:4, sl])
            a = -kk
            bb = kk * iclr
            clast = cum[CHUNK - 1:CHUNK, :]
            e_neg = jnp.exp(-cum)
            e_end = jnp.exp(clast - cum)
            a_s = stack(a * jnp.exp(cum - logw))
            r_s = stack(r * jnp.exp(cum))
            b_s = stack(bb * e_neg)
            k_s = stack(kmod * e_neg)
            bh_s = stack(bb * e_end)
            kh_s = stack(kmod * e_end)
            v_s = stack(v)

            left = jnp.concatenate([a_s, r_s], axis=0).astype(BF16)
            right = jnp.concatenate([b_s, k_s], axis=0).astype(BF16)
            gram = lax.dot_general(left, right, _NT, preferred_element_type=F32)
            c2 = 2 * CHUNK
            n_ab = jnp.where(strict, gram[:c2, :c2], 0.0)
            a_ak = jnp.where(strict, gram[:c2, c2:], 0.0)
            g_b = jnp.where(incl, gram[c2:, :c2], 0.0)
            g_k = jnp.where(incl, gram[c2:, c2:], 0.0)

            x = jnp.where(eye, 1.0, 0.0) + n_ab
            pw = n_ab
            for _ in range(5):
                pwb = pw.astype(BF16)
                pw = jnp.dot(pwb, pwb, preferred_element_type=F32)
                x = x + jnp.dot(pw.astype(BF16), x.astype(BF16), preferred_element_type=F32)

            v_sb = v_s.astype(BF16)
            av = jnp.dot(a_ak.astype(BF16), v_sb, preferred_element_type=F32)
            pq = jnp.dot(x.astype(BF16), jnp.concatenate([a_s, av], axis=1).astype(BF16),
                         preferred_element_type=F32)
            pqb = pq.astype(BF16)
            lower = jnp.concatenate([jnp.zeros((c2, LANES), BF16), v_sb], axis=1)
            top = jnp.dot(jnp.concatenate([g_b, g_k], axis=1).astype(BF16),
                          jnp.concatenate([pqb, lower], axis=0), preferred_element_type=F32)
            r2 = r_s + top[:, :LANES]
            o2 = top[:, LANES:]
            bot = lax.dot_general(bh_s.astype(BF16), pqb, _TN, preferred_element_type=F32)
            kv = lax.dot_general(kh_s.astype(BF16), v_sb, _TN, preferred_element_type=F32)
            tm = bot[:, :LANES] + jnp.where(eye, jnp.exp(clast), 0.0)
            sadd = bot[:, LANES:] + kv

            s_old = st_ref[p].astype(BF16)
            o_st = jnp.dot(r2.astype(BF16), s_old, preferred_element_type=F32) + o2
            st_ref[p] = jnp.dot(tm.astype(BF16), s_old, preferred_element_type=F32) + sadd
            o = o_st[:CHUNK] + o_st[CHUNK:]

            inv_n = 1.0 / RW_DIM
            mean = _dot2(o, ones2) * inv_n
            dlt = o - mean
            var = _dot2(dlt * dlt, ones2) * inv_n
            on = dlt * lax.rsqrt(var + RW_GN_EPS) * vec_ref[5:6, sl] + vec_ref[6:7, sl]
            bonus = _dot2(r * kmod * vec_ref[4:5, sl], ones2) * v
            o_ref[rs, sl] = ((on + bonus) * gate_all[:, sl]).astype(o_ref.dtype)
        return carry

    lax.fori_loop(0, rows // CHUNK, chunk, 0)


def _rwkv_mixer(z, mu, vecs, w2, a2, g2, rows=256):
    t = z.shape[0]
    rows = min(rows, t)
    tril = jnp.asarray(np.tril(np.ones((CHUNK, CHUNK), np.float32)))
    hid = np.arange(LANES) // RW_DIM
    ones2 = jnp.asarray((hid[:, None] == hid[None, :]).astype(np.float32), dtype=BF16)
    const = lambda i: (0, 0)
    return pl.pallas_call(
        functools.partial(_rwkv_body, rows=rows),
        grid=(t // rows,),
        in_specs=[pl.BlockSpec((rows, RW_Z), lambda i: (i, 0)),
                  pl.BlockSpec((1, RW_Z), const),
                  pl.BlockSpec((8, BRANCH_W), const),
                  pl.BlockSpec((LANES, BRANCH_W), const),
                  pl.BlockSpec((LANES, BRANCH_W), const),
                  pl.BlockSpec((LANES, BRANCH_W), const),
                  pl.BlockSpec((CHUNK, CHUNK), const),
                  pl.BlockSpec((LANES, LANES), const)],
        out_specs=pl.BlockSpec((rows, BRANCH_W), lambda i: (i, 0)),
        out_shape=jax.ShapeDtypeStruct((t, BRANCH_W), BF16),
        scratch_shapes=[pltpu.VMEM((1, RW_Z), F32), pltpu.VMEM((RW_PAIRS, LANES, LANES), F32)],
        compiler_params=_cparams(("arbitrary",)),
        name="rwkv7",
    )(z, mu, vecs, w2, a2, g2, tril, ones2)


def _sb_body(q_ref, k_ref, v_ref, upper_ref, o_ref, acc_ref, car_ref):
    i = pl.program_id(1)
    q = q_ref[...]
    upper = upper_ref[...]
    acc_ref[...] = jnp.zeros_like(acc_ref)
    car_ref[...] = jnp.zeros_like(car_ref)
    qpos = i * SB_BLOCK + lax.broadcasted_iota(I32, (SB_BLOCK, SB_BLOCK), 0)
    kidx = lax.broadcasted_iota(I32, (SB_BLOCK, SB_BLOCK), 1)

    def body(state):
        j, _ = state
        k0 = pl.multiple_of(j * SB_BLOCK, SB_BLOCK)
        kj = k_ref[pl.ds(k0, SB_BLOCK), :]
        vj = v_ref[pl.ds(k0, SB_BLOCK), :]
        z = lax.dot_general(q, kj, _NT, preferred_element_type=F32)
        strict = (kidx + j * SB_BLOCK) < qpos
        lk = jnp.where(strict, jnp.minimum(-z, 0.0) - jnp.log1p(jnp.exp(-jnp.abs(z))), 0.0)
        hi = lk.astype(BF16)
        lo = (lk - hi.astype(F32)).astype(BF16)
        later = (jnp.dot(hi, upper, preferred_element_type=F32)
                 + jnp.dot(lo, upper, preferred_element_type=F32))
        car = car_ref[...]
        w = jnp.where(strict, jnp.exp(z + lk + later + car), 0.0)
        acc_ref[...] += jnp.dot(w.astype(BF16), vj, preferred_element_type=F32)
        car = car + jnp.sum(lk, axis=-1, keepdims=True)
        car_ref[...] = car
        return j - 1, jnp.max(car)

    def cond(state):
        j, top = state
        return jnp.logical_and(j >= 0, top > -SB_SKIP)

    lax.while_loop(cond, body, (i, jnp.float32(0.0)))
    o_ref[...] = acc_ref[...].astype(o_ref.dtype)


def _sb_attention(q, k, v):
    t = q.shape[0]
    heads = q.shape[1] // SB_DIM
    upper = jnp.asarray(np.triu(np.ones((SB_BLOCK, SB_BLOCK), np.float32), 1).T, dtype=BF16)
    return pl.pallas_call(
        _sb_body,
        grid=(heads, t // SB_BLOCK),
        in_specs=[pl.BlockSpec((SB_BLOCK, SB_DIM), lambda h, i: (i, h)),
                  pl.BlockSpec((t, SB_DIM), lambda h, i: (0, h)),
                  pl.BlockSpec((t, SB_DIM), lambda h, i: (0, h)),
                  pl.BlockSpec((SB_BLOCK, SB_BLOCK), lambda h, i: (0, 0))],
        out_specs=pl.BlockSpec((SB_BLOCK, SB_DIM), lambda h, i: (i, h)),
        out_shape=jax.ShapeDtypeStruct((t, heads * SB_DIM), BF16),
        scratch_shapes=[pltpu.VMEM((SB_BLOCK, SB_DIM), F32), pltpu.VMEM((SB_BLOCK, 1), F32)],
        compiler_params=_cparams(("arbitrary", "arbitrary")),
        name="stick_breaking",
    )(q, k, v, upper)


def _merge_body(b0_ref, b1_ref, b2_ref, b3_ref, gate_ref, wbr_ref, wout_ref, x_ref, lnw_ref, lnb_ref,
                xo_ref, xb_ref, xp_ref):
    merged = None
    for g, b_ref in enumerate((b0_ref, b1_ref, b2_ref, b3_ref)):
        y = jnp.dot(b_ref[...], wbr_ref[g], preferred_element_type=F32)
        y = y * gate_ref[:, g * D_MODEL:(g + 1) * D_MODEL].astype(F32)
        merged = y if merged is None else merged + y
    mix = jnp.dot(merged.astype(BF16), wout_ref[...], preferred_element_type=F32)
    y = _layer_norm(DEEPNORM_ALPHA * x_ref[...] + mix, lnw_ref[...], lnb_ref[...])
    xo_ref[...] = y
    xb_ref[...] = y.astype(BF16)
    xp_ref[...] = _pack_pair(y[:, :D_MODEL // 2], y[:, D_MODEL // 2:])


def _merge(branches, gates, w_br, w_out, x, ln_w, ln_b, tm=256):
    t = x.shape[0]
    tm = min(tm, t)
    row = lambda i: (i, 0)
    const2 = lambda i: (0, 0)
    return pl.pallas_call(
        _merge_body,
        grid=(t // tm,),
        in_specs=[pl.BlockSpec((tm, BRANCH_W), row)] * 4 + [
            pl.BlockSpec((tm, N_BRANCHES * D_MODEL), row),
            pl.BlockSpec((N_BRANCHES, BRANCH_W, D_MODEL), lambda i: (0, 0, 0)),
            pl.BlockSpec((D_MODEL, D_MODEL), const2),
            pl.BlockSpec((tm, D_MODEL), row),
            pl.BlockSpec((1, D_MODEL), const2),
            pl.BlockSpec((1, D_MODEL), const2)],
        out_specs=[pl.BlockSpec((tm, D_MODEL), row), pl.BlockSpec((tm, D_MODEL), row),
                   pl.BlockSpec((tm, D_MODEL // 2), row)],
        out_shape=[jax.ShapeDtypeStruct((t, D_MODEL), F32), jax.ShapeDtypeStruct((t, D_MODEL), BF16),
                   jax.ShapeDtypeStruct((t, D_MODEL // 2), U32)],
        compiler_params=_cparams(("parallel",), 56),
        name="merge_ln",
    )(*branches, gates, w_br, w_out, x, ln_w, ln_b)


def _first_index(hit, idx, size, axis):
    return jnp.min(jnp.where(hit, idx, size), axis=axis, keepdims=True)


def _router_body(x_ref, wr_ref, bias_ref, before_ref, idx_ref, w_ref, rank_ref, cnt_ref, run_ref, *, tm):
    @pl.when(pl.program_id(0) == 0)
    def _():
        run_ref[...] = jnp.zeros_like(run_ref)

    per = N_EXPERTS // N_GROUPS
    logits = lax.dot_general(wr_ref[...], x_ref[...], _NT, precision=HI, preferred_element_type=F32)
    scores = _sigmoid(logits)
    biased = scores + bias_ref[:, 0:1]
    g3 = biased.reshape(N_GROUPS, per, tm)
    pos = lax.broadcasted_iota(I32, (N_GROUPS, per, tm), 1)
    m1 = jnp.max(g3, axis=1, keepdims=True)
    f1 = _first_index(g3 == m1, pos, per, 1)
    m2 = jnp.max(jnp.where(pos == f1, -jnp.inf, g3), axis=1, keepdims=True)
    gscore = (m1 + m2).reshape(N_GROUPS, tm)

    gpos = lax.broadcasted_iota(I32, (N_GROUPS, tm), 0)
    chosen = jnp.zeros((N_GROUPS, tm), F32)
    cur = gscore
    for _ in range(TOPK_GROUPS):
        m = jnp.max(cur, axis=0, keepdims=True)
        pick = gpos == _first_index(cur == m, gpos, N_GROUPS, 0)
        chosen = jnp.where(pick, 1.0, chosen)
        cur = jnp.where(pick, -jnp.inf, cur)
    ok = jnp.broadcast_to(chosen.reshape(N_GROUPS, 1, tm), (N_GROUPS, per, tm)).reshape(N_EXPERTS, tm)

    epos = lax.broadcasted_iota(I32, (N_EXPERTS, tm), 0)
    cur = jnp.where(ok > 0.5, biased, -jnp.inf)
    picks, idx_rows, w_rows = [], [], []
    member = jnp.zeros((N_EXPERTS, tm), F32)
    for _ in range(TOP_K):
        m = jnp.max(cur, axis=0, keepdims=True)
        f = _first_index(cur == m, epos, N_EXPERTS, 0)
        pick = epos == f
        picks.append(pick)
        idx_rows.append(f)
        w_rows.append(jnp.sum(jnp.where(pick, scores, 0.0), axis=0, keepdims=True))
        member = jnp.where(pick, 1.0, member)
        cur = jnp.where(pick, -jnp.inf, cur)
    w_sel = jnp.concatenate(w_rows, axis=0)
    w_ref[...] = ROUTED_SCALE * w_sel / jnp.sum(w_sel, axis=0, keepdims=True)
    idx_ref[...] = jnp.concatenate(idx_rows, axis=0)

    seen = jnp.dot(member.astype(BF16), before_ref[...], preferred_element_type=F32) + run_ref[:, 0:1]
    rank_rows = [jnp.sum(jnp.where(pk, seen, 0.0), axis=0, keepdims=True) for pk in picks]
    rank_ref[...] = jnp.concatenate(rank_rows, axis=0).astype(I32)
    run_ref[...] = run_ref[...] + jnp.sum(member, axis=1, keepdims=True)
    cnt_ref[...] = run_ref[...]


def _router(x, w_router_t, bias, tm=512):
    t = x.shape[0]
    tm = min(tm, t)
    before = jnp.asarray(np.triu(np.ones((tm, tm), np.float32), 1), dtype=BF16)
    slot = lambda i: (0, i)
    const = lambda i: (0, 0)
    return pl.pallas_call(
        functools.partial(_router_body, tm=tm),
        grid=(t // tm,),
        in_specs=[pl.BlockSpec((tm, D_MODEL), lambda i: (i, 0)),
                  pl.BlockSpec((N_EXPERTS, D_MODEL), const),
                  pl.BlockSpec((N_EXPERTS, LANES), const),
                  pl.BlockSpec((tm, tm), const)],
        out_specs=[pl.BlockSpec((TOP_K, tm), slot), pl.BlockSpec((TOP_K, tm), slot),
                   pl.BlockSpec((TOP_K, tm), slot), pl.BlockSpec((N_EXPERTS, LANES), const)],
        out_shape=[jax.ShapeDtypeStruct((TOP_K, t), I32), jax.ShapeDtypeStruct((TOP_K, t), F32),
                   jax.ShapeDtypeStruct((TOP_K, t), I32), jax.ShapeDtypeStruct((N_EXPERTS, LANES), F32)],
        scratch_shapes=[pltpu.VMEM((N_EXPERTS, LANES), F32)],
        compiler_params=_cparams(("arbitrary",)),
        name="router",
    )(x, w_router_t, bias, before)


def _dest_body(start_ref, idx_ref, rank_ref, dest_ref):
    idx = idx_ref[...]
    base = jnp.zeros(idx.shape, I32)
    for e in range(N_EXPERTS):
        base = jnp.where(idx == e, start_ref[e], base)
    dest_ref[...] = base + rank_ref[...]


def _dest_rows(pad_start, idx, rank, tm=2048):
    t = idx.shape[1]
    tm = min(tm, t)
    slot = lambda i, s: (0, i)
    return pl.pallas_call(
        _dest_body,
        grid_spec=pltpu.PrefetchScalarGridSpec(
            num_scalar_prefetch=1, grid=(t // tm,),
            in_specs=[pl.BlockSpec((TOP_K, tm), slot), pl.BlockSpec((TOP_K, tm), slot)],
            out_specs=pl.BlockSpec((TOP_K, tm), slot)),
        out_shape=jax.ShapeDtypeStruct((TOP_K, t), I32),
        compiler_params=_cparams(("parallel",)),
        name="dest_rows",
    )(pad_start, idx, rank)


def _dispatch_body(dest_ref, x_ref, zero_hbm, xs_hbm, sem, *, tm):
    del zero_hbm

    def row(t, carry):
        for k in range(TOP_K):
            pltpu.make_async_copy(x_ref.at[pl.ds(t, 1)], xs_hbm.at[pl.ds(dest_ref[k, t], 1)], sem).start()
        return carry

    lax.fori_loop(0, tm, row, 0)
    for _ in range(TOP_K):
        pltpu.make_async_copy(x_ref, xs_hbm.at[pl.ds(0, tm)], sem).wait()


def _dispatch(dest, x_packed, n_rows, tm=256):
    t, half = x_packed.shape
    tm = min(tm, t)
    zeros = jnp.zeros((n_rows, half), U32)
    return pl.pallas_call(
        functools.partial(_dispatch_body, tm=tm),
        grid=(t // tm,),
        in_specs=[pl.BlockSpec((TOP_K, tm), lambda i: (0, i), memory_space=pltpu.SMEM),
                  pl.BlockSpec((tm, half), lambda i: (i, 0)),
                  pl.BlockSpec(memory_space=pl.ANY)],
        out_specs=pl.BlockSpec(memory_space=pl.ANY),
        out_shape=jax.ShapeDtypeStruct((n_rows, half), U32),
        scratch_shapes=[pltpu.SemaphoreType.DMA(())],
        input_output_aliases={2: 0},
        compiler_params=_cparams(("arbitrary",)),
        name="dispatch",
    )(dest, x_packed, zeros)


def _experts_body(be_ref, nu_ref, xs_ref, wg_ref, wu_ref, wd_ref, ys_ref):
    del be_ref

    @pl.when(pl.program_id(0) < nu_ref[0])
    def _():
        half = D_MODEL // 2
        lo, hi = _unpack_pair(xs_ref[...])
        lo = lo.astype(BF16)
        hi = hi.astype(BF16)
        gate = (jnp.dot(lo, wg_ref[0, :half, :], preferred_element_type=F32)
                + jnp.dot(hi, wg_ref[0, half:, :], preferred_element_type=F32))
        up = (jnp.dot(lo, wu_ref[0, :half, :], preferred_element_type=F32)
              + jnp.dot(hi, wu_ref[0, half:, :], preferred_element_type=F32))
        h = (_silu(gate) * up).astype(BF16)
        y = jnp.dot(h, wd_ref[0], preferred_element_type=F32)
        ys_ref[...] = _pack_pair(y[:, :half], y[:, half:])

    @pl.when(pl.program_id(0) >= nu_ref[0])
    def _():
        ys_ref[...] = jnp.zeros_like(ys_ref)


def _experts(block_expert, n_used, xs, wg, wu, wd):
    n_rows, half = xs.shape
    n_blocks = n_rows // ROW_BLOCK
    blk = lambda b, be, nu: (jnp.minimum(b, nu[0] - 1), 0)
    out_blk = lambda b, be, nu: (b, 0)
    wsel = lambda b, be, nu: (be[jnp.minimum(b, nu[0] - 1)], 0, 0)
    return pl.pallas_call(
        _experts_body,
        grid_spec=pltpu.PrefetchScalarGridSpec(
            num_scalar_prefetch=2, grid=(n_blocks,),
            in_specs=[pl.BlockSpec((ROW_BLOCK, half), blk),
                      pl.BlockSpec((1, D_MODEL, EXPERT_W), wsel),
                      pl.BlockSpec((1, D_MODEL, EXPERT_W), wsel),
                      pl.BlockSpec((1, EXPERT_W, D_MODEL), wsel)],
            out_specs=pl.BlockSpec((ROW_BLOCK, half), out_blk)),
        out_shape=jax.ShapeDtypeStruct((n_rows, half), U32),
        compiler_params=_cparams(("arbitrary",)),
        name="experts",
    )(block_expert, n_used, xs, wg, wu, wd)


def _combine_body(dest_ref, w_ref, x_ref, xb_ref, sg_ref, su_ref, sd_ref, lnw_ref, lnb_ref, ys_hbm,
                  xo_ref, xb_out_ref, buf_ref, sem, *, tm):
    def row(t, carry):
        for k in range(TOP_K):
            pltpu.make_async_copy(ys_hbm.at[pl.ds(dest_ref[k, t], 1)], buf_ref.at[k, pl.ds(t, 1)], sem).start()
        return carry

    lax.fori_loop(0, tm, row, 0)

    xb = xb_ref[...]
    h = _silu(jnp.dot(xb, sg_ref[...], preferred_element_type=F32)) * jnp.dot(xb, su_ref[...],
                                                                           preferred_element_type=F32)
    shared = jnp.dot(h.astype(BF16), sd_ref[...], preferred_element_type=F32)

    for k in range(TOP_K):
        pltpu.make_async_copy(ys_hbm.at[pl.ds(0, tm)], buf_ref.at[k], sem).wait()

    half = D_MODEL // 2
    wt = w_ref[...].T
    lo_acc = jnp.zeros((tm, half), F32)
    hi_acc = jnp.zeros((tm, half), F32)
    for k in range(TOP_K):
        lo, hi = _unpack_pair(buf_ref[k])
        wk = wt[:, k:k + 1]
        lo_acc = lo_acc + wk * lo
        hi_acc = hi_acc + wk * hi
    routed = jnp.concatenate([lo_acc, hi_acc], axis=1)
    y = _layer_norm(DEEPNORM_ALPHA * x_ref[...] + routed + shared, lnw_ref[...], lnb_ref[...])
    xo_ref[...] = y
    xb_out_ref[...] = y.astype(BF16)


def _combine(dest, w_sel, x, xb, sg, su, sd, ln_w, ln_b, ys, tm=128):
    t = x.shape[0]
    tm = min(tm, t)
    half = D_MODEL // 2
    row = lambda i: (i, 0)
    slot = lambda i: (0, i)
    const = lambda i: (0, 0)
    return pl.pallas_call(
        functools.partial(_combine_body, tm=tm),
        grid=(t // tm,),
        in_specs=[pl.BlockSpec((TOP_K, tm), slot, memory_space=pltpu.SMEM),
                  pl.BlockSpec((TOP_K, tm), slot),
                  pl.BlockSpec((tm, D_MODEL), row),
                  pl.BlockSpec((tm, D_MODEL), row),
                  pl.BlockSpec((D_MODEL, EXPERT_W), const),
                  pl.BlockSpec((D_MODEL, EXPERT_W), const),
                  pl.BlockSpec((EXPERT_W, D_MODEL), const),
                  pl.BlockSpec((1, D_MODEL), const),
                  pl.BlockSpec((1, D_MODEL), const),
                  pl.BlockSpec(memory_space=pl.ANY)],
        out_specs=[pl.BlockSpec((tm, D_MODEL), row), pl.BlockSpec((tm, D_MODEL), row)],
        out_shape=[jax.ShapeDtypeStruct((t, D_MODEL), F32), jax.ShapeDtypeStruct((t, D_MODEL), BF16)],
        scratch_shapes=[pltpu.VMEM((TOP_K, tm, half), U32), pltpu.SemaphoreType.DMA(())],
        compiler_params=_cparams(("arbitrary",)),
        name="combine_ln",
    )(dest, w_sel, x, xb, sg, su, sd, ln_w, ln_b, ys)


def _pad_cols(a, width):
    return jnp.pad(a, ((0, 0), (0, width - a.shape[1])))


def _pad_rows(a, height):
    return jnp.pad(a, ((0, height - a.shape[0]), (0, 0)))


def _pad_heads(a, heads, dk):
    r = a.shape[0]
    return jnp.pad(a.reshape(r, heads, dk), ((0, 0), (0, 0), (0, HEAD_PAD - dk))).reshape(r, heads * HEAD_PAD)


def _token_mixing(xb, w_in, lower_bound, hg_norm_w, rw_mu, rw_w0, rw_w2, rw_a0, rw_a2, rw_g2, rw_kk, rw_ka,
                  rw_rk, rw_ln_w, rw_ln_b, gla_g2, gla_gb, gla_norm_w):
    row = lambda a: a.reshape(1, -1).astype(F32)

    z_hg = _matmul(xb, w_in[:, HG_OFF:RW_OFF].astype(BF16), F32)
    o_hg = _gated_mixer(z_hg, row(lower_bound), row(hg_norm_w), jnp.zeros((8, BRANCH_W), F32), "hgrn2")

    w_rw = w_in[:, RW_OFF:SB_OFF]
    parts = [w_rw[:, :1536], _pad_cols(w_rw[:, 1536:1568], LANES), _pad_cols(w_rw[:, 1568:1600], LANES),
             _pad_cols(w_rw[:, 1600:1696], LANES)]
    z_rw = _matmul(xb, jnp.concatenate(parts, axis=1).astype(BF16), F32)
    mu = jnp.concatenate([rw_mu[:1536], jnp.pad(rw_mu[1536:1568], (0, 96)), jnp.pad(rw_mu[1568:1600], (0, 96)),
                          jnp.pad(rw_mu[1600:1696], (0, 32))]).reshape(1, RW_Z)
    vecs = jnp.stack([rw_w0, rw_a0, rw_kk, rw_ka, rw_rk, rw_ln_w, rw_ln_b, jnp.zeros_like(rw_w0)]).astype(F32)
    o_rw = _rwkv_mixer(z_rw, mu, vecs, _pad_rows(rw_w2, LANES), _pad_rows(rw_a2, LANES), _pad_rows(rw_g2, LANES))

    w_sb = w_in[:, SB_OFF:GLA_OFF]
    w_sb = jnp.concatenate([w_sb[:, :BRANCH_W] * (SB_DIM ** -0.5), w_sb[:, BRANCH_W:]], axis=1)
    z_sb = _matmul(xb, w_sb.astype(BF16), BF16)
    o_sb = _sb_attention(z_sb[:, :BRANCH_W], z_sb[:, BRANCH_W:2 * BRANCH_W], z_sb[:, 2 * BRANCH_W:])

    w_gla = w_in[:, GLA_OFF:GATE_OFF]
    parts = [_pad_heads(w_gla[:, :256], 4, GLA_DK), _pad_heads(w_gla[:, 256:512], 4, GLA_DK),
             w_gla[:, 512:1024], w_gla[:, 1040:1552], _pad_cols(w_gla[:, 1024:1040], LANES)]
    z_gla = _matmul(xb, jnp.concatenate(parts, axis=1).astype(BF16), F32)
    g2p = _pad_rows(_pad_heads(gla_g2, 4, GLA_DK), LANES)
    gbp = _pad_heads(gla_gb.reshape(1, -1), 4, GLA_DK)
    o_gla = _gated_mixer(z_gla, gbp, row(gla_norm_w), g2p, "gla")

    gates = _matmul(xb, w_in[:, GATE_OFF:].astype(BF16), BF16, act="sigmoid", tm=1024, tn=1024)
    return (o_hg, o_rw, o_sb, o_gla), gates


def _moe(x, xb, xp, w_router, router_bias, we_gate, we_up, we_down, ws_gate, ws_up, ws_down, ln_w, ln_b):
    t = x.shape[0]
    bias = jnp.broadcast_to(router_bias.astype(F32).reshape(N_EXPERTS, 1), (N_EXPERTS, LANES))
    idx, w_sel, rank, counts = _router(x, w_router.T.astype(F32), bias)

    cnt = counts[:, 0].astype(I32)
    padded = (cnt + ROW_BLOCK - 1) // ROW_BLOCK * ROW_BLOCK
    pad_end = jnp.cumsum(padded)
    pad_start = (pad_end - padded).astype(I32)
    n_blocks = t * TOP_K // ROW_BLOCK + N_EXPERTS
    n_used = (pad_end[-1:] // ROW_BLOCK).astype(I32)
    first_row = jnp.arange(n_blocks, dtype=I32) * ROW_BLOCK
    block_expert = jnp.minimum(jnp.sum(pad_end[None, :] <= first_row[:, None], axis=1), N_EXPERTS - 1).astype(I32)

    dest = _dest_rows(pad_start, idx, rank)
    xs = _dispatch(dest, xp, n_blocks * ROW_BLOCK)
    ys = _experts(block_expert, n_used, xs, we_gate.astype(BF16), we_up.astype(BF16), we_down.astype(BF16))
    return _combine(dest, w_sel, x, xb, ws_gate.astype(BF16), ws_up.astype(BF16), ws_down.astype(BF16),
                    ln_w.reshape(1, -1), ln_b.reshape(1, -1), ys)


def kernel(x, w_in, hg_lb_logits, hg_norm_w, rw_mu, rw_w0, rw_w2, rw_a0, rw_a2, rw_g2, rw_kk, rw_ka, rw_rk,
           rw_ln_w, rw_ln_b, gla_g2, gla_gb, gla_norm_w, w_br, w_out, ln1_w, ln1_b, w_router, router_bias,
           we_gate, we_up, we_down, ws_gate, ws_up, ws_down, ln2_w, ln2_b):
    bsz, t, d = x.shape
    cum = jnp.cumsum(jax.nn.softmax(hg_lb_logits.astype(F32), axis=0), axis=0)
    lower_bounds = cum - cum[0:1]
    outs = []
    for bi in range(bsz):
        xf = x[bi].astype(F32)
        xb = xf.astype(BF16)
        for l in range(DEPTH):
            branches, gates = _token_mixing(
                xb, w_in[l], lower_bounds[l], hg_norm_w[l], rw_mu[l], rw_w0[l], rw_w2[l], rw_a0[l], rw_a2[l],
                rw_g2[l], rw_kk[l], rw_ka[l], rw_rk[l], rw_ln_w[l], rw_ln_b[l], gla_g2[l], gla_gb[l],
                gla_norm_w[l])
            xf, xb, xp = _merge(branches, gates, w_br[l].astype(BF16), w_out[l].astype(BF16), xf,
                                ln1_w[l].reshape(1, -1), ln1_b[l].reshape(1, -1))
            xf, xb = _moe(xf, xb, xp, w_router[l], router_bias[l], we_gate[l], we_up[l], we_down[l],
                          ws_gate[l], ws_up[l], ws_down[l], ln2_w[l], ln2_b[l])
        outs.append(xf)
    return jnp.stack(outs).astype(x.dtype)
```

```python
import functools

import jax
import jax.numpy as jnp
import numpy as np
from jax import lax
from jax.experimental import pallas as pl
from jax.experimental.pallas import tpu as pltpu

F32 = jnp.float32
BF16 = jnp.bfloat16
I32 = jnp.int32
U32 = jnp.uint32
HI = lax.Precision.HIGHEST

D_MODEL = 2048
DEPTH = 2
BRANCH_W = 512
N_BRANCHES = 4
CHUNK = 64
SUB = 16
N_SUB = CHUNK // SUB
LANES = 128
HEAD_PAD = 128
RW_HEADS = 8
RW_DIM = 64
RW_PAIRS = RW_HEADS // 2
GLA_DK = 64
GLA_TAU = 16.0
SB_DIM = 128
SB_BLOCK = 256
SB_SKIP = 120.0
N_EXPERTS = 64
TOP_K = 8
N_GROUPS = 8
TOPK_GROUPS = 4
EXPERT_W = 512
ROUTED_SCALE = 2.5
ROW_BLOCK = 256
DEEPNORM_ALPHA = (2 * DEPTH) ** 0.25
LN_EPS = 1e-5
RW_GN_EPS = 64e-5
NEG_BIG = -1e30

HG_OFF = 0
RW_OFF = 2048
SB_OFF = 3744
GLA_OFF = 5280
GATE_OFF = 6832
PROJ_TN = 512
RW_Z = 2048
GLA_Z = 2560
ZHG_OFF = 0
ZRW_OFF = ZHG_OFF + 4 * BRANCH_W
ZSB_OFF = ZRW_OFF + RW_Z
ZGLA_OFF = ZSB_OFF + 3 * BRANCH_W
ZGATE_OFF = ZGLA_OFF + GLA_Z
Z_TOTAL = ZGATE_OFF + N_BRANCHES * D_MODEL

_NT = (((1,), (1,)), ((), ()))
_TN = (((0,), (0,)), ((), ()))


def _cparams(sem, vmem_mb=48):
    return pltpu.CompilerParams(dimension_semantics=sem, vmem_limit_bytes=vmem_mb << 20)


def _sigmoid(x):
    return 1.0 / (1.0 + jnp.exp(-x))


def _log_sigmoid(x):
    return jnp.minimum(x, 0.0) - jnp.log1p(jnp.exp(-jnp.abs(x)))


def _silu(x):
    return x * _sigmoid(x)


def _layer_norm(y, w, b):
    mu = jnp.mean(y, axis=-1, keepdims=True)
    d = y - mu
    var = jnp.mean(d * d, axis=-1, keepdims=True)
    return d * lax.rsqrt(var + LN_EPS) * w + b


def _pack_pair(lo, hi):
    lo_b = lax.bitcast_convert_type(lo.astype(BF16).astype(F32), U32) >> 16
    hi_b = lax.bitcast_convert_type(hi.astype(BF16).astype(F32), U32) & jnp.uint32(0xFFFF0000)
    return lo_b | hi_b


def _unpack_pair(u):
    lo = lax.bitcast_convert_type(u << 16, F32)
    hi = lax.bitcast_convert_type(u & jnp.uint32(0xFFFF0000), F32)
    return lo, hi


def _dot2(x, ones_bf16):
    hi = x.astype(BF16)
    lo = (x - hi.astype(F32)).astype(BF16)
    return (jnp.dot(hi, ones_bf16, preferred_element_type=F32)
            + jnp.dot(lo, ones_bf16, preferred_element_type=F32))


def _mm_body(x_ref, w_ref, o_ref, *, act):
    acc = jnp.dot(x_ref[...], w_ref[...], preferred_element_type=F32)
    if act == "sigmoid":
        acc = _sigmoid(acc)
    o_ref[...] = acc.astype(o_ref.dtype)


def _matmul(x, w, col0, n, out_dtype, act=None, tm=1024, tn=PROJ_TN):
    m, k = x.shape
    tm = min(tm, m)
    off = col0 // tn
    return pl.pallas_call(
        functools.partial(_mm_body, act=act),
        grid=(m // tm, n // tn),
        in_specs=[pl.BlockSpec((tm, k), lambda i, j: (i, 0)),
                  pl.BlockSpec((k, tn), lambda i, j: (0, off + j))],
        out_specs=pl.BlockSpec((tm, tn), lambda i, j: (i, j)),
        out_shape=jax.ShapeDtypeStruct((m, n), out_dtype),
        compiler_params=_cparams(("parallel", "parallel")),
        name="proj",
    )(x, w)


def _gated_chunk(q, k, v, g, st, tril):
    hs = range(len(q))
    subs = range(N_SUB)
    dotf = functools.partial(jnp.dot, preferred_element_type=F32)
    blk = lambda x, i: x[i * SUB:(i + 1) * SUB]
    b = [jnp.dot(tril, g[h], precision=HI, preferred_element_type=F32) for h in hs]
    blast = [b[h][CHUNK - 1:CHUNK, :] for h in hs]
    vb = [v[h].astype(BF16) for h in hs]
    qe = [(q[h] * jnp.exp(b[h])).astype(BF16) for h in hs]
    o = [lax.dot_general(qe[h], st[h].astype(BF16), _NT, preferred_element_type=F32) for h in hs]
    o_parts = [[blk(o[h], i) for i in subs] for h in hs]
    ends = [[b[h][(j + 1) * SUB - 1:(j + 1) * SUB, :] for j in subs] for h in hs]
    khat = [[blk(k[h], j) * jnp.exp(ends[h][j] - blk(b[h], j)) for j in subs] for h in hs]

    for j in range(N_SUB - 1):
        lo = (j + 1) * SUB
        qs = [(q[h][lo:] * jnp.exp(b[h][lo:] - ends[h][j])).astype(BF16) for h in hs]
        a = [lax.dot_general(qs[h], khat[h][j].astype(BF16), _NT, preferred_element_type=F32) for h in hs]
        pv = [dotf(a[h].astype(BF16), blk(vb[h], j)) for h in hs]
        for h in hs:
            for i in range(j + 1, N_SUB):
                o_parts[h][i] = o_parts[h][i] + pv[h][(i - j - 1) * SUB:(i - j) * SUB]

    lane = lax.broadcasted_iota(I32, (SUB, LANES), 1)
    trow = lax.broadcasted_iota(I32, (SUB, 1), 0)
    for i in subs:
        d = [jnp.zeros((SUB, LANES), F32) for _ in hs]
        for s in range(SUB):
            for h in hs:
                bi = blk(b[h], i)
                e = jnp.exp(jnp.where(trow >= s, bi - bi[s:s + 1, :], NEG_BIG))
                col = jnp.sum(blk(q[h], i) * e * blk(k[h], i)[s:s + 1, :], axis=-1, keepdims=True)
                d[h] = jnp.where(lane == s, col, d[h])
        pv = [dotf(d[h][:, :SUB].astype(BF16), blk(vb[h], i)) for h in hs]
        for h in hs:
            o_parts[h][i] = o_parts[h][i] + pv[h]

    o = [jnp.concatenate(o_parts[h], axis=0) for h in hs]
    kd = [jnp.concatenate([khat[h][j] * jnp.exp(blast[h] - ends[h][j]) for j in subs], axis=0).astype(BF16)
          for h in hs]
    st_new = [st[h] * jnp.exp(blast[h]) + lax.dot_general(vb[h], kd[h], _TN, preferred_element_type=F32)
              for h in hs]
    return o, st_new


def _gated_body(z_ref, aux_ref, nw_ref, g2_ref, tril_ref, o_ref, st_ref, *, mode, rows):
    @pl.when(pl.program_id(0) == 0)
    def _():
        st_ref[...] = jnp.zeros_like(st_ref)

    tril = tril_ref[...]

    def chunk(c, carry):
        r0 = pl.multiple_of(c * CHUNK, CHUNK)
        rs = pl.ds(r0, CHUNK)
        hs = range(4)
        sls = [slice(h * HEAD_PAD, (h + 1) * HEAD_PAD) for h in hs]
        zq = [z_ref[rs, h * HEAD_PAD:(h + 1) * HEAD_PAD] for h in hs]
        zk = [z_ref[rs, BRANCH_W + h * HEAD_PAD:BRANCH_W + (h + 1) * HEAD_PAD] for h in hs]
        v = [z_ref[rs, 2 * BRANCH_W + h * HEAD_PAD:2 * BRANCH_W + (h + 1) * HEAD_PAD] for h in hs]
        if mode == "hgrn2":
            q = [_silu(zq[h]) for h in hs]
            k = [(1.0 - aux_ref[0:1, sls[h]]) * _sigmoid(-zk[h]) for h in hs]
            g = [jnp.log1p(-k[h]) for h in hs]
        else:
            q = [zq[h] * (GLA_DK ** -0.5) for h in hs]
            k = zk
            la = jnp.dot(z_ref[rs, 4 * BRANCH_W:4 * BRANCH_W + LANES], g2_ref[...], precision=HI,
                         preferred_element_type=F32) + aux_ref[...]
            g = [_log_sigmoid(la[:, sls[h]]) * (1.0 / GLA_TAU) for h in hs]
        o, st_new = _gated_chunk(q, k, v, g, [st_ref[h] for h in hs], tril)
        for h in hs:
            st_ref[h] = st_new[h]
            gate = z_ref[rs, 3 * BRANCH_W + h * HEAD_PAD:3 * BRANCH_W + (h + 1) * HEAD_PAD]
            on = o[h] * lax.rsqrt(jnp.mean(o[h] * o[h], axis=-1, keepdims=True) + LN_EPS)
            o_ref[rs, sls[h]] = (on * nw_ref[0:1, sls[h]] * _silu(gate)).astype(o_ref.dtype)
        return carry

    lax.fori_loop(0, rows // CHUNK, chunk, 0)


def _gated_mixer(z, aux, norm_w, g2, mode, rows=256):
    t, wz = z.shape
    rows = min(rows, t)
    tril = jnp.asarray(np.tril(np.ones((CHUNK, CHUNK), np.float32)))
    return pl.pallas_call(
        functools.partial(_gated_body, mode=mode, rows=rows),
        grid=(t // rows,),
        in_specs=[pl.BlockSpec((rows, wz), lambda i: (i, 0)),
                  pl.BlockSpec((1, BRANCH_W), lambda i: (0, 0)),
                  pl.BlockSpec((1, BRANCH_W), lambda i: (0, 0)),
                  pl.BlockSpec(g2.shape, lambda i: (0, 0)),
                  pl.BlockSpec((CHUNK, CHUNK), lambda i: (0, 0))],
        out_specs=pl.BlockSpec((rows, BRANCH_W), lambda i: (i, 0)),
        out_shape=jax.ShapeDtypeStruct((t, BRANCH_W), BF16),
        scratch_shapes=[pltpu.VMEM((4, HEAD_PAD, HEAD_PAD), F32)],
        compiler_params=_cparams(("arbitrary",)),
        name="gated_" + mode,
    )(z, aux, norm_w, g2, tril)


def _rwkv_body(z_ref, mu_ref, vec_ref, w2_ref, a2_ref, g2_ref, tril_ref, ones2_ref, o_ref,
               prev_ref, st_ref, *, rows):
    @pl.when(pl.program_id(0) == 0)
    def _():
        prev_ref[...] = jnp.zeros_like(prev_ref)
        st_ref[...] = jnp.zeros_like(st_ref)

    tril = tril_ref[...]
    ones2 = ones2_ref[...]
    row128 = lax.broadcasted_iota(I32, (2 * CHUNK, 2 * CHUNK), 0)
    col128 = lax.broadcasted_iota(I32, (2 * CHUNK, 2 * CHUNK), 1)
    rt = jnp.where(row128 >= CHUNK, row128 - CHUNK, row128)
    ct = jnp.where(col128 >= CHUNK, col128 - CHUNK, col128)
    strict = rt > ct
    incl = rt >= ct
    eye = row128 == col128
    head0 = lax.broadcasted_iota(I32, (1, LANES), 1) < RW_DIM
    first_row = lax.broadcasted_iota(I32, (CHUNK, 1), 0) == 0

    def stack(x):
        return jnp.concatenate([jnp.where(head0, x, 0.0), jnp.where(head0, 0.0, x)], axis=0)

    def chunk(c, carry):
        r0 = pl.multiple_of(c * CHUNK, CHUNK)
        rs = pl.ds(r0, CHUNK)
        z = z_ref[rs, :]
        zprev = jnp.where(first_row, prev_ref[...], pltpu.roll(z, 1, axis=0))
        prev_ref[...] = z[CHUNK - 1:CHUNK, :]
        zs = z + (zprev - z) * mu_ref[...]
        lw = zs[:, 3 * BRANCH_W:3 * BRANCH_W + LANES]
        la = zs[:, 3 * BRANCH_W + LANES:3 * BRANCH_W + 2 * LANES]
        lg = zs[:, 3 * BRANCH_W + 2 * LANES:3 * BRANCH_W + 3 * LANES]
        wl = -(vec_ref[0:1, :] + jnp.dot(jnp.tanh(lw), w2_ref[...], precision=HI, preferred_element_type=F32))
        w_raw = -(jnp.maximum(wl, 0.0) + jnp.log1p(jnp.exp(-jnp.abs(wl)))) - 0.5
        logw_all = -jnp.exp(w_raw)
        iclr_all = _sigmoid(vec_ref[1:2, :] + jnp.dot(la, a2_ref[...], precision=HI, preferred_element_type=F32))
        gate_all = jnp.dot(_sigmoid(lg), g2_ref[...], precision=HI, preferred_element_type=F32)
        cum_all = jnp.dot(tril, logw_all, precision=HI, preferred_element_type=F32)

        pairs = range(RW_PAIRS)
        sls = [slice(p * LANES, (p + 1) * LANES) for p in pairs]
        dotf = functools.partial(jnp.dot, preferred_element_type=F32)
        c2 = 2 * CHUNK
        r = [zs[:, p * LANES:(p + 1) * LANES] for p in pairs]
        k = [zs[:, BRANCH_W + p * LANES:BRANCH_W + (p + 1) * LANES] for p in pairs]
        v = [zs[:, 2 * BRANCH_W + p * LANES:2 * BRANCH_W + (p + 1) * LANES] for p in pairs]
        logw = [logw_all[:, s] for s in sls]
        iclr = [iclr_all[:, s] for s in sls]
        cum = [cum_all[:, s] for s in sls]
        kkr = [k[p] * vec_ref[2:3, sls[p]] for p in pairs]
        ssq = [_dot2(kkr[p] * kkr[p], ones2) for p in pairs]
        kk = [kkr[p] / jnp.maximum(jnp.sqrt(ssq[p]), 1e-12) for p in pairs]
        kmod = [k[p] * (1.0 + (iclr[p] - 1.0) * vec_ref[3:4, sls[p]]) for p in pairs]
        bb = [kk[p] * iclr[p] for p in pairs]
        clast = [cum[p][CHUNK - 1:CHUNK, :] for p in pairs]
        e_neg = [jnp.exp(-cum[p]) for p in pairs]
        e_end = [jnp.exp(clast[p] - cum[p]) for p in pairs]
        a_s = [stack(-kk[p] * jnp.exp(cum[p] - logw[p])) for p in pairs]
        r_s = [stack(r[p] * jnp.exp(cum[p])) for p in pairs]
        b_s = [stack(bb[p] * e_neg[p]) for p in pairs]
        k_s = [stack(kmod[p] * e_neg[p]) for p in pairs]
        bh_s = [stack(bb[p] * e_end[p]).astype(BF16) for p in pairs]
        kh_s = [stack(kmod[p] * e_end[p]).astype(BF16) for p in pairs]
        v_sb = [stack(v[p]).astype(BF16) for p in pairs]

        left = [jnp.concatenate([a_s[p], r_s[p]], axis=0).astype(BF16) for p in pairs]
        right = [jnp.concatenate([b_s[p], k_s[p]], axis=0).astype(BF16) for p in pairs]
        gram = [lax.dot_general(left[p], right[p], _NT, preferred_element_type=F32) for p in pairs]
        n_ab = [jnp.where(strict, gram[p][:c2, :c2], 0.0) for p in pairs]
        a_ak = [jnp.where(strict, gram[p][:c2, c2:], 0.0).astype(BF16) for p in pairs]
        g_bk = [jnp.concatenate([jnp.where(incl, gram[p][c2:, :c2], 0.0),
                                 jnp.where(incl, gram[p][c2:, c2:], 0.0)], axis=1).astype(BF16) for p in pairs]

        x = [jnp.where(eye, 1.0, 0.0) + n_ab[p] for p in pairs]
        pw = n_ab
        for _ in range(5):
            pwb = [pw[p].astype(BF16) for p in pairs]
            pw = [dotf(pwb[p], pwb[p]) for p in pairs]
            x = [x[p] + dotf(pw[p].astype(BF16), x[p].astype(BF16)) for p in pairs]

        av = [dotf(a_ak[p], v_sb[p]) for p in pairs]
        pqb = [dotf(x[p].astype(BF16), jnp.concatenate([a_s[p], av[p]], axis=1).astype(BF16)).astype(BF16)
               for p in pairs]
        zero_blk = jnp.zeros((c2, LANES), BF16)
        top = [dotf(g_bk[p], jnp.concatenate([pqb[p], jnp.concatenate([zero_blk, v_sb[p]], axis=1)], axis=0))
               for p in pairs]
        bot = [lax.dot_general(bh_s[p], pqb[p], _TN, preferred_element_type=F32) for p in pairs]
        kv = [lax.dot_general(kh_s[p], v_sb[p], _TN, preferred_element_type=F32) for p in pairs]
        r2 = [(r_s[p] + top[p][:, :LANES]).astype(BF16) for p in pairs]
        tm = [(bot[p][:, :LANES] + jnp.where(eye, jnp.exp(clast[p]), 0.0)).astype(BF16) for p in pairs]

        s_old = [st_ref[p].astype(BF16) for p in pairs]
        o_st = [dotf(r2[p], s_old[p]) + top[p][:, LANES:] for p in pairs]
        s_new = [dotf(tm[p], s_old[p]) + bot[p][:, LANES:] + kv[p] for p in pairs]
        for p in pairs:
            st_ref[p] = s_new[p]
        o = [o_st[p][:CHUNK] + o_st[p][CHUNK:] for p in pairs]

        inv_n = 1.0 / RW_DIM
        mean = [_dot2(o[p], ones2) * inv_n for p in pairs]
        dlt = [o[p] - mean[p] for p in pairs]
        var = [_dot2(dlt[p] * dlt[p], ones2) * inv_n for p in pairs]
        bonus = [_dot2(r[p] * kmod[p] * vec_ref[4:5, sls[p]], ones2) * v[p] for p in pairs]
        for p in pairs:
            on = dlt[p] * lax.rsqrt(var[p] + RW_GN_EPS) * vec_ref[5:6, sls[p]] + vec_ref[6:7, sls[p]]
            o_ref[rs, sls[p]] = ((on + bonus[p]) * gate_all[:, sls[p]]).astype(o_ref.dtype)
        return carry

    lax.fori_loop(0, rows // CHUNK, chunk, 0)


def _rwkv_mixer(z, mu, vecs, w2, a2, g2, rows=256):
    t = z.shape[0]
    rows = min(rows, t)
    tril = jnp.asarray(np.tril(np.ones((CHUNK, CHUNK), np.float32)))
    hid = np.arange(LANES) // RW_DIM
    ones2 = jnp.asarray((hid[:, None] == hid[None, :]).astype(np.float32), dtype=BF16)
    const = lambda i: (0, 0)
    return pl.pallas_call(
        functools.partial(_rwkv_body, rows=rows),
        grid=(t // rows,),
        in_specs=[pl.BlockSpec((rows, RW_Z), lambda i: (i, 0)),
                  pl.BlockSpec((1, RW_Z), const),
                  pl.BlockSpec((8, BRANCH_W), const),
                  pl.BlockSpec((LANES, BRANCH_W), const),
                  pl.BlockSpec((LANES, BRANCH_W), const),
                  pl.BlockSpec((LANES, BRANCH_W), const),
                  pl.BlockSpec((CHUNK, CHUNK), const),
                  pl.BlockSpec((LANES, LANES), const)],
        out_specs=pl.BlockSpec((rows, BRANCH_W), lambda i: (i, 0)),
        out_shape=jax.ShapeDtypeStruct((t, BRANCH_W), BF16),
        scratch_shapes=[pltpu.VMEM((1, RW_Z), F32), pltpu.VMEM((RW_PAIRS, LANES, LANES), F32)],
        compiler_params=_cparams(("arbitrary",)),
        name="rwkv7",
    )(z, mu, vecs, w2, a2, g2, tril, ones2)


def _sb_body(q_ref, k_ref, v_ref, upper_ref, o_ref, acc_ref, car_ref):
    i = pl.program_id(1)
    q = q_ref[...]
    upper = upper_ref[...]
    acc_ref[...] = jnp.zeros_like(acc_ref)
    car_ref[...] = jnp.zeros_like(car_ref)
    qpos = i * SB_BLOCK + lax.broadcasted_iota(I32, (SB_BLOCK, SB_BLOCK), 0)
    kidx = lax.broadcasted_iota(I32, (SB_BLOCK, SB_BLOCK), 1)

    def body(state):
        j, _ = state
        k0 = pl.multiple_of(j * SB_BLOCK, SB_BLOCK)
        kj = k_ref[pl.ds(k0, SB_BLOCK), :]
        vj = v_ref[pl.ds(k0, SB_BLOCK), :]
        z = lax.dot_general(q, kj, _NT, preferred_element_type=F32)
        strict = (kidx + j * SB_BLOCK) < qpos
        lk = jnp.where(strict, jnp.minimum(-z, 0.0) - jnp.log1p(jnp.exp(-jnp.abs(z))), 0.0)
        hi = lk.astype(BF16)
        lo = (lk - hi.astype(F32)).astype(BF16)
        later = (jnp.dot(hi, upper, preferred_element_type=F32)
                 + jnp.dot(lo, upper, preferred_element_type=F32))
        car = car_ref[...]
        w = jnp.where(strict, jnp.exp(z + lk + later + car), 0.0)
        acc_ref[...] += jnp.dot(w.astype(BF16), vj, preferred_element_type=F32)
        car = car + jnp.sum(lk, axis=-1, keepdims=True)
        car_ref[...] = car
        return j - 1, jnp.max(car)

    def cond(state):
        j, top = state
        return jnp.logical_and(j >= 0, top > -SB_SKIP)

    lax.while_loop(cond, body, (i, jnp.float32(0.0)))
    o_ref[...] = acc_ref[...].astype(o_ref.dtype)


def _sb_attention(z):
    t = z.shape[0]
    heads = BRANCH_W // SB_DIM
    upper = jnp.asarray(np.triu(np.ones((SB_BLOCK, SB_BLOCK), np.float32), 1).T, dtype=BF16)
    return pl.pallas_call(
        _sb_body,
        grid=(heads, t // SB_BLOCK),
        in_specs=[pl.BlockSpec((SB_BLOCK, SB_DIM), lambda h, i: (i, h)),
                  pl.BlockSpec((t, SB_DIM), lambda h, i: (0, heads + h)),
                  pl.BlockSpec((t, SB_DIM), lambda h, i: (0, 2 * heads + h)),
                  pl.BlockSpec((SB_BLOCK, SB_BLOCK), lambda h, i: (0, 0))],
        out_specs=pl.BlockSpec((SB_BLOCK, SB_DIM), lambda h, i: (i, h)),
        out_shape=jax.ShapeDtypeStruct((t, heads * SB_DIM), BF16),
        scratch_shapes=[pltpu.VMEM((SB_BLOCK, SB_DIM), F32), pltpu.VMEM((SB_BLOCK, 1), F32)],
        compiler_params=_cparams(("arbitrary", "arbitrary")),
        name="stick_breaking",
    )(z, z, z, upper)


def _merge_body(b0_ref, b1_ref, b2_ref, b3_ref, gate_ref, wbr_ref, wout_ref, x_ref, lnw_ref, lnb_ref,
                xo_ref, xb_ref, xp_ref):
    merged = None
    for g, b_ref in enumerate((b0_ref, b1_ref, b2_ref, b3_ref)):
        y = jnp.dot(b_ref[...], wbr_ref[g], preferred_element_type=F32)
        y = y * gate_ref[:, g * D_MODEL:(g + 1) * D_MODEL].astype(F32)
        merged = y if merged is None else merged + y
    mix = jnp.dot(merged.astype(BF16), wout_ref[...], preferred_element_type=F32)
    y = _layer_norm(DEEPNORM_ALPHA * x_ref[...] + mix, lnw_ref[...], lnb_ref[...])
    xo_ref[...] = y
    xb_ref[...] = y.astype(BF16)
    xp_ref[...] = _pack_pair(y[:, :D_MODEL // 2], y[:, D_MODEL // 2:])


def _merge(branches, gates, w_br, w_out, x, ln_w, ln_b, tm=256):
    t = x.shape[0]
    tm = min(tm, t)
    row = lambda i: (i, 0)
    const2 = lambda i: (0, 0)
    return pl.pallas_call(
        _merge_body,
        grid=(t // tm,),
        in_specs=[pl.BlockSpec((tm, BRANCH_W), row)] * 4 + [
            pl.BlockSpec((tm, N_BRANCHES * D_MODEL), row),
            pl.BlockSpec((N_BRANCHES, BRANCH_W, D_MODEL), lambda i: (0, 0, 0)),
            pl.BlockSpec((D_MODEL, D_MODEL), const2),
            pl.BlockSpec((tm, D_MODEL), row),
            pl.BlockSpec((1, D_MODEL), const2),
            pl.BlockSpec((1, D_MODEL), const2)],
        out_specs=[pl.BlockSpec((tm, D_MODEL), row), pl.BlockSpec((tm, D_MODEL), row),
                   pl.BlockSpec((tm, D_MODEL // 2), row)],
        out_shape=[jax.ShapeDtypeStruct((t, D_MODEL), F32), jax.ShapeDtypeStruct((t, D_MODEL), BF16),
                   jax.ShapeDtypeStruct((t, D_MODEL // 2), U32)],
        compiler_params=_cparams(("parallel",), 56),
        name="merge_ln",
    )(*branches, gates, w_br, w_out, x, ln_w, ln_b)


def _first_index(hit, idx, size, axis):
    return jnp.min(jnp.where(hit, idx, size), axis=axis, keepdims=True)


def _router_body(x_ref, wr_ref, bias_ref, before_ref, idx_ref, w_ref, rank_ref, cnt_ref, run_ref, *, tm):
    @pl.when(pl.program_id(0) == 0)
    def _():
        run_ref[...] = jnp.zeros_like(run_ref)

    per = N_EXPERTS // N_GROUPS
    logits = lax.dot_general(wr_ref[...], x_ref[...], _NT, precision=HI, preferred_element_type=F32)
    scores = _sigmoid(logits)
    biased = scores + bias_ref[:, 0:1]
    g3 = biased.reshape(N_GROUPS, per, tm)
    pos = lax.broadcasted_iota(I32, (N_GROUPS, per, tm), 1)
    m1 = jnp.max(g3, axis=1, keepdims=True)
    f1 = _first_index(g3 == m1, pos, per, 1)
    m2 = jnp.max(jnp.where(pos == f1, -jnp.inf, g3), axis=1, keepdims=True)
    gscore = (m1 + m2).reshape(N_GROUPS, tm)

    gpos = lax.broadcasted_iota(I32, (N_GROUPS, tm), 0)
    chosen = jnp.zeros((N_GROUPS, tm), F32)
    cur = gscore
    for _ in range(TOPK_GROUPS):
        m = jnp.max(cur, axis=0, keepdims=True)
        pick = gpos == _first_index(cur == m, gpos, N_GROUPS, 0)
        chosen = jnp.where(pick, 1.0, chosen)
        cur = jnp.where(pick, -jnp.inf, cur)
    ok = jnp.broadcast_to(chosen.reshape(N_GROUPS, 1, tm), (N_GROUPS, per, tm)).reshape(N_EXPERTS, tm)

    epos = lax.broadcasted_iota(I32, (N_EXPERTS, tm), 0)
    cur = jnp.where(ok > 0.5, biased, -jnp.inf)
    picks, idx_rows, w_rows = [], [], []
    member = jnp.zeros((N_EXPERTS, tm), F32)
    for _ in range(TOP_K):
        m = jnp.max(cur, axis=0, keepdims=True)
        f = _first_index(cur == m, epos, N_EXPERTS, 0)
        pick = epos == f
        picks.append(pick)
        idx_rows.append(f)
        w_rows.append(jnp.sum(jnp.where(pick, scores, 0.0), axis=0, keepdims=True))
        member = jnp.where(pick, 1.0, member)
        cur = jnp.where(pick, -jnp.inf, cur)
    w_sel = jnp.concatenate(w_rows, axis=0)
    w_ref[...] = ROUTED_SCALE * w_sel / jnp.sum(w_sel, axis=0, keepdims=True)
    idx_ref[...] = jnp.concatenate(idx_rows, axis=0)

    seen = jnp.dot(member.astype(BF16), before_ref[...], preferred_element_type=F32) + run_ref[:, 0:1]
    rank_rows = [jnp.sum(jnp.where(pk, seen, 0.0), axis=0, keepdims=True) for pk in picks]
    rank_ref[...] = jnp.concatenate(rank_rows, axis=0).astype(I32)
    run_ref[...] = run_ref[...] + jnp.sum(member, axis=1, keepdims=True)
    cnt_ref[...] = run_ref[...]


def _router(x, w_router_t, bias, tm=512):
    t = x.shape[0]
    tm = min(tm, t)
    before = jnp.asarray(np.triu(np.ones((tm, tm), np.float32), 1), dtype=BF16)
    slot = lambda i: (0, i)
    const = lambda i: (0, 0)
    return pl.pallas_call(
        functools.partial(_router_body, tm=tm),
        grid=(t // tm,),
        in_specs=[pl.BlockSpec((tm, D_MODEL), lambda i: (i, 0)),
                  pl.BlockSpec((N_EXPERTS, D_MODEL), const),
                  pl.BlockSpec((N_EXPERTS, LANES), const),
                  pl.BlockSpec((tm, tm), const)],
        out_specs=[pl.BlockSpec((TOP_K, tm), slot), pl.BlockSpec((TOP_K, tm), slot),
                   pl.BlockSpec((TOP_K, tm), slot), pl.BlockSpec((N_EXPERTS, LANES), const)],
        out_shape=[jax.ShapeDtypeStruct((TOP_K, t), I32), jax.ShapeDtypeStruct((TOP_K, t), F32),
                   jax.ShapeDtypeStruct((TOP_K, t), I32), jax.ShapeDtypeStruct((N_EXPERTS, LANES), F32)],
        scratch_shapes=[pltpu.VMEM((N_EXPERTS, LANES), F32)],
        compiler_params=_cparams(("arbitrary",)),
        name="router",
    )(x, w_router_t, bias, before)


def _dest_body(start_ref, idx_ref, rank_ref, dest_ref):
    idx = idx_ref[...]
    base = jnp.zeros(idx.shape, I32)
    for e in range(N_EXPERTS):
        base = jnp.where(idx == e, start_ref[e], base)
    dest_ref[...] = base + rank_ref[...]


def _dest_rows(pad_start, idx, rank, tm=2048):
    t = idx.shape[1]
    tm = min(tm, t)
    slot = lambda i, s: (0, i)
    return pl.pallas_call(
        _dest_body,
        grid_spec=pltpu.PrefetchScalarGridSpec(
            num_scalar_prefetch=1, grid=(t // tm,),
            in_specs=[pl.BlockSpec((TOP_K, tm), slot), pl.BlockSpec((TOP_K, tm), slot)],
            out_specs=pl.BlockSpec((TOP_K, tm), slot)),
        out_shape=jax.ShapeDtypeStruct((TOP_K, t), I32),
        compiler_params=_cparams(("parallel",)),
        name="dest_rows",
    )(pad_start, idx, rank)


def _dispatch_body(zb_ref, dest_ref, x_ref, xs_hbm, zero_ref, sem, zsem, *, tm, n_fill):
    @pl.when(pl.program_id(0) == 0)
    def _():
        zero_ref[...] = jnp.zeros_like(zero_ref)

        def fill(i, carry):
            @pl.when(zb_ref[i] >= 0)
            def _():
                r0 = pl.multiple_of(zb_ref[i] * ROW_BLOCK, ROW_BLOCK)
                pltpu.make_async_copy(zero_ref, xs_hbm.at[pl.ds(r0, ROW_BLOCK)], zsem).start()
            return carry

        def drain(i, carry):
            @pl.when(zb_ref[i] >= 0)
            def _():
                pltpu.make_async_copy(zero_ref, xs_hbm.at[pl.ds(0, ROW_BLOCK)], zsem).wait()
            return carry

        lax.fori_loop(0, n_fill, fill, 0)
        lax.fori_loop(0, n_fill, drain, 0)

    def row(t, carry):
        for k in range(TOP_K):
            pltpu.make_async_copy(x_ref.at[pl.ds(t, 1)], xs_hbm.at[pl.ds(dest_ref[k, t], 1)], sem).start()
        return carry

    lax.fori_loop(0, tm, row, 0)
    for _ in range(TOP_K):
        pltpu.make_async_copy(x_ref, xs_hbm.at[pl.ds(0, tm)], sem).wait()


def _dispatch(zero_blocks, dest, x_packed, n_rows, tm=256):
    t, half = x_packed.shape
    tm = min(tm, t)
    return pl.pallas_call(
        functools.partial(_dispatch_body, tm=tm, n_fill=zero_blocks.shape[0]),
        grid_spec=pltpu.PrefetchScalarGridSpec(
            num_scalar_prefetch=1, grid=(t // tm,),
            in_specs=[pl.BlockSpec((TOP_K, tm), lambda i, zb: (0, i), memory_space=pltpu.SMEM),
                      pl.BlockSpec((tm, half), lambda i, zb: (i, 0))],
            out_specs=pl.BlockSpec(memory_space=pl.ANY),
            scratch_shapes=[pltpu.VMEM((ROW_BLOCK, half), U32), pltpu.SemaphoreType.DMA(()),
                            pltpu.SemaphoreType.DMA(())]),
        out_shape=jax.ShapeDtypeStruct((n_rows, half), U32),
        compiler_params=_cparams(("arbitrary",)),
        name="dispatch",
    )(zero_blocks, dest, x_packed)


def _experts_body(be_ref, nu_ref, xs_ref, wg_ref, wu_ref, wd_ref, ys_ref, wgb_ref, wub_ref, wdb_ref):
    b = pl.program_id(0)
    used = b < nu_ref[0]
    new_expert = jnp.logical_or(b == 0, be_ref[b] != be_ref[jnp.maximum(b - 1, 0)])

    @pl.when(jnp.logical_and(used, new_expert))
    def _():
        wgb_ref[...] = wg_ref[0, 0].astype(BF16)
        wub_ref[...] = wu_ref[0, 0].astype(BF16)
        wdb_ref[...] = wd_ref[0, 0].astype(BF16)

    @pl.when(used)
    def _():
        half = D_MODEL // 2
        lo, hi = _unpack_pair(xs_ref[...])
        lo = lo.astype(BF16)
        hi = hi.astype(BF16)
        gate = (jnp.dot(lo, wgb_ref[:half, :], preferred_element_type=F32)
                + jnp.dot(hi, wgb_ref[half:, :], preferred_element_type=F32))
        up = (jnp.dot(lo, wub_ref[:half, :], preferred_element_type=F32)
              + jnp.dot(hi, wub_ref[half:, :], preferred_element_type=F32))
        h = (_silu(gate) * up).astype(BF16)
        y = jnp.dot(h, wdb_ref[...], preferred_element_type=F32)
        ys_ref[...] = _pack_pair(y[:, :half], y[:, half:])

    @pl.when(jnp.logical_not(used))
    def _():
        ys_ref[...] = jnp.zeros_like(ys_ref)


def _experts(block_expert, n_used, xs, wg, wu, wd, layer):
    n_rows, half = xs.shape
    n_blocks = n_rows // ROW_BLOCK
    blk = lambda b, be, nu: (jnp.minimum(b, nu[0] - 1), 0)
    out_blk = lambda b, be, nu: (b, 0)
    wsel = lambda b, be, nu: (layer, be[jnp.minimum(b, nu[0] - 1)], 0, 0)
    return pl.pallas_call(
        _experts_body,
        grid_spec=pltpu.PrefetchScalarGridSpec(
            num_scalar_prefetch=2, grid=(n_blocks,),
            in_specs=[pl.BlockSpec((ROW_BLOCK, half), blk),
                      pl.BlockSpec((1, 1, D_MODEL, EXPERT_W), wsel),
                      pl.BlockSpec((1, 1, D_MODEL, EXPERT_W), wsel),
                      pl.BlockSpec((1, 1, EXPERT_W, D_MODEL), wsel)],
            out_specs=pl.BlockSpec((ROW_BLOCK, half), out_blk),
            scratch_shapes=[pltpu.VMEM((D_MODEL, EXPERT_W), BF16), pltpu.VMEM((D_MODEL, EXPERT_W), BF16),
                            pltpu.VMEM((EXPERT_W, D_MODEL), BF16)]),
        out_shape=jax.ShapeDtypeStruct((n_rows, half), U32),
        compiler_params=_cparams(("arbitrary",)),
        name="experts",
    )(block_expert, n_used, xs, wg, wu, wd)


def _combine_body(dest_ref, w_ref, x_ref, xb_ref, sg_ref, su_ref, sd_ref, lnw_ref, lnb_ref, ys_hbm,
                  xo_ref, xb_out_ref, buf_ref, sem, *, tm):
    def row(t, carry):
        for k in range(TOP_K):
            pltpu.make_async_copy(ys_hbm.at[pl.ds(dest_ref[k, t], 1)], buf_ref.at[k, pl.ds(t, 1)], sem).start()
        return carry

    lax.fori_loop(0, tm, row, 0)

    xb = xb_ref[...]
    h = _silu(jnp.dot(xb, sg_ref[...], preferred_element_type=F32)) * jnp.dot(xb, su_ref[...],
                                                                           preferred_element_type=F32)
    shared = jnp.dot(h.astype(BF16), sd_ref[...], preferred_element_type=F32)

    for k in range(TOP_K):
        pltpu.make_async_copy(ys_hbm.at[pl.ds(0, tm)], buf_ref.at[k], sem).wait()

    half = D_MODEL // 2
    wt = w_ref[...].T
    lo_acc = jnp.zeros((tm, half), F32)
    hi_acc = jnp.zeros((tm, half), F32)
    for k in range(TOP_K):
        lo, hi = _unpack_pair(buf_ref[k])
        wk = wt[:, k:k + 1]
        lo_acc = lo_acc + wk * lo
        hi_acc = hi_acc + wk * hi
    routed = jnp.concatenate([lo_acc, hi_acc], axis=1)
    y = _layer_norm(DEEPNORM_ALPHA * x_ref[...] + routed + shared, lnw_ref[...], lnb_ref[...])
    xo_ref[...] = y
    xb_out_ref[...] = y.astype(BF16)


def _combine(dest, w_sel, x, xb, sg, su, sd, ln_w, ln_b, ys, tm=128):
    t = x.shape[0]
    tm = min(tm, t)
    half = D_MODEL // 2
    row = lambda i: (i, 0)
    slot = lambda i: (0, i)
    const = lambda i: (0, 0)
    return pl.pallas_call(
        functools.partial(_combine_body, tm=tm),
        grid=(t // tm,),
        in_specs=[pl.BlockSpec((TOP_K, tm), slot, memory_space=pltpu.SMEM),
                  pl.BlockSpec((TOP_K, tm), slot),
                  pl.BlockSpec((tm, D_MODEL), row),
                  pl.BlockSpec((tm, D_MODEL), row),
                  pl.BlockSpec((D_MODEL, EXPERT_W), const),
                  pl.BlockSpec((D_MODEL, EXPERT_W), const),
                  pl.BlockSpec((EXPERT_W, D_MODEL), const),
                  pl.BlockSpec((1, D_MODEL), const),
                  pl.BlockSpec((1, D_MODEL), const),
                  pl.BlockSpec(memory_space=pl.ANY)],
        out_specs=[pl.BlockSpec((tm, D_MODEL), row), pl.BlockSpec((tm, D_MODEL), row)],
        out_shape=[jax.ShapeDtypeStruct((t, D_MODEL), F32), jax.ShapeDtypeStruct((t, D_MODEL), BF16)],
        scratch_shapes=[pltpu.VMEM((TOP_K, tm, half), U32), pltpu.SemaphoreType.DMA(())],
        compiler_params=_cparams(("arbitrary",)),
        name="combine_ln",
    )(dest, w_sel, x, xb, sg, su, sd, ln_w, ln_b, ys)


def _pad_cols(a, width):
    return jnp.pad(a, ((0, 0), (0, width - a.shape[1])))


def _pad_rows(a, height):
    return jnp.pad(a, ((0, height - a.shape[0]), (0, 0)))


def _pad_heads(a, heads, dk):
    r = a.shape[0]
    return jnp.pad(a.reshape(r, heads, dk), ((0, 0), (0, 0), (0, HEAD_PAD - dk))).reshape(r, heads * HEAD_PAD)


def _token_mixing(xb, w_in, lower_bound, hg_norm_w, rw_mu, rw_w0, rw_w2, rw_a0, rw_a2, rw_g2, rw_kk, rw_ka,
                  rw_rk, rw_ln_w, rw_ln_b, gla_g2, gla_gb, gla_norm_w):
    row = lambda a: a.reshape(1, -1).astype(F32)

    w_rw = w_in[:, RW_OFF:SB_OFF]
    w_sb = w_in[:, SB_OFF:GLA_OFF]
    w_gla = w_in[:, GLA_OFF:GATE_OFF]
    zcol = lambda n: jnp.zeros((D_MODEL, n), w_in.dtype)
    w_all = jnp.concatenate([
        w_in[:, HG_OFF:RW_OFF],
        w_rw[:, :1536], _pad_cols(w_rw[:, 1536:1568], LANES), _pad_cols(w_rw[:, 1568:1600], LANES),
        _pad_cols(w_rw[:, 1600:1696], LANES), zcol(RW_Z - 3 * BRANCH_W - 3 * LANES),
        w_sb[:, :BRANCH_W] * (SB_DIM ** -0.5), w_sb[:, BRANCH_W:],
        _pad_heads(w_gla[:, :256], 4, GLA_DK), _pad_heads(w_gla[:, 256:512], 4, GLA_DK),
        w_gla[:, 512:1024], w_gla[:, 1040:1552], _pad_cols(w_gla[:, 1024:1040], LANES),
        zcol(GLA_Z - 4 * BRANCH_W - LANES),
        w_in[:, GATE_OFF:]], axis=1).astype(BF16)

    z_hg = _matmul(xb, w_all, ZHG_OFF, 4 * BRANCH_W, F32)
    o_hg = _gated_mixer(z_hg, row(lower_bound), row(hg_norm_w), jnp.zeros((8, BRANCH_W), F32), "hgrn2")

    z_rw = _matmul(xb, w_all, ZRW_OFF, RW_Z, F32)
    mu = jnp.concatenate([rw_mu[:1536], jnp.pad(rw_mu[1536:1568], (0, 96)), jnp.pad(rw_mu[1568:1600], (0, 96)),
                          jnp.pad(rw_mu[1600:1696], (0, 32 + RW_Z - 3 * BRANCH_W - 3 * LANES))]).reshape(1, RW_Z)
    vecs = jnp.stack([rw_w0, rw_a0, rw_kk, rw_ka, rw_rk, rw_ln_w, rw_ln_b, jnp.zeros_like(rw_w0)]).astype(F32)
    o_rw = _rwkv_mixer(z_rw, mu, vecs, _pad_rows(rw_w2, LANES), _pad_rows(rw_a2, LANES), _pad_rows(rw_g2, LANES))

    o_sb = _sb_attention(_matmul(xb, w_all, ZSB_OFF, 3 * BRANCH_W, BF16))

    z_gla = _matmul(xb, w_all, ZGLA_OFF, GLA_Z, F32)
    g2p = _pad_rows(_pad_heads(gla_g2, 4, GLA_DK), LANES)
    gbp = _pad_heads(gla_gb.reshape(1, -1), 4, GLA_DK)
    o_gla = _gated_mixer(z_gla, gbp, row(gla_norm_w), g2p, "gla")

    gates = _matmul(xb, w_all, ZGATE_OFF, N_BRANCHES * D_MODEL, BF16, act="sigmoid", tn=1024)
    return (o_hg, o_rw, o_sb, o_gla), gates


def _moe(x, xb, xp, w_router, router_bias, we_gate, we_up, we_down, layer, ws_gate, ws_up, ws_down, ln_w, ln_b):
    t = x.shape[0]
    bias = jnp.broadcast_to(router_bias.astype(F32).reshape(N_EXPERTS, 1), (N_EXPERTS, LANES))
    idx, w_sel, rank, counts = _router(x, w_router.T.astype(F32), bias)

    cnt = counts[:, 0].astype(I32)
    padded = (cnt + ROW_BLOCK - 1) // ROW_BLOCK * ROW_BLOCK
    pad_end = jnp.cumsum(padded)
    pad_start = (pad_end - padded).astype(I32)
    n_blocks = t * TOP_K // ROW_BLOCK + N_EXPERTS
    n_used = (pad_end[-1:] // ROW_BLOCK).astype(I32)
    first_row = jnp.arange(n_blocks, dtype=I32) * ROW_BLOCK
    block_expert = jnp.minimum(jnp.sum(pad_end[None, :] <= first_row[:, None], axis=1), N_EXPERTS - 1).astype(I32)

    last_block = jnp.where(padded > 0, pad_end // ROW_BLOCK - 1, -1)
    tail = n_used[0] + jnp.arange(N_EXPERTS, dtype=I32)
    zero_blocks = jnp.concatenate([last_block, jnp.where(tail < n_blocks, tail, -1)]).astype(I32)

    dest = _dest_rows(pad_start, idx, rank)
    xs = _dispatch(zero_blocks, dest, xp, n_blocks * ROW_BLOCK)
    ys = _experts(block_expert, n_used, xs, we_gate, we_up, we_down, layer)
    return _combine(dest, w_sel, x, xb, ws_gate.astype(BF16), ws_up.astype(BF16), ws_down.astype(BF16),
                    ln_w.reshape(1, -1), ln_b.reshape(1, -1), ys)


def kernel(x, w_in, hg_lb_logits, hg_norm_w, rw_mu, rw_w0, rw_w2, rw_a0, rw_a2, rw_g2, rw_kk, rw_ka, rw_rk,
           rw_ln_w, rw_ln_b, gla_g2, gla_gb, gla_norm_w, w_br, w_out, ln1_w, ln1_b, w_router, router_bias,
           we_gate, we_up, we_down, ws_gate, ws_up, ws_down, ln2_w, ln2_b):
    bsz, t, d = x.shape
    cum = jnp.cumsum(jax.nn.softmax(hg_lb_logits.astype(F32), axis=0), axis=0)
    lower_bounds = cum - cum[0:1]
    outs = []
    for bi in range(bsz):
        xf = x[bi].astype(F32)
        xb = xf.astype(BF16)
        for l in range(DEPTH):
            branches, gates = _token_mixing(
                xb, w_in[l], lower_bounds[l], hg_norm_w[l], rw_mu[l], rw_w0[l], rw_w2[l], rw_a0[l], rw_a2[l],
                rw_g2[l], rw_kk[l], rw_ka[l], rw_rk[l], rw_ln_w[l], rw_ln_b[l], gla_g2[l], gla_gb[l],
                gla_norm_w[l])
            xf, xb, xp = _merge(branches, gates, w_br[l].astype(BF16), w_out[l].astype(BF16), xf,
                                ln1_w[l].reshape(1, -1), ln1_b[l].reshape(1, -1))
            xf, xb = _moe(xf, xb, xp, w_router[l], router_bias[l], we_gate, we_up, we_down, l,
                          ws_gate[l], ws_up[l], ws_down[l], ln2_w[l], ln2_b[l])
        outs.append(xf)
    return jnp.stack(outs).astype(x.dtype)
```

```python
import functools

import jax
import jax.numpy as jnp
import numpy as np
from jax import lax
from jax.experimental import pallas as pl
from jax.experimental.pallas import tpu as pltpu

F32 = jnp.float32
BF16 = jnp.bfloat16
I32 = jnp.int32
U32 = jnp.uint32
HI = lax.Precision.HIGHEST

D_MODEL = 2048
DEPTH = 2
BRANCH_W = 512
N_BRANCHES = 4
CHUNK = 64
SUB = 16
N_SUB = CHUNK // SUB
LANES = 128
HEAD_PAD = 128
RW_HEADS = 8
RW_DIM = 64
RW_PAIRS = RW_HEADS // 2
RW_SPAN = 2
GLA_DK = 64
GLA_TAU = 16.0
SB_DIM = 128
SB_BLOCK = 256
SB_SKIP = 120.0
N_EXPERTS = 64
TOP_K = 8
N_GROUPS = 8
TOPK_GROUPS = 4
EXPERT_W = 512
ROUTED_SCALE = 2.5
ROW_BLOCK = 256
DEEPNORM_ALPHA = (2 * DEPTH) ** 0.25
LN_EPS = 1e-5
RW_GN_EPS = 64e-5
NEG_BIG = -1e30

HG_OFF = 0
RW_OFF = 2048
SB_OFF = 3744
GLA_OFF = 5280
GATE_OFF = 6832
PROJ_TN = 512
RW_Z = 2048
GLA_Z = 2560
ZHG_OFF = 0
ZRW_OFF = ZHG_OFF + 4 * BRANCH_W
ZSB_OFF = ZRW_OFF + RW_Z
ZGLA_OFF = ZSB_OFF + 3 * BRANCH_W
ZGATE_OFF = ZGLA_OFF + GLA_Z
Z_TOTAL = ZGATE_OFF + N_BRANCHES * D_MODEL

_NT = (((1,), (1,)), ((), ()))
_TN = (((0,), (0,)), ((), ()))


def _cparams(sem, vmem_mb=48):
    return pltpu.CompilerParams(dimension_semantics=sem, vmem_limit_bytes=vmem_mb << 20)


def _sigmoid(x):
    return 1.0 / (1.0 + jnp.exp(-x))


def _log_sigmoid(x):
    return jnp.minimum(x, 0.0) - jnp.log1p(jnp.exp(-jnp.abs(x)))


def _silu(x):
    return x * _sigmoid(x)


def _layer_norm(y, w, b):
    mu = jnp.mean(y, axis=-1, keepdims=True)
    d = y - mu
    var = jnp.mean(d * d, axis=-1, keepdims=True)
    return d * lax.rsqrt(var + LN_EPS) * w + b


def _pack_pair(lo, hi):
    lo_b = lax.bitcast_convert_type(lo.astype(BF16).astype(F32), U32) >> 16
    hi_b = lax.bitcast_convert_type(hi.astype(BF16).astype(F32), U32) & jnp.uint32(0xFFFF0000)
    return lo_b | hi_b


def _unpack_pair(u):
    lo = lax.bitcast_convert_type(u << 16, F32)
    hi = lax.bitcast_convert_type(u & jnp.uint32(0xFFFF0000), F32)
    return lo, hi


def _dot2(x, ones_bf16):
    hi = x.astype(BF16)
    lo = (x - hi.astype(F32)).astype(BF16)
    return (jnp.dot(hi, ones_bf16, preferred_element_type=F32)
            + jnp.dot(lo, ones_bf16, preferred_element_type=F32))


def _mm_body(x_ref, w_ref, o_ref, *, act):
    acc = jnp.dot(x_ref[...], w_ref[...], preferred_element_type=F32)
    if act == "sigmoid":
        acc = _sigmoid(acc)
    o_ref[...] = acc.astype(o_ref.dtype)


def _matmul(x, w, col0, n, out_dtype, act=None, tm=1024, tn=PROJ_TN):
    m, k = x.shape
    tm = min(tm, m)
    off = col0 // tn
    return pl.pallas_call(
        functools.partial(_mm_body, act=act),
        grid=(m // tm, n // tn),
        in_specs=[pl.BlockSpec((tm, k), lambda i, j: (i, 0)),
                  pl.BlockSpec((k, tn), lambda i, j: (0, off + j))],
        out_specs=pl.BlockSpec((tm, tn), lambda i, j: (i, j)),
        out_shape=jax.ShapeDtypeStruct((m, n), out_dtype),
        compiler_params=_cparams(("parallel", "parallel")),
        name="proj",
    )(x, w)


def _gated_chunk(q, k, v, g, st, tril):
    hs = range(len(q))
    subs = range(N_SUB)
    dotf = functools.partial(jnp.dot, preferred_element_type=F32)
    blk = lambda x, i: x[i * SUB:(i + 1) * SUB]
    b = [jnp.dot(tril, g[h], precision=HI, preferred_element_type=F32) for h in hs]
    blast = [b[h][CHUNK - 1:CHUNK, :] for h in hs]
    vb = [v[h].astype(BF16) for h in hs]
    qe = [(q[h] * jnp.exp(b[h])).astype(BF16) for h in hs]
    o = [lax.dot_general(qe[h], st[h].astype(BF16), _NT, preferred_element_type=F32) for h in hs]
    o_parts = [[blk(o[h], i) for i in subs] for h in hs]
    ends = [[b[h][(j + 1) * SUB - 1:(j + 1) * SUB, :] for j in subs] for h in hs]
    khat = [[blk(k[h], j) * jnp.exp(ends[h][j] - blk(b[h], j)) for j in subs] for h in hs]

    for j in range(N_SUB - 1):
        lo = (j + 1) * SUB
        qs = [(q[h][lo:] * jnp.exp(b[h][lo:] - ends[h][j])).astype(BF16) for h in hs]
        a = [lax.dot_general(qs[h], khat[h][j].astype(BF16), _NT, preferred_element_type=F32) for h in hs]
        pv = [dotf(a[h].astype(BF16), blk(vb[h], j)) for h in hs]
        for h in hs:
            for i in range(j + 1, N_SUB):
                o_parts[h][i] = o_parts[h][i] + pv[h][(i - j - 1) * SUB:(i - j) * SUB]

    lane = lax.broadcasted_iota(I32, (SUB, LANES), 1)
    trow = lax.broadcasted_iota(I32, (SUB, 1), 0)
    for i in subs:
        d = [jnp.zeros((SUB, LANES), F32) for _ in hs]
        for s in range(SUB):
            for h in hs:
                bi = blk(b[h], i)
                e = jnp.exp(jnp.where(trow >= s, bi - bi[s:s + 1, :], NEG_BIG))
                col = jnp.sum(blk(q[h], i) * e * blk(k[h], i)[s:s + 1, :], axis=-1, keepdims=True)
                d[h] = jnp.where(lane == s, col, d[h])
        pv = [dotf(d[h][:, :SUB].astype(BF16), blk(vb[h], i)) for h in hs]
        for h in hs:
            o_parts[h][i] = o_parts[h][i] + pv[h]

    o = [jnp.concatenate(o_parts[h], axis=0) for h in hs]
    kd = [jnp.concatenate([khat[h][j] * jnp.exp(blast[h] - ends[h][j]) for j in subs], axis=0).astype(BF16)
          for h in hs]
    st_new = [st[h] * jnp.exp(blast[h]) + lax.dot_general(vb[h], kd[h], _TN, preferred_element_type=F32)
              for h in hs]
    return o, st_new


def _gated_body(z_ref, aux_ref, nw_ref, g2_ref, tril_ref, o_ref, st_ref, *, mode, rows):
    @pl.when(pl.program_id(0) == 0)
    def _():
        st_ref[...] = jnp.zeros_like(st_ref)

    tril = tril_ref[...]

    def chunk(c, carry):
        r0 = pl.multiple_of(c * CHUNK, CHUNK)
        rs = pl.ds(r0, CHUNK)
        hs = range(4)
        sls = [slice(h * HEAD_PAD, (h + 1) * HEAD_PAD) for h in hs]
        zq = [z_ref[rs, h * HEAD_PAD:(h + 1) * HEAD_PAD] for h in hs]
        zk = [z_ref[rs, BRANCH_W + h * HEAD_PAD:BRANCH_W + (h + 1) * HEAD_PAD] for h in hs]
        v = [z_ref[rs, 2 * BRANCH_W + h * HEAD_PAD:2 * BRANCH_W + (h + 1) * HEAD_PAD] for h in hs]
        if mode == "hgrn2":
            q = [_silu(zq[h]) for h in hs]
            k = [(1.0 - aux_ref[0:1, sls[h]]) * _sigmoid(-zk[h]) for h in hs]
            g = [jnp.log1p(-k[h]) for h in hs]
        else:
            q = [zq[h] * (GLA_DK ** -0.5) for h in hs]
            k = zk
            la = jnp.dot(z_ref[rs, 4 * BRANCH_W:4 * BRANCH_W + LANES], g2_ref[...], precision=HI,
                         preferred_element_type=F32) + aux_ref[...]
            g = [_log_sigmoid(la[:, sls[h]]) * (1.0 / GLA_TAU) for h in hs]
        o, st_new = _gated_chunk(q, k, v, g, [st_ref[h] for h in hs], tril)
        for h in hs:
            st_ref[h] = st_new[h]
            gate = z_ref[rs, 3 * BRANCH_W + h * HEAD_PAD:3 * BRANCH_W + (h + 1) * HEAD_PAD]
            on = o[h] * lax.rsqrt(jnp.mean(o[h] * o[h], axis=-1, keepdims=True) + LN_EPS)
            o_ref[rs, sls[h]] = (on * nw_ref[0:1, sls[h]] * _silu(gate)).astype(o_ref.dtype)
        return carry

    lax.fori_loop(0, rows // CHUNK, chunk, 0)


def _gated_mixer(z, aux, norm_w, g2, mode, rows=256):
    t, wz = z.shape
    rows = min(rows, t)
    tril = jnp.asarray(np.tril(np.ones((CHUNK, CHUNK), np.float32)))
    return pl.pallas_call(
        functools.partial(_gated_body, mode=mode, rows=rows),
        grid=(t // rows,),
        in_specs=[pl.BlockSpec((rows, wz), lambda i: (i, 0)),
                  pl.BlockSpec((1, BRANCH_W), lambda i: (0, 0)),
                  pl.BlockSpec((1, BRANCH_W), lambda i: (0, 0)),
                  pl.BlockSpec(g2.shape, lambda i: (0, 0)),
                  pl.BlockSpec((CHUNK, CHUNK), lambda i: (0, 0))],
        out_specs=pl.BlockSpec((rows, BRANCH_W), lambda i: (i, 0)),
        out_shape=jax.ShapeDtypeStruct((t, BRANCH_W), BF16),
        scratch_shapes=[pltpu.VMEM((4, HEAD_PAD, HEAD_PAD), F32)],
        compiler_params=_cparams(("arbitrary",)),
        name="gated_" + mode,
    )(z, aux, norm_w, g2, tril)


def _rwkv_body(z_ref, mu_ref, vec_ref, w2_ref, a2_ref, g2_ref, tril_ref, ones2_ref, o_ref,
               prev_ref, st_ref, *, rows):
    @pl.when(pl.program_id(0) == 0)
    def _():
        prev_ref[...] = jnp.zeros_like(prev_ref)
        st_ref[...] = jnp.zeros_like(st_ref)

    tril = tril_ref[...]
    ones2 = ones2_ref[...]
    row128 = lax.broadcasted_iota(I32, (2 * CHUNK, 2 * CHUNK), 0)
    col128 = lax.broadcasted_iota(I32, (2 * CHUNK, 2 * CHUNK), 1)
    rt = jnp.where(row128 >= CHUNK, row128 - CHUNK, row128)
    ct = jnp.where(col128 >= CHUNK, col128 - CHUNK, col128)
    strict = rt > ct
    incl = rt >= ct
    eye = row128 == col128
    head0 = lax.broadcasted_iota(I32, (1, LANES), 1) < RW_DIM
    span = RW_SPAN * CHUNK
    first_row = lax.broadcasted_iota(I32, (span, 1), 0) == 0

    def stack(x):
        return jnp.concatenate([jnp.where(head0, x, 0.0), jnp.where(head0, 0.0, x)], axis=0)

    def step(c, carry):
        r0 = pl.multiple_of(c * span, span)
        rs = pl.ds(r0, span)
        z = z_ref[rs, :]
        zprev = jnp.where(first_row, prev_ref[...], pltpu.roll(z, 1, axis=0))
        prev_ref[...] = z[span - 1:span, :]
        zs = z + (zprev - z) * mu_ref[...]
        lw = zs[:, 3 * BRANCH_W:3 * BRANCH_W + LANES]
        la = zs[:, 3 * BRANCH_W + LANES:3 * BRANCH_W + 2 * LANES]
        lg = zs[:, 3 * BRANCH_W + 2 * LANES:3 * BRANCH_W + 3 * LANES]
        wl = -(vec_ref[0:1, :] + jnp.dot(jnp.tanh(lw), w2_ref[...], precision=HI, preferred_element_type=F32))
        w_raw = -(jnp.maximum(wl, 0.0) + jnp.log1p(jnp.exp(-jnp.abs(wl)))) - 0.5
        logw_all = -jnp.exp(w_raw)
        iclr_all = _sigmoid(vec_ref[1:2, :] + jnp.dot(la, a2_ref[...], precision=HI, preferred_element_type=F32))
        gate_all = jnp.dot(_sigmoid(lg), g2_ref[...], precision=HI, preferred_element_type=F32)
        cum_all = jnp.dot(tril, logw_all, precision=HI, preferred_element_type=F32)

        pairs = range(RW_PAIRS)
        sls = [slice(p * LANES, (p + 1) * LANES) for p in pairs]
        dotf = functools.partial(jnp.dot, preferred_element_type=F32)
        c2 = 2 * CHUNK
        r_f = [zs[:, p * LANES:(p + 1) * LANES] for p in pairs]
        k_f = [zs[:, BRANCH_W + p * LANES:BRANCH_W + (p + 1) * LANES] for p in pairs]
        v_f = [zs[:, 2 * BRANCH_W + p * LANES:2 * BRANCH_W + (p + 1) * LANES] for p in pairs]
        cum_f = [cum_all[:, s] for s in sls]
        iclr_f = [iclr_all[:, s] for s in sls]
        kkr = [k_f[p] * vec_ref[2:3, sls[p]] for p in pairs]
        ssq = [_dot2(kkr[p] * kkr[p], ones2) for p in pairs]
        kk_f = [kkr[p] / jnp.maximum(jnp.sqrt(ssq[p]), 1e-12) for p in pairs]
        kmod_f = [k_f[p] * (1.0 + (iclr_f[p] - 1.0) * vec_ref[3:4, sls[p]]) for p in pairs]
        bb_f = [kk_f[p] * iclr_f[p] for p in pairs]
        eneg_f = [jnp.exp(-cum_f[p]) for p in pairs]
        a_f = [-kk_f[p] * jnp.exp(cum_f[p] - logw_all[:, sls[p]]) for p in pairs]
        rd_f = [r_f[p] * jnp.exp(cum_f[p]) for p in pairs]
        bn_f = [bb_f[p] * eneg_f[p] for p in pairs]
        kn_f = [kmod_f[p] * eneg_f[p] for p in pairs]

        lanes = [(ci, p) for ci in range(RW_SPAN) for p in pairs]
        ls = range(len(lanes))
        cut = lambda xs: [xs[p][ci * CHUNK:(ci + 1) * CHUNK] for ci, p in lanes]
        cum, bb, kmod = cut(cum_f), cut(bb_f), cut(kmod_f)
        clast = [cum[i][CHUNK - 1:CHUNK, :] for i in ls]
        e_end = [jnp.exp(clast[i] - cum[i]) for i in ls]
        a_s = [stack(x) for x in cut(a_f)]
        r_s = [stack(x) for x in cut(rd_f)]
        b_s = [stack(x) for x in cut(bn_f)]
        k_s = [stack(x) for x in cut(kn_f)]
        bh_s = [stack(bb[i] * e_end[i]).astype(BF16) for i in ls]
        kh_s = [stack(kmod[i] * e_end[i]).astype(BF16) for i in ls]
        v_sb = [stack(x).astype(BF16) for x in cut(v_f)]

        left = [jnp.concatenate([a_s[i], r_s[i]], axis=0).astype(BF16) for i in ls]
        right = [jnp.concatenate([b_s[i], k_s[i]], axis=0).astype(BF16) for i in ls]
        gram = [lax.dot_general(left[i], right[i], _NT, preferred_element_type=F32) for i in ls]
        n_ab = [jnp.where(strict, gram[i][:c2, :c2], 0.0) for i in ls]
        a_ak = [jnp.where(strict, gram[i][:c2, c2:], 0.0).astype(BF16) for i in ls]
        g_bk = [jnp.concatenate([jnp.where(incl, gram[i][c2:, :c2], 0.0),
                                 jnp.where(incl, gram[i][c2:, c2:], 0.0)], axis=1).astype(BF16) for i in ls]

        x = [jnp.where(eye, 1.0, 0.0) + n_ab[i] for i in ls]
        pw = n_ab
        for _ in range(5):
            pwb = [pw[i].astype(BF16) for i in ls]
            pw = [dotf(pwb[i], pwb[i]) for i in ls]
            x = [x[i] + dotf(pw[i].astype(BF16), x[i].astype(BF16)) for i in ls]

        av = [dotf(a_ak[i], v_sb[i]) for i in ls]
        pqb = [dotf(x[i].astype(BF16), jnp.concatenate([a_s[i], av[i]], axis=1).astype(BF16)).astype(BF16)
               for i in ls]
        zero_blk = jnp.zeros((c2, LANES), BF16)
        top = [dotf(g_bk[i], jnp.concatenate([pqb[i], jnp.concatenate([zero_blk, v_sb[i]], axis=1)], axis=0))
               for i in ls]
        bot = [lax.dot_general(bh_s[i], pqb[i], _TN, preferred_element_type=F32) for i in ls]
        kv = [lax.dot_general(kh_s[i], v_sb[i], _TN, preferred_element_type=F32) for i in ls]
        r2 = [(r_s[i] + top[i][:, :LANES]).astype(BF16) for i in ls]
        tm = [(bot[i][:, :LANES] + jnp.where(eye, jnp.exp(clast[i]), 0.0)).astype(BF16) for i in ls]

        state = [st_ref[p] for p in pairs]
        o_ch = []
        for ci in range(RW_SPAN):
            idx = [ci * RW_PAIRS + p for p in pairs]
            s_old = [state[p].astype(BF16) for p in pairs]
            o_st = [dotf(r2[idx[p]], s_old[p]) + top[idx[p]][:, LANES:] for p in pairs]
            state = [dotf(tm[idx[p]], s_old[p]) + bot[idx[p]][:, LANES:] + kv[idx[p]] for p in pairs]
            o_ch.append([o_st[p][:CHUNK] + o_st[p][CHUNK:] for p in pairs])
        for p in pairs:
            st_ref[p] = state[p]
        o = [jnp.concatenate([o_ch[ci][p] for ci in range(RW_SPAN)], axis=0) for p in pairs]

        inv_n = 1.0 / RW_DIM
        mean = [_dot2(o[p], ones2) * inv_n for p in pairs]
        dlt = [o[p] - mean[p] for p in pairs]
        var = [_dot2(dlt[p] * dlt[p], ones2) * inv_n for p in pairs]
        bonus = [_dot2(r_f[p] * kmod_f[p] * vec_ref[4:5, sls[p]], ones2) * v_f[p] for p in pairs]
        for p in pairs:
            on = dlt[p] * lax.rsqrt(var[p] + RW_GN_EPS) * vec_ref[5:6, sls[p]] + vec_ref[6:7, sls[p]]
            o_ref[rs, sls[p]] = ((on + bonus[p]) * gate_all[:, sls[p]]).astype(o_ref.dtype)
        return carry

    lax.fori_loop(0, rows // span, step, 0)


def _rwkv_mixer(z, mu, vecs, w2, a2, g2, rows=256):
    t = z.shape[0]
    rows = min(rows, t)
    span = RW_SPAN * CHUNK
    pos = np.arange(span)
    tril = jnp.asarray(((pos[:, None] >= pos[None, :]) & (pos[:, None] // CHUNK == pos[None, :] // CHUNK))
                       .astype(np.float32))
    hid = np.arange(LANES) // RW_DIM
    ones2 = jnp.asarray((hid[:, None] == hid[None, :]).astype(np.float32), dtype=BF16)
    const = lambda i: (0, 0)
    return pl.pallas_call(
        functools.partial(_rwkv_body, rows=rows),
        grid=(t // rows,),
        in_specs=[pl.BlockSpec((rows, RW_Z), lambda i: (i, 0)),
                  pl.BlockSpec((1, RW_Z), const),
                  pl.BlockSpec((8, BRANCH_W), const),
                  pl.BlockSpec((LANES, BRANCH_W), const),
                  pl.BlockSpec((LANES, BRANCH_W), const),
                  pl.BlockSpec((LANES, BRANCH_W), const),
                  pl.BlockSpec((span, span), const),
                  pl.BlockSpec((LANES, LANES), const)],
        out_specs=pl.BlockSpec((rows, BRANCH_W), lambda i: (i, 0)),
        out_shape=jax.ShapeDtypeStruct((t, BRANCH_W), BF16),
        scratch_shapes=[pltpu.VMEM((1, RW_Z), F32), pltpu.VMEM((RW_PAIRS, LANES, LANES), F32)],
        compiler_params=_cparams(("arbitrary",)),
        name="rwkv7",
    )(z, mu, vecs, w2, a2, g2, tril, ones2)


def _sb_body(q_ref, k_ref, v_ref, upper_ref, o_ref, acc_ref, car_ref):
    i = pl.program_id(1)
    q = q_ref[...]
    upper = upper_ref[...]
    acc_ref[...] = jnp.zeros_like(acc_ref)
    car_ref[...] = jnp.zeros_like(car_ref)
    qpos = i * SB_BLOCK + lax.broadcasted_iota(I32, (SB_BLOCK, SB_BLOCK), 0)
    kidx = lax.broadcasted_iota(I32, (SB_BLOCK, SB_BLOCK), 1)

    def body(state):
        j, _ = state
        k0 = pl.multiple_of(j * SB_BLOCK, SB_BLOCK)
        kj = k_ref[pl.ds(k0, SB_BLOCK), :]
        vj = v_ref[pl.ds(k0, SB_BLOCK), :]
        z = lax.dot_general(q, kj, _NT, preferred_element_type=F32)
        strict = (kidx + j * SB_BLOCK) < qpos
        lk = jnp.where(strict, jnp.minimum(-z, 0.0) - jnp.log1p(jnp.exp(-jnp.abs(z))), 0.0)
        hi = lk.astype(BF16)
        lo = (lk - hi.astype(F32)).astype(BF16)
        later = (jnp.dot(hi, upper, preferred_element_type=F32)
                 + jnp.dot(lo, upper, preferred_element_type=F32))
        car = car_ref[...]
        w = jnp.where(strict, jnp.exp(z + lk + later + car), 0.0)
        acc_ref[...] += jnp.dot(w.astype(BF16), vj, preferred_element_type=F32)
        car = car + jnp.sum(lk, axis=-1, keepdims=True)
        car_ref[...] = car
        return j - 1, jnp.max(car)

    def cond(state):
        j, top = state
        return jnp.logical_and(j >= 0, top > -SB_SKIP)

    lax.while_loop(cond, body, (i, jnp.float32(0.0)))
    o_ref[...] = acc_ref[...].astype(o_ref.dtype)


def _sb_attention(z):
    t = z.shape[0]
    heads = BRANCH_W // SB_DIM
    upper = jnp.asarray(np.triu(np.ones((SB_BLOCK, SB_BLOCK), np.float32), 1).T, dtype=BF16)
    return pl.pallas_call(
        _sb_body,
        grid=(heads, t // SB_BLOCK),
        in_specs=[pl.BlockSpec((SB_BLOCK, SB_DIM), lambda h, i: (i, h)),
                  pl.BlockSpec((t, SB_DIM), lambda h, i: (0, heads + h)),
                  pl.BlockSpec((t, SB_DIM), lambda h, i: (0, 2 * heads + h)),
                  pl.BlockSpec((SB_BLOCK, SB_BLOCK), lambda h, i: (0, 0))],
        out_specs=pl.BlockSpec((SB_BLOCK, SB_DIM), lambda h, i: (i, h)),
        out_shape=jax.ShapeDtypeStruct((t, heads * SB_DIM), BF16),
        scratch_shapes=[pltpu.VMEM((SB_BLOCK, SB_DIM), F32), pltpu.VMEM((SB_BLOCK, 1), F32)],
        compiler_params=_cparams(("arbitrary", "arbitrary")),
        name="stick_breaking",
    )(z, z, z, upper)


def _merge_body(b0_ref, b1_ref, b2_ref, b3_ref, gate_ref, wbr_ref, wout_ref, x_ref, lnw_ref, lnb_ref,
                xo_ref, xb_ref, xp_ref):
    merged = None
    for g, b_ref in enumerate((b0_ref, b1_ref, b2_ref, b3_ref)):
        y = jnp.dot(b_ref[...], wbr_ref[g], preferred_element_type=F32)
        y = y * gate_ref[:, g * D_MODEL:(g + 1) * D_MODEL].astype(F32)
        merged = y if merged is None else merged + y
    mix = jnp.dot(merged.astype(BF16), wout_ref[...], preferred_element_type=F32)
    y = _layer_norm(DEEPNORM_ALPHA * x_ref[...] + mix, lnw_ref[...], lnb_ref[...])
    xo_ref[...] = y
    xb_ref[...] = y.astype(BF16)
    xp_ref[...] = _pack_pair(y[:, :D_MODEL // 2], y[:, D_MODEL // 2:])


def _merge(branches, gates, w_br, w_out, x, ln_w, ln_b, tm=256):
    t = x.shape[0]
    tm = min(tm, t)
    row = lambda i: (i, 0)
    const2 = lambda i: (0, 0)
    return pl.pallas_call(
        _merge_body,
        grid=(t // tm,),
        in_specs=[pl.BlockSpec((tm, BRANCH_W), row)] * 4 + [
            pl.BlockSpec((tm, N_BRANCHES * D_MODEL), row),
            pl.BlockSpec((N_BRANCHES, BRANCH_W, D_MODEL), lambda i: (0, 0, 0)),
            pl.BlockSpec((D_MODEL, D_MODEL), const2),
            pl.BlockSpec((tm, D_MODEL), row),
            pl.BlockSpec((1, D_MODEL), const2),
            pl.BlockSpec((1, D_MODEL), const2)],
        out_specs=[pl.BlockSpec((tm, D_MODEL), row), pl.BlockSpec((tm, D_MODEL), row),
                   pl.BlockSpec((tm, D_MODEL // 2), row)],
        out_shape=[jax.ShapeDtypeStruct((t, D_MODEL), F32), jax.ShapeDtypeStruct((t, D_MODEL), BF16),
                   jax.ShapeDtypeStruct((t, D_MODEL // 2), U32)],
        compiler_params=_cparams(("parallel",), 56),
        name="merge_ln",
    )(*branches, gates, w_br, w_out, x, ln_w, ln_b)


def _first_index(hit, idx, size, axis):
    return jnp.min(jnp.where(hit, idx, size), axis=axis, keepdims=True)


def _router_body(x_ref, wr_ref, bias_ref, before_ref, idx_ref, w_ref, rank_ref, cnt_ref, run_ref, *, tm):
    @pl.when(pl.program_id(0) == 0)
    def _():
        run_ref[...] = jnp.zeros_like(run_ref)

    per = N_EXPERTS // N_GROUPS
    logits = lax.dot_general(wr_ref[...], x_ref[...], _NT, precision=HI, preferred_element_type=F32)
    scores = _sigmoid(logits)
    biased = scores + bias_ref[:, 0:1]
    g3 = biased.reshape(N_GROUPS, per, tm)
    pos = lax.broadcasted_iota(I32, (N_GROUPS, per, tm), 1)
    m1 = jnp.max(g3, axis=1, keepdims=True)
    f1 = _first_index(g3 == m1, pos, per, 1)
    m2 = jnp.max(jnp.where(pos == f1, -jnp.inf, g3), axis=1, keepdims=True)
    gscore = (m1 + m2).reshape(N_GROUPS, tm)

    gpos = lax.broadcasted_iota(I32, (N_GROUPS, tm), 0)
    chosen = jnp.zeros((N_GROUPS, tm), F32)
    cur = gscore
    for _ in range(TOPK_GROUPS):
        m = jnp.max(cur, axis=0, keepdims=True)
        pick = gpos == _first_index(cur == m, gpos, N_GROUPS, 0)
        chosen = jnp.where(pick, 1.0, chosen)
        cur = jnp.where(pick, -jnp.inf, cur)
    ok = jnp.broadcast_to(chosen.reshape(N_GROUPS, 1, tm), (N_GROUPS, per, tm)).reshape(N_EXPERTS, tm)

    epos = lax.broadcasted_iota(I32, (N_EXPERTS, tm), 0)
    cur = jnp.where(ok > 0.5, biased, -jnp.inf)
    picks, idx_rows, w_rows = [], [], []
    member = jnp.zeros((N_EXPERTS, tm), F32)
    for _ in range(TOP_K):
        m = jnp.max(cur, axis=0, keepdims=True)
        f = _first_index(cur == m, epos, N_EXPERTS, 0)
        pick = epos == f
        picks.append(pick)
        idx_rows.append(f)
        w_rows.append(jnp.sum(jnp.where(pick, scores, 0.0), axis=0, keepdims=True))
        member = jnp.where(pick, 1.0, member)
        cur = jnp.where(pick, -jnp.inf, cur)
    w_sel = jnp.concatenate(w_rows, axis=0)
    w_ref[...] = ROUTED_SCALE * w_sel / jnp.sum(w_sel, axis=0, keepdims=True)
    idx_ref[...] = jnp.concatenate(idx_rows, axis=0)

    seen = jnp.dot(member.astype(BF16), before_ref[...], preferred_element_type=F32) + run_ref[:, 0:1]
    rank_rows = [jnp.sum(jnp.where(pk, seen, 0.0), axis=0, keepdims=True) for pk in picks]
    rank_ref[...] = jnp.concatenate(rank_rows, axis=0).astype(I32)
    run_ref[...] = run_ref[...] + jnp.sum(member, axis=1, keepdims=True)
    cnt_ref[...] = run_ref[...]


def _router(x, w_router_t, bias, tm=512):
    t = x.shape[0]
    tm = min(tm, t)
    before = jnp.asarray(np.triu(np.ones((tm, tm), np.float32), 1), dtype=BF16)
    slot = lambda i: (0, i)
    const = lambda i: (0, 0)
    return pl.pallas_call(
        functools.partial(_router_body, tm=tm),
        grid=(t // tm,),
        in_specs=[pl.BlockSpec((tm, D_MODEL), lambda i: (i, 0)),
                  pl.BlockSpec((N_EXPERTS, D_MODEL), const),
                  pl.BlockSpec((N_EXPERTS, LANES), const),
                  pl.BlockSpec((tm, tm), const)],
        out_specs=[pl.BlockSpec((TOP_K, tm), slot), pl.BlockSpec((TOP_K, tm), slot),
                   pl.BlockSpec((TOP_K, tm), slot), pl.BlockSpec((N_EXPERTS, LANES), const)],
        out_shape=[jax.ShapeDtypeStruct((TOP_K, t), I32), jax.ShapeDtypeStruct((TOP_K, t), F32),
                   jax.ShapeDtypeStruct((TOP_K, t), I32), jax.ShapeDtypeStruct((N_EXPERTS, LANES), F32)],
        scratch_shapes=[pltpu.VMEM((N_EXPERTS, LANES), F32)],
        compiler_params=_cparams(("arbitrary",)),
        name="router",
    )(x, w_router_t, bias, before)


def _dest_body(start_ref, idx_ref, rank_ref, dest_ref):
    idx = idx_ref[...]
    base = jnp.zeros(idx.shape, I32)
    for e in range(N_EXPERTS):
        base = jnp.where(idx == e, start_ref[e], base)
    dest_ref[...] = base + rank_ref[...]


def _dest_rows(pad_start, idx, rank, tm=2048):
    t = idx.shape[1]
    tm = min(tm, t)
    slot = lambda i, s: (0, i)
    return pl.pallas_call(
        _dest_body,
        grid_spec=pltpu.PrefetchScalarGridSpec(
            num_scalar_prefetch=1, grid=(t // tm,),
            in_specs=[pl.BlockSpec((TOP_K, tm), slot), pl.BlockSpec((TOP_K, tm), slot)],
            out_specs=pl.BlockSpec((TOP_K, tm), slot)),
        out_shape=jax.ShapeDtypeStruct((TOP_K, t), I32),
        compiler_params=_cparams(("parallel",)),
        name="dest_rows",
    )(pad_start, idx, rank)


def _dispatch_body(zb_ref, dest_ref, x_ref, xs_hbm, zero_ref, sem, zsem, *, tm, n_fill):
    @pl.when(pl.program_id(0) == 0)
    def _():
        zero_ref[...] = jnp.zeros_like(zero_ref)

        def fill(i, carry):
            @pl.when(zb_ref[i] >= 0)
            def _():
                r0 = pl.multiple_of(zb_ref[i] * ROW_BLOCK, ROW_BLOCK)
                pltpu.make_async_copy(zero_ref, xs_hbm.at[pl.ds(r0, ROW_BLOCK)], zsem).start()
            return carry

        def drain(i, carry):
            @pl.when(zb_ref[i] >= 0)
            def _():
                pltpu.make_async_copy(zero_ref, xs_hbm.at[pl.ds(0, ROW_BLOCK)], zsem).wait()
            return carry

        lax.fori_loop(0, n_fill, fill, 0)
        lax.fori_loop(0, n_fill, drain, 0)

    def row(t, carry):
        for k in range(TOP_K):
            pltpu.make_async_copy(x_ref.at[pl.ds(t, 1)], xs_hbm.at[pl.ds(dest_ref[k, t], 1)], sem).start(
                priority=k % 2)
        return carry

    lax.fori_loop(0, tm, row, 0)
    for _ in range(TOP_K):
        pltpu.make_async_copy(x_ref, xs_hbm.at[pl.ds(0, tm)], sem).wait()


def _dispatch(zero_blocks, dest, x_packed, n_rows, tm=256):
    t, half = x_packed.shape
    tm = min(tm, t)
    return pl.pallas_call(
        functools.partial(_dispatch_body, tm=tm, n_fill=zero_blocks.shape[0]),
        grid_spec=pltpu.PrefetchScalarGridSpec(
            num_scalar_prefetch=1, grid=(t // tm,),
            in_specs=[pl.BlockSpec((TOP_K, tm), lambda i, zb: (0, i), memory_space=pltpu.SMEM),
                      pl.BlockSpec((tm, half), lambda i, zb: (i, 0))],
            out_specs=pl.BlockSpec(memory_space=pl.ANY),
            scratch_shapes=[pltpu.VMEM((ROW_BLOCK, half), U32), pltpu.SemaphoreType.DMA(()),
                            pltpu.SemaphoreType.DMA(())]),
        out_shape=jax.ShapeDtypeStruct((n_rows, half), U32),
        compiler_params=_cparams(("arbitrary",)),
        name="dispatch",
    )(zero_blocks, dest, x_packed)


def _experts_body(be_ref, nu_ref, nxt_ref, slot_ref, xs_ref, wg_hbm, wu_hbm, wd_hbm, ys_ref,
                  wgf_ref, wuf_ref, wdf_ref, wgb_ref, wub_ref, wdb_ref, sem, *, layer):
    b = pl.program_id(0)
    used = b < nu_ref[0]
    new_expert = jnp.logical_or(b == 0, be_ref[b] != be_ref[jnp.maximum(b - 1, 0)])

    def weight_copies(e, s):
        return (pltpu.make_async_copy(wg_hbm.at[layer, e], wgf_ref.at[s], sem.at[s]),
                pltpu.make_async_copy(wu_hbm.at[layer, e], wuf_ref.at[s], sem.at[s]),
                pltpu.make_async_copy(wd_hbm.at[layer, e], wdf_ref.at[s], sem.at[s]))

    @pl.when(b == 0)
    def _():
        for c in weight_copies(be_ref[0], slot_ref[0]):
            c.start()

    @pl.when(jnp.logical_and(used, new_expert))
    def _():
        s = slot_ref[b]
        for c in weight_copies(be_ref[b], s):
            c.wait()

        @pl.when(nxt_ref[b] >= 0)
        def _():
            for c in weight_copies(nxt_ref[b], 1 - s):
                c.start()

        wgb_ref[...] = wgf_ref[s].astype(BF16)
        wub_ref[...] = wuf_ref[s].astype(BF16)
        wdb_ref[...] = wdf_ref[s].astype(BF16)

    @pl.when(used)
    def _():
        half = D_MODEL // 2
        lo, hi = _unpack_pair(xs_ref[...])
        lo = lo.astype(BF16)
        hi = hi.astype(BF16)
        gate = (jnp.dot(lo, wgb_ref[:half, :], preferred_element_type=F32)
                + jnp.dot(hi, wgb_ref[half:, :], preferred_element_type=F32))
        up = (jnp.dot(lo, wub_ref[:half, :], preferred_element_type=F32)
              + jnp.dot(hi, wub_ref[half:, :], preferred_element_type=F32))
        h = (_silu(gate) * up).astype(BF16)
        y = jnp.dot(h, wdb_ref[...], preferred_element_type=F32)
        ys_ref[...] = _pack_pair(y[:, :half], y[:, half:])

    @pl.when(jnp.logical_not(used))
    def _():
        ys_ref[...] = jnp.zeros_like(ys_ref)


def _experts(block_expert, n_used, next_expert, slot, xs, wg, wu, wd, layer):
    n_rows, half = xs.shape
    n_blocks = n_rows // ROW_BLOCK
    blk = lambda b, be, nu, nx, sl: (jnp.minimum(b, nu[0] - 1), 0)
    out_blk = lambda b, be, nu, nx, sl: (b, 0)
    hbm = pl.BlockSpec(memory_space=pl.ANY)
    return pl.pallas_call(
        functools.partial(_experts_body, layer=layer),
        grid_spec=pltpu.PrefetchScalarGridSpec(
            num_scalar_prefetch=4, grid=(n_blocks,),
            in_specs=[pl.BlockSpec((ROW_BLOCK, half), blk), hbm, hbm, hbm],
            out_specs=pl.BlockSpec((ROW_BLOCK, half), out_blk),
            scratch_shapes=[pltpu.VMEM((2, D_MODEL, EXPERT_W), F32), pltpu.VMEM((2, D_MODEL, EXPERT_W), F32),
                            pltpu.VMEM((2, EXPERT_W, D_MODEL), F32),
                            pltpu.VMEM((D_MODEL, EXPERT_W), BF16), pltpu.VMEM((D_MODEL, EXPERT_W), BF16),
                            pltpu.VMEM((EXPERT_W, D_MODEL), BF16), pltpu.SemaphoreType.DMA((2,))]),
        out_shape=jax.ShapeDtypeStruct((n_rows, half), U32),
        compiler_params=_cparams(("arbitrary",), 56),
        name="experts",
    )(block_expert, n_used, next_expert, slot, xs, wg, wu, wd)


def _combine_body(dest_ref, w_ref, x_ref, xb_ref, sg_ref, su_ref, sd_ref, lnw_ref, lnb_ref, ys_hbm,
                  xo_ref, xb_out_ref, buf_ref, sem, *, tm):
    def row(t, carry):
        for k in range(TOP_K):
            pltpu.make_async_copy(ys_hbm.at[pl.ds(dest_ref[k, t], 1)], buf_ref.at[k, pl.ds(t, 1)], sem).start(
                priority=k % 2)
        return carry

    lax.fori_loop(0, tm, row, 0)

    xb = xb_ref[...]
    h = _silu(jnp.dot(xb, sg_ref[...], preferred_element_type=F32)) * jnp.dot(xb, su_ref[...],
                                                                           preferred_element_type=F32)
    shared = jnp.dot(h.astype(BF16), sd_ref[...], preferred_element_type=F32)

    for k in range(TOP_K):
        pltpu.make_async_copy(ys_hbm.at[pl.ds(0, tm)], buf_ref.at[k], sem).wait()

    half = D_MODEL // 2
    wt = w_ref[...].T
    lo_acc = jnp.zeros((tm, half), F32)
    hi_acc = jnp.zeros((tm, half), F32)
    for k in range(TOP_K):
        lo, hi = _unpack_pair(buf_ref[k])
        wk = wt[:, k:k + 1]
        lo_acc = lo_acc + wk * lo
        hi_acc = hi_acc + wk * hi
    routed = jnp.concatenate([lo_acc, hi_acc], axis=1)
    y = _layer_norm(DEEPNORM_ALPHA * x_ref[...] + routed + shared, lnw_ref[...], lnb_ref[...])
    xo_ref[...] = y
    xb_out_ref[...] = y.astype(BF16)


def _combine(dest, w_sel, x, xb, sg, su, sd, ln_w, ln_b, ys, tm=256):
    t = x.shape[0]
    tm = min(tm, t)
    half = D_MODEL // 2
    row = lambda i: (i, 0)
    slot = lambda i: (0, i)
    const = lambda i: (0, 0)
    return pl.pallas_call(
        functools.partial(_combine_body, tm=tm),
        grid=(t // tm,),
        in_specs=[pl.BlockSpec((TOP_K, tm), slot, memory_space=pltpu.SMEM),
                  pl.BlockSpec((TOP_K, tm), slot),
                  pl.BlockSpec((tm, D_MODEL), row),
                  pl.BlockSpec((tm, D_MODEL), row),
                  pl.BlockSpec((D_MODEL, EXPERT_W), const),
                  pl.BlockSpec((D_MODEL, EXPERT_W), const),
                  pl.BlockSpec((EXPERT_W, D_MODEL), const),
                  pl.BlockSpec((1, D_MODEL), const),
                  pl.BlockSpec((1, D_MODEL), const),
                  pl.BlockSpec(memory_space=pl.ANY)],
        out_specs=[pl.BlockSpec((tm, D_MODEL), row), pl.BlockSpec((tm, D_MODEL), row)],
        out_shape=[jax.ShapeDtypeStruct((t, D_MODEL), F32), jax.ShapeDtypeStruct((t, D_MODEL), BF16)],
        scratch_shapes=[pltpu.VMEM((TOP_K, tm, half), U32), pltpu.SemaphoreType.DMA(())],
        compiler_params=_cparams(("arbitrary",)),
        name="combine_ln",
    )(dest, w_sel, x, xb, sg, su, sd, ln_w, ln_b, ys)


def _pad_cols(a, width):
    return jnp.pad(a, ((0, 0), (0, width - a.shape[1])))


def _pad_rows(a, height):
    return jnp.pad(a, ((0, height - a.shape[0]), (0, 0)))


def _pad_heads(a, heads, dk):
    r = a.shape[0]
    return jnp.pad(a.reshape(r, heads, dk), ((0, 0), (0, 0), (0, HEAD_PAD - dk))).reshape(r, heads * HEAD_PAD)


def _token_mixing(xb, w_in, lower_bound, hg_norm_w, rw_mu, rw_w0, rw_w2, rw_a0, rw_a2, rw_g2, rw_kk, rw_ka,
                  rw_rk, rw_ln_w, rw_ln_b, gla_g2, gla_gb, gla_norm_w):
    row = lambda a: a.reshape(1, -1).astype(F32)

    w_rw = w_in[:, RW_OFF:SB_OFF]
    w_sb = w_in[:, SB_OFF:GLA_OFF]
    w_gla = w_in[:, GLA_OFF:GATE_OFF]
    zcol = lambda n: jnp.zeros((D_MODEL, n), w_in.dtype)
    w_all = jnp.concatenate([
        w_in[:, HG_OFF:RW_OFF],
        w_rw[:, :1536], _pad_cols(w_rw[:, 1536:1568], LANES), _pad_cols(w_rw[:, 1568:1600], LANES),
        _pad_cols(w_rw[:, 1600:1696], LANES), zcol(RW_Z - 3 * BRANCH_W - 3 * LANES),
        w_sb[:, :BRANCH_W] * (SB_DIM ** -0.5), w_sb[:, BRANCH_W:],
        _pad_heads(w_gla[:, :256], 4, GLA_DK), _pad_heads(w_gla[:, 256:512], 4, GLA_DK),
        w_gla[:, 512:1024], w_gla[:, 1040:1552], _pad_cols(w_gla[:, 1024:1040], LANES),
        zcol(GLA_Z - 4 * BRANCH_W - LANES),
        w_in[:, GATE_OFF:]], axis=1).astype(BF16)

    z_hg = _matmul(xb, w_all, ZHG_OFF, 4 * BRANCH_W, F32, tn=2 * PROJ_TN)
    o_hg = _gated_mixer(z_hg, row(lower_bound), row(hg_norm_w), jnp.zeros((8, BRANCH_W), F32), "hgrn2")

    z_rw = _matmul(xb, w_all, ZRW_OFF, RW_Z, F32, tn=2 * PROJ_TN)
    mu = jnp.concatenate([rw_mu[:1536], jnp.pad(rw_mu[1536:1568], (0, 96)), jnp.pad(rw_mu[1568:1600], (0, 96)),
                          jnp.pad(rw_mu[1600:1696], (0, 32 + RW_Z - 3 * BRANCH_W - 3 * LANES))]).reshape(1, RW_Z)
    vecs = jnp.stack([rw_w0, rw_a0, rw_kk, rw_ka, rw_rk, rw_ln_w, rw_ln_b, jnp.zeros_like(rw_w0)]).astype(F32)
    o_rw = _rwkv_mixer(z_rw, mu, vecs, _pad_rows(rw_w2, LANES), _pad_rows(rw_a2, LANES), _pad_rows(rw_g2, LANES))

    o_sb = _sb_attention(_matmul(xb, w_all, ZSB_OFF, 3 * BRANCH_W, BF16))

    z_gla = _matmul(xb, w_all, ZGLA_OFF, GLA_Z, F32)
    g2p = _pad_rows(_pad_heads(gla_g2, 4, GLA_DK), LANES)
    gbp = _pad_heads(gla_gb.reshape(1, -1), 4, GLA_DK)
    o_gla = _gated_mixer(z_gla, gbp, row(gla_norm_w), g2p, "gla")

    gates = _matmul(xb, w_all, ZGATE_OFF, N_BRANCHES * D_MODEL, BF16, act="sigmoid", tn=1024)
    return (o_hg, o_rw, o_sb, o_gla), gates


def _moe(x, xb, xp, w_router, router_bias, we_gate, we_up, we_down, layer, ws_gate, ws_up, ws_down, ln_w, ln_b):
    t = x.shape[0]
    bias = jnp.broadcast_to(router_bias.astype(F32).reshape(N_EXPERTS, 1), (N_EXPERTS, LANES))
    idx, w_sel, rank, counts = _router(x, w_router.T.astype(F32), bias)

    cnt = counts[:, 0].astype(I32)
    padded = (cnt + ROW_BLOCK - 1) // ROW_BLOCK * ROW_BLOCK
    pad_end = jnp.cumsum(padded)
    pad_start = (pad_end - padded).astype(I32)
    n_blocks = t * TOP_K // ROW_BLOCK + N_EXPERTS
    n_used = (pad_end[-1:] // ROW_BLOCK).astype(I32)
    first_row = jnp.arange(n_blocks, dtype=I32) * ROW_BLOCK
    block_expert = jnp.minimum(jnp.sum(pad_end[None, :] <= first_row[:, None], axis=1), N_EXPERTS - 1).astype(I32)

    last_block = jnp.where(padded > 0, pad_end // ROW_BLOCK - 1, -1)
    tail = n_used[0] + jnp.arange(N_EXPERTS, dtype=I32)
    zero_blocks = jnp.concatenate([last_block, jnp.where(tail < n_blocks, tail, -1)]).astype(I32)

    has_rows = cnt > 0
    eid = jnp.arange(N_EXPERTS, dtype=I32)
    later = jnp.where(has_rows[None, :] & (eid[None, :] > eid[:, None]), eid[None, :], N_EXPERTS)
    next_used = jnp.min(later, axis=1)
    next_used = jnp.where(next_used < N_EXPERTS, next_used, -1).astype(I32)
    ordinal = jnp.cumsum(has_rows.astype(I32)) - 1
    next_expert = next_used[block_expert]
    slot = (ordinal[block_expert] % 2).astype(I32)

    dest = _dest_rows(pad_start, idx, rank)
    xs = _dispatch(zero_blocks, dest, xp, n_blocks * ROW_BLOCK)
    ys = _experts(block_expert, n_used, next_expert, slot, xs, we_gate, we_up, we_down, layer)
    return _combine(dest, w_sel, x, xb, ws_gate.astype(BF16), ws_up.astype(BF16), ws_down.astype(BF16),
                    ln_w.reshape(1, -1), ln_b.reshape(1, -1), ys)


def kernel(x, w_in, hg_lb_logits, hg_norm_w, rw_mu, rw_w0, rw_w2, rw_a0, rw_a2, rw_g2, rw_kk, rw_ka, rw_rk,
           rw_ln_w, rw_ln_b, gla_g2, gla_gb, gla_norm_w, w_br, w_out, ln1_w, ln1_b, w_router, router_bias,
           we_gate, we_up, we_down, ws_gate, ws_up, ws_down, ln2_w, ln2_b):
    bsz, t, d = x.shape
    cum = jnp.cumsum(jax.nn.softmax(hg_lb_logits.astype(F32), axis=0), axis=0)
    lower_bounds = cum - cum[0:1]
    outs = []
    for bi in range(bsz):
        xf = x[bi].astype(F32)
        xb = xf.astype(BF16)
        for l in range(DEPTH):
            branches, gates = _token_mixing(
                xb, w_in[l], lower_bounds[l], hg_norm_w[l], rw_mu[l], rw_w0[l], rw_w2[l], rw_a0[l], rw_a2[l],
                rw_g2[l], rw_kk[l], rw_ka[l], rw_rk[l], rw_ln_w[l], rw_ln_b[l], gla_g2[l], gla_gb[l],
                gla_norm_w[l])
            xf, xb, xp = _merge(branches, gates, w_br[l].astype(BF16), w_out[l].astype(BF16), xf,
                                ln1_w[l].reshape(1, -1), ln1_b[l].reshape(1, -1))
            xf, xb = _moe(xf, xb, xp, w_router[l], router_bias[l], we_gate, we_up, we_down, l,
                          ws_gate[l], ws_up[l], ws_down[l], ln2_w[l], ln2_b[l])
        outs.append(xf)
    return jnp.stack(outs).astype(x.dtype)
```

```python
import functools

import jax
import jax.numpy as jnp
import numpy as np
from jax import lax
from jax.experimental import pallas as pl
from jax.experimental.pallas import tpu as pltpu

F32 = jnp.float32
BF16 = jnp.bfloat16
I32 = jnp.int32
U32 = jnp.uint32
HI = lax.Precision.HIGHEST

D_MODEL = 2048
DEPTH = 2
BRANCH_W = 512
N_BRANCHES = 4
CHUNK = 64
SUB = 16
N_SUB = CHUNK // SUB
LANES = 128
HEAD_PAD = 128
RW_HEADS = 8
RW_DIM = 64
RW_PAIRS = RW_HEADS // 2
RW_SPAN = 2
GLA_DK = 64
GLA_TAU = 16.0
SB_DIM = 128
SB_BLOCK = 256
SB_PARTS = 2
SB_SKIP = 120.0
N_EXPERTS = 64
TOP_K = 8
N_GROUPS = 8
TOPK_GROUPS = 4
EXPERT_W = 512
ROUTED_SCALE = 2.5
ROW_BLOCK = 256
DEEPNORM_ALPHA = (2 * DEPTH) ** 0.25
LN_EPS = 1e-5
RW_GN_EPS = 64e-5
NEG_BIG = -1e30

HG_OFF = 0
RW_OFF = 2048
SB_OFF = 3744
GLA_OFF = 5280
GATE_OFF = 6832
PROJ_TN = 512
RW_Z = 2048
GLA_Z = 2560
ZHG_OFF = 0
ZRW_OFF = ZHG_OFF + 4 * BRANCH_W
ZSB_OFF = ZRW_OFF + RW_Z
ZGLA_OFF = ZSB_OFF + 3 * BRANCH_W
ZGATE_OFF = ZGLA_OFF + GLA_Z
Z_TOTAL = ZGATE_OFF + N_BRANCHES * D_MODEL

_NT = (((1,), (1,)), ((), ()))
_TN = (((0,), (0,)), ((), ()))


def _cparams(sem, vmem_mb=48):
    return pltpu.CompilerParams(dimension_semantics=sem, vmem_limit_bytes=vmem_mb << 20)


def _sigmoid(x):
    return 1.0 / (1.0 + jnp.exp(-x))


def _log_sigmoid(x):
    return jnp.minimum(x, 0.0) - jnp.log1p(jnp.exp(-jnp.abs(x)))


def _silu(x):
    return x * _sigmoid(x)


def _layer_norm(y, w, b):
    mu = jnp.mean(y, axis=-1, keepdims=True)
    d = y - mu
    var = jnp.mean(d * d, axis=-1, keepdims=True)
    return d * lax.rsqrt(var + LN_EPS) * w + b


def _pack_pair(lo, hi):
    lo_b = lax.bitcast_convert_type(lo.astype(BF16).astype(F32), U32) >> 16
    hi_b = lax.bitcast_convert_type(hi.astype(BF16).astype(F32), U32) & jnp.uint32(0xFFFF0000)
    return lo_b | hi_b


def _unpack_pair(u):
    lo = lax.bitcast_convert_type(u << 16, F32)
    hi = lax.bitcast_convert_type(u & jnp.uint32(0xFFFF0000), F32)
    return lo, hi


ROW_TILES = D_MODEL // 2 // LANES


def _store_rows(ref, packed):
    r = packed.shape[0]
    for j in range(ROW_TILES):
        ref[pl.ds(j, r, stride=ROW_TILES), :] = packed[:, j * LANES:(j + 1) * LANES]


def _load_rows(ref):
    r = ref.shape[0] // ROW_TILES
    return jnp.concatenate([ref[pl.ds(j, r, stride=ROW_TILES), :] for j in range(ROW_TILES)], axis=1)


def _row(ref, i):
    return ref.at[pl.ds(pl.multiple_of(i * ROW_TILES, ROW_TILES), ROW_TILES)]


def _dot2(x, ones_bf16):
    hi = x.astype(BF16)
    lo = (x - hi.astype(F32)).astype(BF16)
    return (jnp.dot(hi, ones_bf16, preferred_element_type=F32)
            + jnp.dot(lo, ones_bf16, preferred_element_type=F32))


def _mm_body(x_ref, w_ref, o_ref, *, act):
    acc = jnp.dot(x_ref[...], w_ref[...], preferred_element_type=F32)
    if act == "sigmoid":
        acc = _sigmoid(acc)
    o_ref[...] = acc.astype(o_ref.dtype)


def _matmul(x, w, col0, n, out_dtype, act=None, tm=1024, tn=PROJ_TN):
    m, k = x.shape
    tm = min(tm, m)
    off = col0 // tn
    return pl.pallas_call(
        functools.partial(_mm_body, act=act),
        grid=(m // tm, n // tn),
        in_specs=[pl.BlockSpec((tm, k), lambda i, j: (i, 0)),
                  pl.BlockSpec((k, tn), lambda i, j: (0, off + j))],
        out_specs=pl.BlockSpec((tm, tn), lambda i, j: (i, j)),
        out_shape=jax.ShapeDtypeStruct((m, n), out_dtype),
        compiler_params=_cparams(("parallel", "parallel")),
        name="proj",
    )(x, w)


def _gated_chunk(q, k, v, g, st, tril):
    hs = range(len(q))
    subs = range(N_SUB)
    dotf = functools.partial(jnp.dot, preferred_element_type=F32)
    blk = lambda x, i: x[i * SUB:(i + 1) * SUB]
    b = [jnp.dot(tril, g[h], precision=HI, preferred_element_type=F32) for h in hs]
    blast = [b[h][CHUNK - 1:CHUNK, :] for h in hs]
    vb = [v[h].astype(BF16) for h in hs]
    qe = [(q[h] * jnp.exp(b[h])).astype(BF16) for h in hs]
    o = [lax.dot_general(qe[h], st[h].astype(BF16), _NT, preferred_element_type=F32) for h in hs]
    o_parts = [[blk(o[h], i) for i in subs] for h in hs]
    ends = [[b[h][(j + 1) * SUB - 1:(j + 1) * SUB, :] for j in subs] for h in hs]
    khat = [[blk(k[h], j) * jnp.exp(ends[h][j] - blk(b[h], j)) for j in subs] for h in hs]

    for j in range(N_SUB - 1):
        lo = (j + 1) * SUB
        qs = [(q[h][lo:] * jnp.exp(b[h][lo:] - ends[h][j])).astype(BF16) for h in hs]
        a = [lax.dot_general(qs[h], khat[h][j].astype(BF16), _NT, preferred_element_type=F32) for h in hs]
        pv = [dotf(a[h].astype(BF16), blk(vb[h], j)) for h in hs]
        for h in hs:
            for i in range(j + 1, N_SUB):
                o_parts[h][i] = o_parts[h][i] + pv[h][(i - j - 1) * SUB:(i - j) * SUB]

    lane = lax.broadcasted_iota(I32, (SUB, LANES), 1)
    trow = lax.broadcasted_iota(I32, (SUB, 1), 0)
    for i in subs:
        d = [jnp.zeros((SUB, LANES), F32) for _ in hs]
        for s in range(SUB):
            for h in hs:
                bi = blk(b[h], i)
                e = jnp.exp(jnp.where(trow >= s, bi - bi[s:s + 1, :], NEG_BIG))
                col = jnp.sum(blk(q[h], i) * e * blk(k[h], i)[s:s + 1, :], axis=-1, keepdims=True)
                d[h] = jnp.where(lane == s, col, d[h])
        pv = [dotf(d[h][:, :SUB].astype(BF16), blk(vb[h], i)) for h in hs]
        for h in hs:
            o_parts[h][i] = o_parts[h][i] + pv[h]

    o = [jnp.concatenate(o_parts[h], axis=0) for h in hs]
    kd = [jnp.concatenate([khat[h][j] * jnp.exp(blast[h] - ends[h][j]) for j in subs], axis=0).astype(BF16)
          for h in hs]
    st_new = [st[h] * jnp.exp(blast[h]) + lax.dot_general(vb[h], kd[h], _TN, preferred_element_type=F32)
              for h in hs]
    return o, st_new


def _gated_body(z_ref, aux_ref, nw_ref, g2_ref, tril_ref, o_ref, st_ref, *, mode, rows):
    @pl.when(pl.program_id(0) == 0)
    def _():
        st_ref[...] = jnp.zeros_like(st_ref)

    tril = tril_ref[...]

    def chunk(c, carry):
        r0 = pl.multiple_of(c * CHUNK, CHUNK)
        rs = pl.ds(r0, CHUNK)
        hs = range(4)
        sls = [slice(h * HEAD_PAD, (h + 1) * HEAD_PAD) for h in hs]
        zq = [z_ref[rs, h * HEAD_PAD:(h + 1) * HEAD_PAD] for h in hs]
        zk = [z_ref[rs, BRANCH_W + h * HEAD_PAD:BRANCH_W + (h + 1) * HEAD_PAD] for h in hs]
        v = [z_ref[rs, 2 * BRANCH_W + h * HEAD_PAD:2 * BRANCH_W + (h + 1) * HEAD_PAD] for h in hs]
        if mode == "hgrn2":
            q = [_silu(zq[h]) for h in hs]
            k = [(1.0 - aux_ref[0:1, sls[h]]) * _sigmoid(-zk[h]) for h in hs]
            g = [jnp.log1p(-k[h]) for h in hs]
        else:
            q = [zq[h] * (GLA_DK ** -0.5) for h in hs]
            k = zk
            la = jnp.dot(z_ref[rs, 4 * BRANCH_W:4 * BRANCH_W + LANES], g2_ref[...], precision=HI,
                         preferred_element_type=F32) + aux_ref[...]
            g = [_log_sigmoid(la[:, sls[h]]) * (1.0 / GLA_TAU) for h in hs]
        o, st_new = _gated_chunk(q, k, v, g, [st_ref[h] for h in hs], tril)
        for h in hs:
            st_ref[h] = st_new[h]
            gate = z_ref[rs, 3 * BRANCH_W + h * HEAD_PAD:3 * BRANCH_W + (h + 1) * HEAD_PAD]
            on = o[h] * lax.rsqrt(jnp.mean(o[h] * o[h], axis=-1, keepdims=True) + LN_EPS)
            o_ref[rs, sls[h]] = (on * nw_ref[0:1, sls[h]] * _silu(gate)).astype(o_ref.dtype)
        return carry

    lax.fori_loop(0, rows // CHUNK, chunk, 0)


def _gated_mixer(z, aux, norm_w, g2, mode, rows=256):
    t, wz = z.shape
    rows = min(rows, t)
    tril = jnp.asarray(np.tril(np.ones((CHUNK, CHUNK), np.float32)))
    return pl.pallas_call(
        functools.partial(_gated_body, mode=mode, rows=rows),
        grid=(t // rows,),
        in_specs=[pl.BlockSpec((rows, wz), lambda i: (i, 0)),
                  pl.BlockSpec((1, BRANCH_W), lambda i: (0, 0)),
                  pl.BlockSpec((1, BRANCH_W), lambda i: (0, 0)),
                  pl.BlockSpec(g2.shape, lambda i: (0, 0)),
                  pl.BlockSpec((CHUNK, CHUNK), lambda i: (0, 0))],
        out_specs=pl.BlockSpec((rows, BRANCH_W), lambda i: (i, 0)),
        out_shape=jax.ShapeDtypeStruct((t, BRANCH_W), BF16),
        scratch_shapes=[pltpu.VMEM((4, HEAD_PAD, HEAD_PAD), F32)],
        compiler_params=_cparams(("arbitrary",)),
        name="gated_" + mode,
    )(z, aux, norm_w, g2, tril)


def _rwkv_body(z_ref, mu_ref, vec_ref, w2_ref, a2_ref, g2_ref, tril_ref, ones2_ref, o_ref,
               prev_ref, st_ref, *, rows):
    @pl.when(pl.program_id(0) == 0)
    def _():
        prev_ref[...] = jnp.zeros_like(prev_ref)
        st_ref[...] = jnp.zeros_like(st_ref)

    tril = tril_ref[...]
    ones2 = ones2_ref[...]
    row128 = lax.broadcasted_iota(I32, (2 * CHUNK, 2 * CHUNK), 0)
    col128 = lax.broadcasted_iota(I32, (2 * CHUNK, 2 * CHUNK), 1)
    rt = jnp.where(row128 >= CHUNK, row128 - CHUNK, row128)
    ct = jnp.where(col128 >= CHUNK, col128 - CHUNK, col128)
    strict = rt > ct
    incl = rt >= ct
    eye = row128 == col128
    head0 = lax.broadcasted_iota(I32, (1, LANES), 1) < RW_DIM
    span = RW_SPAN * CHUNK
    first_row = lax.broadcasted_iota(I32, (span, 1), 0) == 0

    def stack(x):
        return jnp.concatenate([jnp.where(head0, x, 0.0), jnp.where(head0, 0.0, x)], axis=0)

    def step(c, carry):
        r0 = pl.multiple_of(c * span, span)
        rs = pl.ds(r0, span)
        z = z_ref[rs, :]
        zprev = jnp.where(first_row, prev_ref[...], pltpu.roll(z, 1, axis=0))
        prev_ref[...] = z[span - 1:span, :]
        zs = z + (zprev - z) * mu_ref[...]
        lw = zs[:, 3 * BRANCH_W:3 * BRANCH_W + LANES]
        la = zs[:, 3 * BRANCH_W + LANES:3 * BRANCH_W + 2 * LANES]
        lg = zs[:, 3 * BRANCH_W + 2 * LANES:3 * BRANCH_W + 3 * LANES]
        wl = -(vec_ref[0:1, :] + jnp.dot(jnp.tanh(lw), w2_ref[...], precision=HI, preferred_element_type=F32))
        w_raw = -(jnp.maximum(wl, 0.0) + jnp.log1p(jnp.exp(-jnp.abs(wl)))) - 0.5
        logw_all = -jnp.exp(w_raw)
        iclr_all = _sigmoid(vec_ref[1:2, :] + jnp.dot(la, a2_ref[...], precision=HI, preferred_element_type=F32))
        gate_all = jnp.dot(_sigmoid(lg), g2_ref[...], precision=HI, preferred_element_type=F32)
        cum_all = jnp.dot(tril, logw_all, precision=HI, preferred_element_type=F32)

        pairs = range(RW_PAIRS)
        sls = [slice(p * LANES, (p + 1) * LANES) for p in pairs]
        dotf = functools.partial(jnp.dot, preferred_element_type=F32)
        c2 = 2 * CHUNK
        r_f = [zs[:, p * LANES:(p + 1) * LANES] for p in pairs]
        k_f = [zs[:, BRANCH_W + p * LANES:BRANCH_W + (p + 1) * LANES] for p in pairs]
        v_f = [zs[:, 2 * BRANCH_W + p * LANES:2 * BRANCH_W + (p + 1) * LANES] for p in pairs]
        cum_f = [cum_all[:, s] for s in sls]
        iclr_f = [iclr_all[:, s] for s in sls]
        kkr = [k_f[p] * vec_ref[2:3, sls[p]] for p in pairs]
        ssq = [_dot2(kkr[p] * kkr[p], ones2) for p in pairs]
        kk_f = [kkr[p] / jnp.maximum(jnp.sqrt(ssq[p]), 1e-12) for p in pairs]
        kmod_f = [k_f[p] * (1.0 + (iclr_f[p] - 1.0) * vec_ref[3:4, sls[p]]) for p in pairs]
        bb_f = [kk_f[p] * iclr_f[p] for p in pairs]
        eneg_f = [jnp.exp(-cum_f[p]) for p in pairs]
        a_f = [-kk_f[p] * jnp.exp(cum_f[p] - logw_all[:, sls[p]]) for p in pairs]
        rd_f = [r_f[p] * jnp.exp(cum_f[p]) for p in pairs]
        bn_f = [bb_f[p] * eneg_f[p] for p in pairs]
        kn_f = [kmod_f[p] * eneg_f[p] for p in pairs]

        lanes = [(ci, p) for ci in range(RW_SPAN) for p in pairs]
        ls = range(len(lanes))
        cut = lambda xs: [xs[p][ci * CHUNK:(ci + 1) * CHUNK] for ci, p in lanes]
        cum, bb, kmod = cut(cum_f), cut(bb_f), cut(kmod_f)
        clast = [cum[i][CHUNK - 1:CHUNK, :] for i in ls]
        e_end = [jnp.exp(clast[i] - cum[i]) for i in ls]
        a_s = [stack(x) for x in cut(a_f)]
        r_s = [stack(x) for x in cut(rd_f)]
        b_s = [stack(x) for x in cut(bn_f)]
        k_s = [stack(x) for x in cut(kn_f)]
        bh_s = [stack(bb[i] * e_end[i]).astype(BF16) for i in ls]
        kh_s = [stack(kmod[i] * e_end[i]).astype(BF16) for i in ls]
        v_sb = [stack(x).astype(BF16) for x in cut(v_f)]

        left = [jnp.concatenate([a_s[i], r_s[i]], axis=0).astype(BF16) for i in ls]
        right = [jnp.concatenate([b_s[i], k_s[i]], axis=0).astype(BF16) for i in ls]
        gram = [lax.dot_general(left[i], right[i], _NT, preferred_element_type=F32) for i in ls]
        n_ab = [jnp.where(strict, gram[i][:c2, :c2], 0.0) for i in ls]
        a_ak = [jnp.where(strict, gram[i][:c2, c2:], 0.0).astype(BF16) for i in ls]
        g_bk = [jnp.concatenate([jnp.where(incl, gram[i][c2:, :c2], 0.0),
                                 jnp.where(incl, gram[i][c2:, c2:], 0.0)], axis=1).astype(BF16) for i in ls]

        x = [jnp.where(eye, 1.0, 0.0) + n_ab[i] for i in ls]
        pw = n_ab
        for _ in range(5):
            pwb = [pw[i].astype(BF16) for i in ls]
            pw = [dotf(pwb[i], pwb[i]) for i in ls]
            x = [x[i] + dotf(pw[i].astype(BF16), x[i].astype(BF16)) for i in ls]

        av = [dotf(a_ak[i], v_sb[i]) for i in ls]
        pqb = [dotf(x[i].astype(BF16), jnp.concatenate([a_s[i], av[i]], axis=1).astype(BF16)).astype(BF16)
               for i in ls]
        zero_blk = jnp.zeros((c2, LANES), BF16)
        top = [dotf(g_bk[i], jnp.concatenate([pqb[i], jnp.concatenate([zero_blk, v_sb[i]], axis=1)], axis=0))
               for i in ls]
        bot = [lax.dot_general(bh_s[i], pqb[i], _TN, preferred_element_type=F32) for i in ls]
        kv = [lax.dot_general(kh_s[i], v_sb[i], _TN, preferred_element_type=F32) for i in ls]
        r2 = [(r_s[i] + top[i][:, :LANES]).astype(BF16) for i in ls]
        tm = [(bot[i][:, :LANES] + jnp.where(eye, jnp.exp(clast[i]), 0.0)).astype(BF16) for i in ls]

        state = [st_ref[p] for p in pairs]
        o_ch = []
        for ci in range(RW_SPAN):
            idx = [ci * RW_PAIRS + p for p in pairs]
            s_old = [state[p].astype(BF16) for p in pairs]
            o_st = [dotf(r2[idx[p]], s_old[p]) + top[idx[p]][:, LANES:] for p in pairs]
            state = [dotf(tm[idx[p]], s_old[p]) + bot[idx[p]][:, LANES:] + kv[idx[p]] for p in pairs]
            o_ch.append([o_st[p][:CHUNK] + o_st[p][CHUNK:] for p in pairs])
        for p in pairs:
            st_ref[p] = state[p]
        o = [jnp.concatenate([o_ch[ci][p] for ci in range(RW_SPAN)], axis=0) for p in pairs]

        inv_n = 1.0 / RW_DIM
        mean = [_dot2(o[p], ones2) * inv_n for p in pairs]
        dlt = [o[p] - mean[p] for p in pairs]
        var = [_dot2(dlt[p] * dlt[p], ones2) * inv_n for p in pairs]
        bonus = [_dot2(r_f[p] * kmod_f[p] * vec_ref[4:5, sls[p]], ones2) * v_f[p] for p in pairs]
        for p in pairs:
            on = dlt[p] * lax.rsqrt(var[p] + RW_GN_EPS) * vec_ref[5:6, sls[p]] + vec_ref[6:7, sls[p]]
            o_ref[rs, sls[p]] = ((on + bonus[p]) * gate_all[:, sls[p]]).astype(o_ref.dtype)
        return carry

    lax.fori_loop(0, rows // span, step, 0)


def _rwkv_mixer(z, mu, vecs, w2, a2, g2, rows=256):
    t = z.shape[0]
    rows = min(rows, t)
    span = RW_SPAN * CHUNK
    pos = np.arange(span)
    tril = jnp.asarray(((pos[:, None] >= pos[None, :]) & (pos[:, None] // CHUNK == pos[None, :] // CHUNK))
                       .astype(np.float32))
    hid = np.arange(LANES) // RW_DIM
    ones2 = jnp.asarray((hid[:, None] == hid[None, :]).astype(np.float32), dtype=BF16)
    const = lambda i: (0, 0)
    return pl.pallas_call(
        functools.partial(_rwkv_body, rows=rows),
        grid=(t // rows,),
        in_specs=[pl.BlockSpec((rows, RW_Z), lambda i: (i, 0)),
                  pl.BlockSpec((1, RW_Z), const),
                  pl.BlockSpec((8, BRANCH_W), const),
                  pl.BlockSpec((LANES, BRANCH_W), const),
                  pl.BlockSpec((LANES, BRANCH_W), const),
                  pl.BlockSpec((LANES, BRANCH_W), const),
                  pl.BlockSpec((span, span), const),
                  pl.BlockSpec((LANES, LANES), const)],
        out_specs=pl.BlockSpec((rows, BRANCH_W), lambda i: (i, 0)),
        out_shape=jax.ShapeDtypeStruct((t, BRANCH_W), BF16),
        scratch_shapes=[pltpu.VMEM((1, RW_Z), F32), pltpu.VMEM((RW_PAIRS, LANES, LANES), F32)],
        compiler_params=_cparams(("arbitrary",)),
        name="rwkv7",
    )(z, mu, vecs, w2, a2, g2, tril, ones2)


def _sb_body(q_ref, k_ref, v_ref, upper_ref, o_ref, acc_ref, car_ref):
    i = pl.program_id(1)
    q = q_ref[...]
    upper = upper_ref[...]
    acc_ref[...] = jnp.zeros_like(acc_ref)
    car_ref[...] = jnp.zeros_like(car_ref)
    parts = range(SB_PARTS)
    rows = SB_BLOCK // SB_PARTS
    cut = lambda a, p: a[p * rows:(p + 1) * rows]
    dotf = functools.partial(jnp.dot, preferred_element_type=F32)
    qpos = [i * SB_BLOCK + p * rows + lax.broadcasted_iota(I32, (rows, SB_BLOCK), 0) for p in parts]
    kidx = lax.broadcasted_iota(I32, (rows, SB_BLOCK), 1)

    def body(state):
        j, _ = state
        k0 = pl.multiple_of(j * SB_BLOCK, SB_BLOCK)
        kj = k_ref[pl.ds(k0, SB_BLOCK), :]
        vj = v_ref[pl.ds(k0, SB_BLOCK), :]
        z = [lax.dot_general(cut(q, p), kj, _NT, preferred_element_type=F32) for p in parts]
        strict = [(kidx + j * SB_BLOCK) < qpos[p] for p in parts]
        lk = [jnp.where(strict[p], jnp.minimum(-z[p], 0.0) - jnp.log1p(jnp.exp(-jnp.abs(z[p]))), 0.0)
              for p in parts]
        hi = [lk[p].astype(BF16) for p in parts]
        lo = [(lk[p] - hi[p].astype(F32)).astype(BF16) for p in parts]
        later = [dotf(hi[p], upper) + dotf(lo[p], upper) for p in parts]
        car = [car_ref[p] for p in parts]
        w = [jnp.where(strict[p], jnp.exp(z[p] + lk[p] + later[p] + car[p]), 0.0).astype(BF16) for p in parts]
        pv = [dotf(w[p], vj) for p in parts]
        top = None
        for p in parts:
            acc_ref[p] += pv[p]
            c = car[p] + jnp.sum(lk[p], axis=-1, keepdims=True)
            car_ref[p] = c
            top = jnp.max(c) if top is None else jnp.maximum(top, jnp.max(c))
        return j - 1, top

    def cond(state):
        j, top = state
        return jnp.logical_and(j >= 0, top > -SB_SKIP)

    lax.while_loop(cond, body, (i, jnp.float32(0.0)))
    for p in parts:
        o_ref[p * rows:(p + 1) * rows, :] = acc_ref[p].astype(o_ref.dtype)


def _sb_attention(z):
    t = z.shape[0]
    heads = BRANCH_W // SB_DIM
    upper = jnp.asarray(np.triu(np.ones((SB_BLOCK, SB_BLOCK), np.float32), 1).T, dtype=BF16)
    return pl.pallas_call(
        _sb_body,
        grid=(heads, t // SB_BLOCK),
        in_specs=[pl.BlockSpec((SB_BLOCK, SB_DIM), lambda h, i: (i, h)),
                  pl.BlockSpec((t, SB_DIM), lambda h, i: (0, heads + h)),
                  pl.BlockSpec((t, SB_DIM), lambda h, i: (0, 2 * heads + h)),
                  pl.BlockSpec((SB_BLOCK, SB_BLOCK), lambda h, i: (0, 0))],
        out_specs=pl.BlockSpec((SB_BLOCK, SB_DIM), lambda h, i: (i, h)),
        out_shape=jax.ShapeDtypeStruct((t, heads * SB_DIM), BF16),
        scratch_shapes=[pltpu.VMEM((SB_PARTS, SB_BLOCK // SB_PARTS, SB_DIM), F32),
                        pltpu.VMEM((SB_PARTS, SB_BLOCK // SB_PARTS, 1), F32)],
        compiler_params=_cparams(("arbitrary", "arbitrary")),
        name="stick_breaking",
    )(z, z, z, upper)


def _merge_body(b0_ref, b1_ref, b2_ref, b3_ref, gate_ref, wbr_ref, wout_ref, x_ref, lnw_ref, lnb_ref,
                xo_ref, xb_ref, xp_ref):
    merged = None
    for g, b_ref in enumerate((b0_ref, b1_ref, b2_ref, b3_ref)):
        y = jnp.dot(b_ref[...], wbr_ref[g], preferred_element_type=F32)
        y = y * gate_ref[:, g * D_MODEL:(g + 1) * D_MODEL].astype(F32)
        merged = y if merged is None else merged + y
    mix = jnp.dot(merged.astype(BF16), wout_ref[...], preferred_element_type=F32)
    y = _layer_norm(DEEPNORM_ALPHA * x_ref[...] + mix, lnw_ref[...], lnb_ref[...])
    xo_ref[...] = y
    xb_ref[...] = y.astype(BF16)
    _store_rows(xp_ref, _pack_pair(y[:, :D_MODEL // 2], y[:, D_MODEL // 2:]))


def _merge(branches, gates, w_br, w_out, x, ln_w, ln_b, tm=256):
    t = x.shape[0]
    tm = min(tm, t)
    row = lambda i: (i, 0)
    const2 = lambda i: (0, 0)
    return pl.pallas_call(
        _merge_body,
        grid=(t // tm,),
        in_specs=[pl.BlockSpec((tm, BRANCH_W), row)] * 4 + [
            pl.BlockSpec((tm, N_BRANCHES * D_MODEL), row),
            pl.BlockSpec((N_BRANCHES, BRANCH_W, D_MODEL), lambda i: (0, 0, 0)),
            pl.BlockSpec((D_MODEL, D_MODEL), const2),
            pl.BlockSpec((tm, D_MODEL), row),
            pl.BlockSpec((1, D_MODEL), const2),
            pl.BlockSpec((1, D_MODEL), const2)],
        out_specs=[pl.BlockSpec((tm, D_MODEL), row), pl.BlockSpec((tm, D_MODEL), row),
                   pl.BlockSpec((tm * ROW_TILES, LANES), row)],
        out_shape=[jax.ShapeDtypeStruct((t, D_MODEL), F32), jax.ShapeDtypeStruct((t, D_MODEL), BF16),
                   jax.ShapeDtypeStruct((t * ROW_TILES, LANES), U32)],
        compiler_params=_cparams(("parallel",), 56),
        name="merge_ln",
    )(*branches, gates, w_br, w_out, x, ln_w, ln_b)


def _first_index(hit, idx, size, axis):
    return jnp.min(jnp.where(hit, idx, size), axis=axis, keepdims=True)


def _router_body(x_ref, wr_ref, bias_ref, before_ref, idx_ref, w_ref, rank_ref, cnt_ref, run_ref, *, tm):
    @pl.when(pl.program_id(0) == 0)
    def _():
        run_ref[...] = jnp.zeros_like(run_ref)

    per = N_EXPERTS // N_GROUPS
    logits = lax.dot_general(wr_ref[...], x_ref[...], _NT, precision=HI, preferred_element_type=F32)
    scores = _sigmoid(logits)
    biased = scores + bias_ref[:, 0:1]
    g3 = biased.reshape(N_GROUPS, per, tm)
    pos = lax.broadcasted_iota(I32, (N_GROUPS, per, tm), 1)
    m1 = jnp.max(g3, axis=1, keepdims=True)
    f1 = _first_index(g3 == m1, pos, per, 1)
    m2 = jnp.max(jnp.where(pos == f1, -jnp.inf, g3), axis=1, keepdims=True)
    gscore = (m1 + m2).reshape(N_GROUPS, tm)

    gpos = lax.broadcasted_iota(I32, (N_GROUPS, tm), 0)
    chosen = jnp.zeros((N_GROUPS, tm), F32)
    cur = gscore
    for _ in range(TOPK_GROUPS):
        m = jnp.max(cur, axis=0, keepdims=True)
        pick = gpos == _first_index(cur == m, gpos, N_GROUPS, 0)
        chosen = jnp.where(pick, 1.0, chosen)
        cur = jnp.where(pick, -jnp.inf, cur)
    ok = jnp.broadcast_to(chosen.reshape(N_GROUPS, 1, tm), (N_GROUPS, per, tm)).reshape(N_EXPERTS, tm)

    epos = lax.broadcasted_iota(I32, (N_EXPERTS, tm), 0)
    cur = jnp.where(ok > 0.5, biased, -jnp.inf)
    picks, idx_rows, w_rows = [], [], []
    member = jnp.zeros((N_EXPERTS, tm), F32)
    for _ in range(TOP_K):
        m = jnp.max(cur, axis=0, keepdims=True)
        f = _first_index(cur == m, epos, N_EXPERTS, 0)
        pick = epos == f
        picks.append(pick)
        idx_rows.append(f)
        w_rows.append(jnp.sum(jnp.where(pick, scores, 0.0), axis=0, keepdims=True))
        member = jnp.where(pick, 1.0, member)
        cur = jnp.where(pick, -jnp.inf, cur)
    w_sel = jnp.concatenate(w_rows, axis=0)
    w_ref[...] = ROUTED_SCALE * w_sel / jnp.sum(w_sel, axis=0, keepdims=True)
    idx_ref[...] = jnp.concatenate(idx_rows, axis=0)

    seen = jnp.dot(member.astype(BF16), before_ref[...], preferred_element_type=F32) + run_ref[:, 0:1]
    rank_rows = [jnp.sum(jnp.where(pk, seen, 0.0), axis=0, keepdims=True) for pk in picks]
    rank_ref[...] = jnp.concatenate(rank_rows, axis=0).astype(I32)
    run_ref[...] = run_ref[...] + jnp.sum(member, axis=1, keepdims=True)
    cnt_ref[...] = run_ref[...]


def _router(x, w_router_t, bias, tm=512):
    t = x.shape[0]
    tm = min(tm, t)
    before = jnp.asarray(np.triu(np.ones((tm, tm), np.float32), 1), dtype=BF16)
    slot = lambda i: (0, i)
    const = lambda i: (0, 0)
    return pl.pallas_call(
        functools.partial(_router_body, tm=tm),
        grid=(t // tm,),
        in_specs=[pl.BlockSpec((tm, D_MODEL), lambda i: (i, 0)),
                  pl.BlockSpec((N_EXPERTS, D_MODEL), const),
                  pl.BlockSpec((N_EXPERTS, LANES), const),
                  pl.BlockSpec((tm, tm), const)],
        out_specs=[pl.BlockSpec((TOP_K, tm), slot), pl.BlockSpec((TOP_K, tm), slot),
                   pl.BlockSpec((TOP_K, tm), slot), pl.BlockSpec((N_EXPERTS, LANES), const)],
        out_shape=[jax.ShapeDtypeStruct((TOP_K, t), I32), jax.ShapeDtypeStruct((TOP_K, t), F32),
                   jax.ShapeDtypeStruct((TOP_K, t), I32), jax.ShapeDtypeStruct((N_EXPERTS, LANES), F32)],
        scratch_shapes=[pltpu.VMEM((N_EXPERTS, LANES), F32)],
        compiler_params=_cparams(("arbitrary",)),
        name="router",
    )(x, w_router_t, bias, before)


def _dest_body(start_ref, idx_ref, rank_ref, dest_ref):
    idx = idx_ref[...]
    base = jnp.zeros(idx.shape, I32)
    for e in range(N_EXPERTS):
        base = jnp.where(idx == e, start_ref[e], base)
    dest_ref[...] = base + rank_ref[...]


def _dest_rows(pad_start, idx, rank, tm=2048):
    t = idx.shape[1]
    tm = min(tm, t)
    slot = lambda i, s: (0, i)
    return pl.pallas_call(
        _dest_body,
        grid_spec=pltpu.PrefetchScalarGridSpec(
            num_scalar_prefetch=1, grid=(t // tm,),
            in_specs=[pl.BlockSpec((TOP_K, tm), slot), pl.BlockSpec((TOP_K, tm), slot)],
            out_specs=pl.BlockSpec((TOP_K, tm), slot)),
        out_shape=jax.ShapeDtypeStruct((TOP_K, t), I32),
        compiler_params=_cparams(("parallel",)),
        name="dest_rows",
    )(pad_start, idx, rank)


def _dispatch_body(zb_ref, dest_ref, x_ref, xs_hbm, zero_ref, sem, zsem, *, tm, n_fill):
    @pl.when(pl.program_id(0) == 0)
    def _():
        zero_ref[...] = jnp.zeros_like(zero_ref)

        def fill(i, carry):
            @pl.when(zb_ref[i] >= 0)
            def _():
                r0 = pl.multiple_of(zb_ref[i] * (ROW_BLOCK * ROW_TILES), ROW_BLOCK * ROW_TILES)
                pltpu.make_async_copy(zero_ref, xs_hbm.at[pl.ds(r0, ROW_BLOCK * ROW_TILES)], zsem).start()
            return carry

        def drain(i, carry):
            @pl.when(zb_ref[i] >= 0)
            def _():
                pltpu.make_async_copy(zero_ref, xs_hbm.at[pl.ds(0, ROW_BLOCK * ROW_TILES)], zsem).wait()
            return carry

        lax.fori_loop(0, n_fill, fill, 0)
        lax.fori_loop(0, n_fill, drain, 0)

    def row(t, carry):
        for k in range(TOP_K):
            pltpu.make_async_copy(_row(x_ref, t), _row(xs_hbm, dest_ref[k, t]), sem).start(
                priority=k % 2)
        return carry

    lax.fori_loop(0, tm, row, 0)
    for _ in range(TOP_K):
        pltpu.make_async_copy(x_ref, xs_hbm.at[pl.ds(0, tm * ROW_TILES)], sem).wait()


def _dispatch(zero_blocks, dest, x_packed, n_rows, tm=256):
    t = x_packed.shape[0] // ROW_TILES
    tm = min(tm, t)
    return pl.pallas_call(
        functools.partial(_dispatch_body, tm=tm, n_fill=zero_blocks.shape[0]),
        grid_spec=pltpu.PrefetchScalarGridSpec(
            num_scalar_prefetch=1, grid=(t // tm,),
            in_specs=[pl.BlockSpec((TOP_K, tm), lambda i, zb: (0, i), memory_space=pltpu.SMEM),
                      pl.BlockSpec((tm * ROW_TILES, LANES), lambda i, zb: (i, 0))],
            out_specs=pl.BlockSpec(memory_space=pl.ANY),
            scratch_shapes=[pltpu.VMEM((ROW_BLOCK * ROW_TILES, LANES), U32), pltpu.SemaphoreType.DMA(()),
                            pltpu.SemaphoreType.DMA(())]),
        out_shape=jax.ShapeDtypeStruct((n_rows * ROW_TILES, LANES), U32),
        compiler_params=_cparams(("arbitrary",)),
        name="dispatch",
    )(zero_blocks, dest, x_packed)


def _experts_body(be_ref, nu_ref, nxt_ref, slot_ref, xs_ref, wg_hbm, wu_hbm, wd_hbm, ys_ref,
                  wgf_ref, wuf_ref, wdf_ref, wgb_ref, wub_ref, wdb_ref, sem, *, layer):
    b = pl.program_id(0)
    used = b < nu_ref[0]
    new_expert = jnp.logical_or(b == 0, be_ref[b] != be_ref[jnp.maximum(b - 1, 0)])

    def weight_copies(e, s):
        return (pltpu.make_async_copy(wg_hbm.at[layer, e], wgf_ref.at[s], sem.at[s]),
                pltpu.make_async_copy(wu_hbm.at[layer, e], wuf_ref.at[s], sem.at[s]),
                pltpu.make_async_copy(wd_hbm.at[layer, e], wdf_ref.at[s], sem.at[s]))

    @pl.when(b == 0)
    def _():
        for c in weight_copies(be_ref[0], slot_ref[0]):
            c.start()

    @pl.when(jnp.logical_and(used, new_expert))
    def _():
        s = slot_ref[b]
        for c in weight_copies(be_ref[b], s):
            c.wait()

        @pl.when(nxt_ref[b] >= 0)
        def _():
            for c in weight_copies(nxt_ref[b], 1 - s):
                c.start()

        wgb_ref[...] = wgf_ref[s].astype(BF16)
        wub_ref[...] = wuf_ref[s].astype(BF16)
        wdb_ref[...] = wdf_ref[s].astype(BF16)

    @pl.when(used)
    def _():
        half = D_MODEL // 2
        lo, hi = _unpack_pair(_load_rows(xs_ref))
        lo = lo.astype(BF16)
        hi = hi.astype(BF16)
        gate = (jnp.dot(lo, wgb_ref[:half, :], preferred_element_type=F32)
                + jnp.dot(hi, wgb_ref[half:, :], preferred_element_type=F32))
        up = (jnp.dot(lo, wub_ref[:half, :], preferred_element_type=F32)
              + jnp.dot(hi, wub_ref[half:, :], preferred_element_type=F32))
        h = (_silu(gate) * up).astype(BF16)
        y = jnp.dot(h, wdb_ref[...], preferred_element_type=F32)
        _store_rows(ys_ref, _pack_pair(y[:, :half], y[:, half:]))

    @pl.when(jnp.logical_not(used))
    def _():
        ys_ref[...] = jnp.zeros_like(ys_ref)


def _experts(block_expert, n_used, next_expert, slot, xs, wg, wu, wd, layer):
    n_rows = xs.shape[0] // ROW_TILES
    n_blocks = n_rows // ROW_BLOCK
    blk = lambda b, be, nu, nx, sl: (jnp.minimum(b, nu[0] - 1), 0)
    out_blk = lambda b, be, nu, nx, sl: (b, 0)
    hbm = pl.BlockSpec(memory_space=pl.ANY)
    return pl.pallas_call(
        functools.partial(_experts_body, layer=layer),
        grid_spec=pltpu.PrefetchScalarGridSpec(
            num_scalar_prefetch=4, grid=(n_blocks,),
            in_specs=[pl.BlockSpec((ROW_BLOCK * ROW_TILES, LANES), blk), hbm, hbm, hbm],
            out_specs=pl.BlockSpec((ROW_BLOCK * ROW_TILES, LANES), out_blk),
            scratch_shapes=[pltpu.VMEM((2, D_MODEL, EXPERT_W), F32), pltpu.VMEM((2, D_MODEL, EXPERT_W), F32),
                            pltpu.VMEM((2, EXPERT_W, D_MODEL), F32),
                            pltpu.VMEM((D_MODEL, EXPERT_W), BF16), pltpu.VMEM((D_MODEL, EXPERT_W), BF16),
                            pltpu.VMEM((EXPERT_W, D_MODEL), BF16), pltpu.SemaphoreType.DMA((2,))]),
        out_shape=jax.ShapeDtypeStruct((n_rows * ROW_TILES, LANES), U32),
        compiler_params=_cparams(("arbitrary",), 56),
        name="experts",
    )(block_expert, n_used, next_expert, slot, xs, wg, wu, wd)


def _combine_body(dest_ref, w_ref, x_ref, xb_ref, sg_ref, su_ref, sd_ref, lnw_ref, lnb_ref, ys_hbm,
                  xo_ref, xb_out_ref, buf_ref, lo_ref, hi_ref, sem, *, tm):
    def row(t, carry):
        for k in range(TOP_K):
            pltpu.make_async_copy(_row(ys_hbm, dest_ref[k, t]), _row(buf_ref.at[k], t), sem).start(
                priority=k % 2)
        return carry

    lax.fori_loop(0, tm, row, 0)

    xb = xb_ref[...]
    h = _silu(jnp.dot(xb, sg_ref[...], preferred_element_type=F32)) * jnp.dot(xb, su_ref[...],
                                                                           preferred_element_type=F32)
    shared = jnp.dot(h.astype(BF16), sd_ref[...], preferred_element_type=F32)

    for k in range(TOP_K):
        pltpu.make_async_copy(ys_hbm.at[pl.ds(0, tm * ROW_TILES)], buf_ref.at[k], sem).wait()

    lo_acc = hi_acc = None
    for k in range(TOP_K):
        lo, hi = _unpack_pair(buf_ref[k])
        wk = w_ref[:, k:k + 1]
        lo_acc = wk * lo if lo_acc is None else lo_acc + wk * lo
        hi_acc = wk * hi if hi_acc is None else hi_acc + wk * hi
    lo_ref[...] = lo_acc
    hi_ref[...] = hi_acc
    routed = jnp.concatenate([_load_rows(lo_ref), _load_rows(hi_ref)], axis=1)
    y = _layer_norm(DEEPNORM_ALPHA * x_ref[...] + routed + shared, lnw_ref[...], lnb_ref[...])
    xo_ref[...] = y
    xb_out_ref[...] = y.astype(BF16)


def _combine(dest, w_sel, x, xb, sg, su, sd, ln_w, ln_b, ys, tm=256):
    t = x.shape[0]
    tm = min(tm, t)
    row = lambda i: (i, 0)
    slot = lambda i: (0, i)
    const = lambda i: (0, 0)
    return pl.pallas_call(
        functools.partial(_combine_body, tm=tm),
        grid=(t // tm,),
        in_specs=[pl.BlockSpec((TOP_K, tm), slot, memory_space=pltpu.SMEM),
                  pl.BlockSpec((tm * ROW_TILES, TOP_K), row),
                  pl.BlockSpec((tm, D_MODEL), row),
                  pl.BlockSpec((tm, D_MODEL), row),
                  pl.BlockSpec((D_MODEL, EXPERT_W), const),
                  pl.BlockSpec((D_MODEL, EXPERT_W), const),
                  pl.BlockSpec((EXPERT_W, D_MODEL), const),
                  pl.BlockSpec((1, D_MODEL), const),
                  pl.BlockSpec((1, D_MODEL), const),
                  pl.BlockSpec(memory_space=pl.ANY)],
        out_specs=[pl.BlockSpec((tm, D_MODEL), row), pl.BlockSpec((tm, D_MODEL), row)],
        out_shape=[jax.ShapeDtypeStruct((t, D_MODEL), F32), jax.ShapeDtypeStruct((t, D_MODEL), BF16)],
        scratch_shapes=[pltpu.VMEM((TOP_K, tm * ROW_TILES, LANES), U32),
                        pltpu.VMEM((tm * ROW_TILES, LANES), F32), pltpu.VMEM((tm * ROW_TILES, LANES), F32),
                        pltpu.SemaphoreType.DMA(())],
        compiler_params=_cparams(("arbitrary",)),
        name="combine_ln",
    )(dest, w_sel, x, xb, sg, su, sd, ln_w, ln_b, ys)


def _pad_cols(a, width):
    return jnp.pad(a, ((0, 0), (0, width - a.shape[1])))


def _pad_rows(a, height):
    return jnp.pad(a, ((0, height - a.shape[0]), (0, 0)))


def _pad_heads(a, heads, dk):
    r = a.shape[0]
    return jnp.pad(a.reshape(r, heads, dk), ((0, 0), (0, 0), (0, HEAD_PAD - dk))).reshape(r, heads * HEAD_PAD)


def _token_mixing(xb, w_in, lower_bound, hg_norm_w, rw_mu, rw_w0, rw_w2, rw_a0, rw_a2, rw_g2, rw_kk, rw_ka,
                  rw_rk, rw_ln_w, rw_ln_b, gla_g2, gla_gb, gla_norm_w):
    row = lambda a: a.reshape(1, -1).astype(F32)

    w_rw = w_in[:, RW_OFF:SB_OFF]
    w_sb = w_in[:, SB_OFF:GLA_OFF]
    w_gla = w_in[:, GLA_OFF:GATE_OFF]
    zcol = lambda n: jnp.zeros((D_MODEL, n), w_in.dtype)
    w_all = jnp.concatenate([
        w_in[:, HG_OFF:RW_OFF],
        w_rw[:, :1536], _pad_cols(w_rw[:, 1536:1568], LANES), _pad_cols(w_rw[:, 1568:1600], LANES),
        _pad_cols(w_rw[:, 1600:1696], LANES), zcol(RW_Z - 3 * BRANCH_W - 3 * LANES),
        w_sb[:, :BRANCH_W] * (SB_DIM ** -0.5), w_sb[:, BRANCH_W:],
        _pad_heads(w_gla[:, :256], 4, GLA_DK), _pad_heads(w_gla[:, 256:512], 4, GLA_DK),
        w_gla[:, 512:1024], w_gla[:, 1040:1552], _pad_cols(w_gla[:, 1024:1040], LANES),
        zcol(GLA_Z - 4 * BRANCH_W - LANES),
        w_in[:, GATE_OFF:]], axis=1).astype(BF16)

    z_hg = _matmul(xb, w_all, ZHG_OFF, 4 * BRANCH_W, F32, tn=2 * PROJ_TN)
    o_hg = _gated_mixer(z_hg, row(lower_bound), row(hg_norm_w), jnp.zeros((8, BRANCH_W), F32), "hgrn2")

    z_rw = _matmul(xb, w_all, ZRW_OFF, RW_Z, F32, tn=2 * PROJ_TN)
    mu = jnp.concatenate([rw_mu[:1536], jnp.pad(rw_mu[1536:1568], (0, 96)), jnp.pad(rw_mu[1568:1600], (0, 96)),
                          jnp.pad(rw_mu[1600:1696], (0, 32 + RW_Z - 3 * BRANCH_W - 3 * LANES))]).reshape(1, RW_Z)
    vecs = jnp.stack([rw_w0, rw_a0, rw_kk, rw_ka, rw_rk, rw_ln_w, rw_ln_b, jnp.zeros_like(rw_w0)]).astype(F32)
    o_rw = _rwkv_mixer(z_rw, mu, vecs, _pad_rows(rw_w2, LANES), _pad_rows(rw_a2, LANES), _pad_rows(rw_g2, LANES))

    o_sb = _sb_attention(_matmul(xb, w_all, ZSB_OFF, 3 * BRANCH_W, BF16))

    z_gla = _matmul(xb, w_all, ZGLA_OFF, GLA_Z, F32)
    g2p = _pad_rows(_pad_heads(gla_g2, 4, GLA_DK), LANES)
    gbp = _pad_heads(gla_gb.reshape(1, -1), 4, GLA_DK)
    o_gla = _gated_mixer(z_gla, gbp, row(gla_norm_w), g2p, "gla")

    gates = _matmul(xb, w_all, ZGATE_OFF, N_BRANCHES * D_MODEL, BF16, act="sigmoid", tn=1024)
    return (o_hg, o_rw, o_sb, o_gla), gates


def _moe(x, xb, xp, w_router, router_bias, we_gate, we_up, we_down, layer, ws_gate, ws_up, ws_down, ln_w, ln_b):
    t = x.shape[0]
    bias = jnp.broadcast_to(router_bias.astype(F32).reshape(N_EXPERTS, 1), (N_EXPERTS, LANES))
    idx, w_sel, rank, counts = _router(x, w_router.T.astype(F32), bias)

    cnt = counts[:, 0].astype(I32)
    padded = (cnt + ROW_BLOCK - 1) // ROW_BLOCK * ROW_BLOCK
    pad_end = jnp.cumsum(padded)
    pad_start = (pad_end - padded).astype(I32)
    n_blocks = t * TOP_K // ROW_BLOCK + N_EXPERTS
    n_used = (pad_end[-1:] // ROW_BLOCK).astype(I32)
    first_row = jnp.arange(n_blocks, dtype=I32) * ROW_BLOCK
    block_expert = jnp.minimum(jnp.sum(pad_end[None, :] <= first_row[:, None], axis=1), N_EXPERTS - 1).astype(I32)

    last_block = jnp.where(padded > 0, pad_end // ROW_BLOCK - 1, -1)
    tail = n_used[0] + jnp.arange(N_EXPERTS, dtype=I32)
    zero_blocks = jnp.concatenate([last_block, jnp.where(tail < n_blocks, tail, -1)]).astype(I32)

    has_rows = cnt > 0
    eid = jnp.arange(N_EXPERTS, dtype=I32)
    later = jnp.where(has_rows[None, :] & (eid[None, :] > eid[:, None]), eid[None, :], N_EXPERTS)
    next_used = jnp.min(later, axis=1)
    next_used = jnp.where(next_used < N_EXPERTS, next_used, -1).astype(I32)
    ordinal = jnp.cumsum(has_rows.astype(I32)) - 1
    next_expert = next_used[block_expert]
    slot = (ordinal[block_expert] % 2).astype(I32)

    dest = _dest_rows(pad_start, idx, rank)
    xs = _dispatch(zero_blocks, dest, xp, n_blocks * ROW_BLOCK)
    ys = _experts(block_expert, n_used, next_expert, slot, xs, we_gate, we_up, we_down, layer)
    w_rows = jnp.repeat(w_sel.T, ROW_TILES, axis=0)
    return _combine(dest, w_rows, x, xb, ws_gate.astype(BF16), ws_up.astype(BF16), ws_down.astype(BF16),
                    ln_w.reshape(1, -1), ln_b.reshape(1, -1), ys)


def kernel(x, w_in, hg_lb_logits, hg_norm_w, rw_mu, rw_w0, rw_w2, rw_a0, rw_a2, rw_g2, rw_kk, rw_ka, rw_rk,
           rw_ln_w, rw_ln_b, gla_g2, gla_gb, gla_norm_w, w_br, w_out, ln1_w, ln1_b, w_router, router_bias,
           we_gate, we_up, we_down, ws_gate, ws_up, ws_down, ln2_w, ln2_b):
    bsz, t, d = x.shape
    cum = jnp.cumsum(jax.nn.softmax(hg_lb_logits.astype(F32), axis=0), axis=0)
    lower_bounds = cum - cum[0:1]
    outs = []
    for bi in range(bsz):
        xf = x[bi].astype(F32)
        xb = xf.astype(BF16)
        for l in range(DEPTH):
            branches, gates = _token_mixing(
                xb, w_in[l], lower_bounds[l], hg_norm_w[l], rw_mu[l], rw_w0[l], rw_w2[l], rw_a0[l], rw_a2[l],
                rw_g2[l], rw_kk[l], rw_ka[l], rw_rk[l], rw_ln_w[l], rw_ln_b[l], gla_g2[l], gla_gb[l],
                gla_norm_w[l])
            xf, xb, xp = _merge(branches, gates, w_br[l].astype(BF16), w_out[l].astype(BF16), xf,
                                ln1_w[l].reshape(1, -1), ln1_b[l].reshape(1, -1))
            xf, xb = _moe(xf, xb, xp, w_router[l], router_bias[l], we_gate, we_up, we_down, l,
                          ws_gate[l], ws_up[l], ws_down[l], ln2_w[l], ln2_b[l])
        outs.append(xf)
    return jnp.stack(outs).astype(x.dtype)
```

```python
import functools

import jax
import jax.numpy as jnp
import numpy as np
from jax import lax
from jax.experimental import pallas as pl
from jax.experimental.pallas import tpu as pltpu

F32 = jnp.float32
BF16 = jnp.bfloat16
I32 = jnp.int32
U32 = jnp.uint32
HI = lax.Precision.HIGHEST

D_MODEL = 2048
DEPTH = 2
BRANCH_W = 512
N_BRANCHES = 4
CHUNK = 64
SUB = 16
N_SUB = CHUNK // SUB
LANES = 128
HEAD_PAD = 128
RW_HEADS = 8
RW_DIM = 64
RW_PAIRS = RW_HEADS // 2
GATED_SPAN = 2
RW_SPAN = 2
GLA_DK = 64
GLA_TAU = 16.0
SB_DIM = 128
SB_BLOCK = 256
SB_PARTS = 2
SB_SKIP = 120.0
N_EXPERTS = 64
TOP_K = 8
N_GROUPS = 8
TOPK_GROUPS = 4
EXPERT_W = 512
ROUTED_SCALE = 2.5
ROW_BLOCK = 256
DEEPNORM_ALPHA = (2 * DEPTH) ** 0.25
LN_EPS = 1e-5
RW_GN_EPS = 64e-5
LOG2E = 1.4426950408889634

HG_OFF = 0
RW_OFF = 2048
SB_OFF = 3744
GLA_OFF = 5280
GATE_OFF = 6832
PROJ_TN = 512
RW_Z = 2048
GLA_Z = 2560
ZHG_OFF = 0
ZRW_OFF = ZHG_OFF + 4 * BRANCH_W
ZSB_OFF = ZRW_OFF + RW_Z
ZGLA_OFF = ZSB_OFF + 3 * BRANCH_W
ZGATE_OFF = ZGLA_OFF + GLA_Z
Z_TOTAL = ZGATE_OFF + N_BRANCHES * D_MODEL

_NT = (((1,), (1,)), ((), ()))
_TN = (((0,), (0,)), ((), ()))


def _cparams(sem, vmem_mb=48):
    return pltpu.CompilerParams(dimension_semantics=sem, vmem_limit_bytes=vmem_mb << 20)


def _sigmoid(x):
    return 1.0 / (1.0 + jnp.exp(-x))


def _log_sigmoid(x):
    return jnp.minimum(x, 0.0) - jnp.log1p(jnp.exp(-jnp.abs(x)))


def _silu(x):
    return x * _sigmoid(x)


def _layer_norm(y, w, b):
    mu = jnp.mean(y, axis=-1, keepdims=True)
    d = y - mu
    var = jnp.mean(d * d, axis=-1, keepdims=True)
    return d * lax.rsqrt(var + LN_EPS) * w + b


def _pack_pair(lo, hi):
    lo_b = lax.bitcast_convert_type(lo.astype(BF16).astype(F32), U32) >> 16
    hi_b = lax.bitcast_convert_type(hi.astype(BF16).astype(F32), U32) & jnp.uint32(0xFFFF0000)
    return lo_b | hi_b


def _unpack_pair(u):
    lo = lax.bitcast_convert_type(u << 16, F32)
    hi = lax.bitcast_convert_type(u & jnp.uint32(0xFFFF0000), F32)
    return lo, hi


ROW_TILES = D_MODEL // 2 // LANES


def _store_rows(ref, packed):
    r = packed.shape[0]
    for j in range(ROW_TILES):
        ref[pl.ds(j, r, stride=ROW_TILES), :] = packed[:, j * LANES:(j + 1) * LANES]


def _load_rows(ref):
    r = ref.shape[0] // ROW_TILES
    return jnp.concatenate([ref[pl.ds(j, r, stride=ROW_TILES), :] for j in range(ROW_TILES)], axis=1)


def _row(ref, i):
    return ref.at[pl.ds(pl.multiple_of(i * ROW_TILES, ROW_TILES), ROW_TILES)]


def _dot2(x, ones_bf16):
    hi = x.astype(BF16)
    lo = (x - hi.astype(F32)).astype(BF16)
    return (jnp.dot(hi, ones_bf16, preferred_element_type=F32)
            + jnp.dot(lo, ones_bf16, preferred_element_type=F32))


def _mm_body(x_ref, w_ref, o_ref, *, act):
    acc = jnp.dot(x_ref[...], w_ref[...], preferred_element_type=F32)
    if act == "sigmoid":
        acc = _sigmoid(acc)
    o_ref[...] = acc.astype(o_ref.dtype)


def _matmul(x, w, layer, col0, n, out_dtype, act=None, tm=1024, tn=PROJ_TN):
    m, k = x.shape
    tm = min(tm, m)
    off = col0 // tn
    return pl.pallas_call(
        functools.partial(_mm_body, act=act),
        grid=(m // tm, n // tn),
        in_specs=[pl.BlockSpec((tm, k), lambda i, j: (i, 0)),
                  pl.BlockSpec((None, k, tn), lambda i, j: (layer, 0, off + j))],
        out_specs=pl.BlockSpec((tm, tn), lambda i, j: (i, j)),
        out_shape=jax.ShapeDtypeStruct((m, n), out_dtype),
        compiler_params=_cparams(("parallel", "parallel")),
        name="proj",
    )(x, w)


def _gated_span(q, k, v, g, st, tril):
    hs = range(len(q))
    lanes = [(ci, h) for ci in range(GATED_SPAN) for h in hs]
    ls = range(len(lanes))
    subs = range(N_SUB)
    dotf = functools.partial(jnp.dot, preferred_element_type=F32)
    blk = lambda x, i: x[i * SUB:(i + 1) * SUB]
    cut = lambda xs: [xs[h][ci * CHUNK:(ci + 1) * CHUNK] for ci, h in lanes]
    b_f = [jnp.dot(tril, g[h] * LOG2E, precision=HI, preferred_element_type=F32) for h in hs]
    qe_f = [(q[h] * jnp.exp2(b_f[h])).astype(BF16) for h in hs]
    vb_f = [v[h].astype(BF16) for h in hs]
    b, qc, kc, vb, qe = cut(b_f), cut(q), cut(k), cut(vb_f), cut(qe_f)
    blast = [b[i][CHUNK - 1:CHUNK, :] for i in ls]
    ends = [[b[i][(j + 1) * SUB - 1:(j + 1) * SUB, :] for j in subs] for i in ls]
    khat = [[blk(kc[i], j) * jnp.exp2(ends[i][j] - blk(b[i], j)) for j in subs] for i in ls]
    o_parts = [[None] * N_SUB for _ in ls]

    for j in range(N_SUB - 1):
        lo = (j + 1) * SUB
        qs = [(qc[i][lo:] * jnp.exp2(b[i][lo:] - ends[i][j])).astype(BF16) for i in ls]
        a = [lax.dot_general(qs[i], khat[i][j].astype(BF16), _NT, preferred_element_type=F32) for i in ls]
        pv = [dotf(a[i].astype(BF16), blk(vb[i], j)) for i in ls]
        for i in ls:
            for t in range(j + 1, N_SUB):
                piece = pv[i][(t - j - 1) * SUB:(t - j) * SUB]
                o_parts[i][t] = piece if o_parts[i][t] is None else o_parts[i][t] + piece

    lane = lax.broadcasted_iota(I32, (SUB, LANES), 1)
    trow = lax.broadcasted_iota(I32, (SUB, 1), 0)
    half = SUB // 2
    for t in subs:
        d_top = [jnp.zeros((half, LANES), F32) for _ in ls]
        d_bot = [jnp.zeros((half, LANES), F32) for _ in ls]
        for s in range(SUB):
            for i in ls:
                bi, qi, ki = blk(b[i], t), blk(qc[i], t), blk(kc[i], t)
                if s < half:
                    col = jnp.sum(qi * jnp.exp2(bi - bi[s:s + 1, :]) * ki[s:s + 1, :], axis=-1, keepdims=True)
                    d_top[i] = jnp.where(lane[:half] == s, col[:half], d_top[i])
                    d_bot[i] = jnp.where(lane[:half] == s, col[half:], d_bot[i])
                else:
                    col = jnp.sum(qi[half:] * jnp.exp2(bi[half:] - bi[s:s + 1, :]) * ki[s:s + 1, :],
                                  axis=-1, keepdims=True)
                    d_bot[i] = jnp.where(lane[:half] == s, col, d_bot[i])
        d = [jnp.concatenate([d_top[i], d_bot[i]], axis=0) for i in ls]
        pv = [dotf(jnp.where(lane <= trow, d[i], 0.0)[:, :SUB].astype(BF16), blk(vb[i], t)) for i in ls]
        for i in ls:
            o_parts[i][t] = pv[i] if o_parts[i][t] is None else o_parts[i][t] + pv[i]

    o_intra = [jnp.concatenate(o_parts[i], axis=0) for i in ls]
    kd = [jnp.concatenate([khat[i][j] * jnp.exp2(blast[i] - ends[i][j]) for j in subs], axis=0).astype(BF16)
          for i in ls]
    kv = [lax.dot_general(vb[i], kd[i], _TN, preferred_element_type=F32) for i in ls]
    dec = [jnp.exp2(blast[i]) for i in ls]

    state = list(st)
    o_ch = []
    for ci in range(GATED_SPAN):
        idx = [ci * len(q) + h for h in hs]
        o_ch.append([lax.dot_general(qe[idx[h]], state[h].astype(BF16), _NT, preferred_element_type=F32)
                     + o_intra[idx[h]] for h in hs])
        state = [state[h] * dec[idx[h]] + kv[idx[h]] for h in hs]
    o = [jnp.concatenate([o_ch[ci][h] for ci in range(GATED_SPAN)], axis=0) for h in hs]
    return o, state


def _gated_body(z_ref, aux_ref, nw_ref, g2_ref, tril_ref, o_ref, st_ref, *, mode, rows):
    @pl.when(pl.program_id(0) == 0)
    def _():
        st_ref[...] = jnp.zeros_like(st_ref)

    tril = tril_ref[...]
    span = GATED_SPAN * CHUNK

    def step(c, carry):
        r0 = pl.multiple_of(c * span, span)
        rs = pl.ds(r0, span)
        hs = range(4)
        sls = [slice(h * HEAD_PAD, (h + 1) * HEAD_PAD) for h in hs]
        zq = [z_ref[rs, h * HEAD_PAD:(h + 1) * HEAD_PAD] for h in hs]
        zk = [z_ref[rs, BRANCH_W + h * HEAD_PAD:BRANCH_W + (h + 1) * HEAD_PAD] for h in hs]
        v = [z_ref[rs, 2 * BRANCH_W + h * HEAD_PAD:2 * BRANCH_W + (h + 1) * HEAD_PAD] for h in hs]
        if mode == "hgrn2":
            q = [_silu(zq[h]) for h in hs]
            k = [(1.0 - aux_ref[0:1, sls[h]]) * _sigmoid(-zk[h]) for h in hs]
            g = [jnp.log1p(-k[h]) for h in hs]
        else:
            q = [zq[h] * (GLA_DK ** -0.5) for h in hs]
            k = zk
            la = jnp.dot(z_ref[rs, 4 * BRANCH_W:4 * BRANCH_W + LANES], g2_ref[...], precision=HI,
                         preferred_element_type=F32) + aux_ref[...]
            g = [_log_sigmoid(la[:, sls[h]]) * (1.0 / GLA_TAU) for h in hs]
        o, st_new = _gated_span(q, k, v, g, [st_ref[h] for h in hs], tril)
        for h in hs:
            st_ref[h] = st_new[h]
            gate = z_ref[rs, 3 * BRANCH_W + h * HEAD_PAD:3 * BRANCH_W + (h + 1) * HEAD_PAD]
            on = o[h] * lax.rsqrt(jnp.mean(o[h] * o[h], axis=-1, keepdims=True) + LN_EPS)
            o_ref[rs, sls[h]] = (on * nw_ref[0:1, sls[h]] * _silu(gate)).astype(o_ref.dtype)
        return carry

    lax.fori_loop(0, rows // span, step, 0)


def _gated_mixer(z, aux, norm_w, g2, mode, rows=256):
    t, wz = z.shape
    rows = min(rows, t)
    span = GATED_SPAN * CHUNK
    pos = np.arange(span)
    tril = jnp.asarray(((pos[:, None] >= pos[None, :]) & (pos[:, None] // CHUNK == pos[None, :] // CHUNK))
                       .astype(np.float32))
    return pl.pallas_call(
        functools.partial(_gated_body, mode=mode, rows=rows),
        grid=(t // rows,),
        in_specs=[pl.BlockSpec((rows, wz), lambda i: (i, 0)),
                  pl.BlockSpec((1, BRANCH_W), lambda i: (0, 0)),
                  pl.BlockSpec((1, BRANCH_W), lambda i: (0, 0)),
                  pl.BlockSpec(g2.shape, lambda i: (0, 0)),
                  pl.BlockSpec((span, span), lambda i: (0, 0))],
        out_specs=pl.BlockSpec((rows, BRANCH_W), lambda i: (i, 0)),
        out_shape=jax.ShapeDtypeStruct((t, BRANCH_W), BF16),
        scratch_shapes=[pltpu.VMEM((4, HEAD_PAD, HEAD_PAD), F32)],
        compiler_params=_cparams(("arbitrary",)),
        name="gated_" + mode,
    )(z, aux, norm_w, g2, tril)


def _rwkv_body(z_ref, mu_ref, vec_ref, w2_ref, a2_ref, g2_ref, tril_ref, ones2_ref, o_ref,
               prev_ref, st_ref, *, rows):
    @pl.when(pl.program_id(0) == 0)
    def _():
        prev_ref[...] = jnp.zeros_like(prev_ref)
        st_ref[...] = jnp.zeros_like(st_ref)

    tril = tril_ref[...]
    ones2 = ones2_ref[...]
    row128 = lax.broadcasted_iota(I32, (2 * CHUNK, 2 * CHUNK), 0)
    col128 = lax.broadcasted_iota(I32, (2 * CHUNK, 2 * CHUNK), 1)
    rt = jnp.where(row128 >= CHUNK, row128 - CHUNK, row128)
    ct = jnp.where(col128 >= CHUNK, col128 - CHUNK, col128)
    strict = rt > ct
    incl = rt >= ct
    eye = row128 == col128
    head0 = lax.broadcasted_iota(I32, (1, LANES), 1) < RW_DIM
    span = RW_SPAN * CHUNK
    first_row = lax.broadcasted_iota(I32, (span, 1), 0) == 0

    def stack(x):
        return jnp.concatenate([jnp.where(head0, x, 0.0), jnp.where(head0, 0.0, x)], axis=0)

    def step(c, carry):
        r0 = pl.multiple_of(c * span, span)
        rs = pl.ds(r0, span)
        z = z_ref[rs, :]
        zprev = jnp.where(first_row, prev_ref[...], pltpu.roll(z, 1, axis=0))
        prev_ref[...] = z[span - 1:span, :]
        zs = z + (zprev - z) * mu_ref[...]
        lw = zs[:, 3 * BRANCH_W:3 * BRANCH_W + LANES]
        la = zs[:, 3 * BRANCH_W + LANES:3 * BRANCH_W + 2 * LANES]
        lg = zs[:, 3 * BRANCH_W + 2 * LANES:3 * BRANCH_W + 3 * LANES]
        wl = -(vec_ref[0:1, :] + jnp.dot(jnp.tanh(lw), w2_ref[...], precision=HI, preferred_element_type=F32))
        w_raw = -(jnp.maximum(wl, 0.0) + jnp.log1p(jnp.exp(-jnp.abs(wl)))) - 0.5
        logw_all = -jnp.exp(w_raw)
        iclr_all = _sigmoid(vec_ref[1:2, :] + jnp.dot(la, a2_ref[...], precision=HI, preferred_element_type=F32))
        gate_all = jnp.dot(_sigmoid(lg), g2_ref[...], precision=HI, preferred_element_type=F32)
        cum_all = jnp.dot(tril, logw_all, precision=HI, preferred_element_type=F32)

        pairs = range(RW_PAIRS)
        sls = [slice(p * LANES, (p + 1) * LANES) for p in pairs]
        dotf = functools.partial(jnp.dot, preferred_element_type=F32)
        c2 = 2 * CHUNK
        r_f = [zs[:, p * LANES:(p + 1) * LANES] for p in pairs]
        k_f = [zs[:, BRANCH_W + p * LANES:BRANCH_W + (p + 1) * LANES] for p in pairs]
        v_f = [zs[:, 2 * BRANCH_W + p * LANES:2 * BRANCH_W + (p + 1) * LANES] for p in pairs]
        cum_f = [cum_all[:, s] for s in sls]
        iclr_f = [iclr_all[:, s] for s in sls]
        kkr = [k_f[p] * vec_ref[2:3, sls[p]] for p in pairs]
        ssq = [_dot2(kkr[p] * kkr[p], ones2) for p in pairs]
        kk_f = [kkr[p] / jnp.maximum(jnp.sqrt(ssq[p]), 1e-12) for p in pairs]
        kmod_f = [k_f[p] * (1.0 + (iclr_f[p] - 1.0) * vec_ref[3:4, sls[p]]) for p in pairs]
        bb_f = [kk_f[p] * iclr_f[p] for p in pairs]
        eneg_f = [jnp.exp(-cum_f[p]) for p in pairs]
        a_f = [-kk_f[p] * jnp.exp(cum_f[p] - logw_all[:, sls[p]]) for p in pairs]
        rd_f = [r_f[p] * jnp.exp(cum_f[p]) for p in pairs]
        bn_f = [bb_f[p] * eneg_f[p] for p in pairs]
        kn_f = [kmod_f[p] * eneg_f[p] for p in pairs]

        lanes = [(ci, p) for ci in range(RW_SPAN) for p in pairs]
        ls = range(len(lanes))
        cut = lambda xs: [xs[p][ci * CHUNK:(ci + 1) * CHUNK] for ci, p in lanes]
        cum, bb, kmod = cut(cum_f), cut(bb_f), cut(kmod_f)
        clast = [cum[i][CHUNK - 1:CHUNK, :] for i in ls]
        e_end = [jnp.exp(clast[i] - cum[i]) for i in ls]
        a_s = [stack(x) for x in cut(a_f)]
        r_s = [stack(x) for x in cut(rd_f)]
        b_s = [stack(x) for x in cut(bn_f)]
        k_s = [stack(x) for x in cut(kn_f)]
        bh_s = [stack(bb[i] * e_end[i]).astype(BF16) for i in ls]
        kh_s = [stack(kmod[i] * e_end[i]).astype(BF16) for i in ls]
        v_sb = [stack(x).astype(BF16) for x in cut(v_f)]

        left = [jnp.concatenate([a_s[i], r_s[i]], axis=0).astype(BF16) for i in ls]
        right = [jnp.concatenate([b_s[i], k_s[i]], axis=0).astype(BF16) for i in ls]
        gram = [lax.dot_general(left[i], right[i], _NT, preferred_element_type=F32) for i in ls]
        n_ab = [jnp.where(strict, gram[i][:c2, :c2], 0.0) for i in ls]
        a_ak = [jnp.where(strict, gram[i][:c2, c2:], 0.0).astype(BF16) for i in ls]
        g_bk = [jnp.concatenate([jnp.where(incl, gram[i][c2:, :c2], 0.0),
                                 jnp.where(incl, gram[i][c2:, c2:], 0.0)], axis=1).astype(BF16) for i in ls]

        x = [jnp.where(eye, 1.0, 0.0) + n_ab[i] for i in ls]
        pw = n_ab
        for _ in range(5):
            pwb = [pw[i].astype(BF16) for i in ls]
            pw = [dotf(pwb[i], pwb[i]) for i in ls]
            x = [x[i] + dotf(pw[i].astype(BF16), x[i].astype(BF16)) for i in ls]

        av = [dotf(a_ak[i], v_sb[i]) for i in ls]
        pqb = [dotf(x[i].astype(BF16), jnp.concatenate([a_s[i], av[i]], axis=1).astype(BF16)).astype(BF16)
               for i in ls]
        zero_blk = jnp.zeros((c2, LANES), BF16)
        top = [dotf(g_bk[i], jnp.concatenate([pqb[i], jnp.concatenate([zero_blk, v_sb[i]], axis=1)], axis=0))
               for i in ls]
        bot = [lax.dot_general(bh_s[i], pqb[i], _TN, preferred_element_type=F32) for i in ls]
        kv = [lax.dot_general(kh_s[i], v_sb[i], _TN, preferred_element_type=F32) for i in ls]
        r2 = [(r_s[i] + top[i][:, :LANES]).astype(BF16) for i in ls]
        tm = [(bot[i][:, :LANES] + jnp.where(eye, jnp.exp(clast[i]), 0.0)).astype(BF16) for i in ls]

        state = [st_ref[p] for p in pairs]
        o_ch = []
        for ci in range(RW_SPAN):
            idx = [ci * RW_PAIRS + p for p in pairs]
            s_old = [state[p].astype(BF16) for p in pairs]
            o_st = [dotf(r2[idx[p]], s_old[p]) + top[idx[p]][:, LANES:] for p in pairs]
            state = [dotf(tm[idx[p]], s_old[p]) + bot[idx[p]][:, LANES:] + kv[idx[p]] for p in pairs]
            o_ch.append([o_st[p][:CHUNK] + o_st[p][CHUNK:] for p in pairs])
        for p in pairs:
            st_ref[p] = state[p]
        o = [jnp.concatenate([o_ch[ci][p] for ci in range(RW_SPAN)], axis=0) for p in pairs]

        inv_n = 1.0 / RW_DIM
        mean = [_dot2(o[p], ones2) * inv_n for p in pairs]
        dlt = [o[p] - mean[p] for p in pairs]
        var = [_dot2(dlt[p] * dlt[p], ones2) * inv_n for p in pairs]
        bonus = [_dot2(r_f[p] * kmod_f[p] * vec_ref[4:5, sls[p]], ones2) * v_f[p] for p in pairs]
        for p in pairs:
            on = dlt[p] * lax.rsqrt(var[p] + RW_GN_EPS) * vec_ref[5:6, sls[p]] + vec_ref[6:7, sls[p]]
            o_ref[rs, sls[p]] = ((on + bonus[p]) * gate_all[:, sls[p]]).astype(o_ref.dtype)
        return carry

    lax.fori_loop(0, rows // span, step, 0)


def _rwkv_mixer(z, mu, vecs, w2, a2, g2, rows=256):
    t = z.shape[0]
    rows = min(rows, t)
    span = RW_SPAN * CHUNK
    pos = np.arange(span)
    tril = jnp.asarray(((pos[:, None] >= pos[None, :]) & (pos[:, None] // CHUNK == pos[None, :] // CHUNK))
                       .astype(np.float32))
    hid = np.arange(LANES) // RW_DIM
    ones2 = jnp.asarray((hid[:, None] == hid[None, :]).astype(np.float32), dtype=BF16)
    const = lambda i: (0, 0)
    return pl.pallas_call(
        functools.partial(_rwkv_body, rows=rows),
        grid=(t // rows,),
        in_specs=[pl.BlockSpec((rows, RW_Z), lambda i: (i, 0)),
                  pl.BlockSpec((1, RW_Z), const),
                  pl.BlockSpec((8, BRANCH_W), const),
                  pl.BlockSpec((LANES, BRANCH_W), const),
                  pl.BlockSpec((LANES, BRANCH_W), const),
                  pl.BlockSpec((LANES, BRANCH_W), const),
                  pl.BlockSpec((span, span), const),
                  pl.BlockSpec((LANES, LANES), const)],
        out_specs=pl.BlockSpec((rows, BRANCH_W), lambda i: (i, 0)),
        out_shape=jax.ShapeDtypeStruct((t, BRANCH_W), BF16),
        scratch_shapes=[pltpu.VMEM((1, RW_Z), F32), pltpu.VMEM((RW_PAIRS, LANES, LANES), F32)],
        compiler_params=_cparams(("arbitrary",)),
        name="rwkv7",
    )(z, mu, vecs, w2, a2, g2, tril, ones2)


def _sb_body(q_ref, k_ref, v_ref, upper_ref, o_ref, acc_ref, car_ref):
    i = pl.program_id(1)
    q = q_ref[...]
    upper = upper_ref[...]
    acc_ref[...] = jnp.zeros_like(acc_ref)
    car_ref[...] = jnp.zeros_like(car_ref)
    parts = range(SB_PARTS)
    rows = SB_BLOCK // SB_PARTS
    cut = lambda a, p: a[p * rows:(p + 1) * rows]
    dotf = functools.partial(jnp.dot, preferred_element_type=F32)
    qpos = [i * SB_BLOCK + p * rows + lax.broadcasted_iota(I32, (rows, SB_BLOCK), 0) for p in parts]
    kidx = lax.broadcasted_iota(I32, (rows, SB_BLOCK), 1)

    def body(state):
        j, _ = state
        k0 = pl.multiple_of(j * SB_BLOCK, SB_BLOCK)
        kj = k_ref[pl.ds(k0, SB_BLOCK), :]
        vj = v_ref[pl.ds(k0, SB_BLOCK), :]
        z = [lax.dot_general(cut(q, p), kj, _NT, preferred_element_type=F32) for p in parts]
        strict = [(kidx + j * SB_BLOCK) < qpos[p] for p in parts]
        lk = [jnp.where(strict[p], jnp.minimum(-z[p], 0.0) - jnp.log1p(jnp.exp(-jnp.abs(z[p]))), 0.0)
              for p in parts]
        hi = [lk[p].astype(BF16) for p in parts]
        lo = [(lk[p] - hi[p].astype(F32)).astype(BF16) for p in parts]
        later = [dotf(hi[p], upper) + dotf(lo[p], upper) for p in parts]
        car = [car_ref[p] for p in parts]
        w = [jnp.where(strict[p], jnp.exp(z[p] + lk[p] + later[p] + car[p]), 0.0).astype(BF16) for p in parts]
        pv = [dotf(w[p], vj) for p in parts]
        top = None
        for p in parts:
            acc_ref[p] += pv[p]
            c = car[p] + jnp.sum(lk[p], axis=-1, keepdims=True)
            car_ref[p] = c
            top = jnp.max(c) if top is None else jnp.maximum(top, jnp.max(c))
        return j - 1, top

    def cond(state):
        j, top = state
        return jnp.logical_and(j >= 0, top > -SB_SKIP)

    lax.while_loop(cond, body, (i, jnp.float32(0.0)))
    for p in parts:
        o_ref[p * rows:(p + 1) * rows, :] = acc_ref[p].astype(o_ref.dtype)


def _sb_attention(z):
    t = z.shape[0]
    heads = BRANCH_W // SB_DIM
    upper = jnp.asarray(np.triu(np.ones((SB_BLOCK, SB_BLOCK), np.float32), 1).T, dtype=BF16)
    return pl.pallas_call(
        _sb_body,
        grid=(heads, t // SB_BLOCK),
        in_specs=[pl.BlockSpec((SB_BLOCK, SB_DIM), lambda h, i: (i, h)),
                  pl.BlockSpec((t, SB_DIM), lambda h, i: (0, heads + h)),
                  pl.BlockSpec((t, SB_DIM), lambda h, i: (0, 2 * heads + h)),
                  pl.BlockSpec((SB_BLOCK, SB_BLOCK), lambda h, i: (0, 0))],
        out_specs=pl.BlockSpec((SB_BLOCK, SB_DIM), lambda h, i: (i, h)),
        out_shape=jax.ShapeDtypeStruct((t, heads * SB_DIM), BF16),
        scratch_shapes=[pltpu.VMEM((SB_PARTS, SB_BLOCK // SB_PARTS, SB_DIM), F32),
                        pltpu.VMEM((SB_PARTS, SB_BLOCK // SB_PARTS, 1), F32)],
        compiler_params=_cparams(("arbitrary", "arbitrary")),
        name="stick_breaking",
    )(z, z, z, upper)


def _merge_body(b0_ref, b1_ref, b2_ref, b3_ref, gate_ref, wbr_ref, wout_ref, x_ref, lnw_ref, lnb_ref,
                xo_ref, xb_ref, xp_ref):
    merged = None
    for g, b_ref in enumerate((b0_ref, b1_ref, b2_ref, b3_ref)):
        y = jnp.dot(b_ref[...], wbr_ref[g], preferred_element_type=F32)
        y = y * gate_ref[:, g * D_MODEL:(g + 1) * D_MODEL].astype(F32)
        merged = y if merged is None else merged + y
    mix = jnp.dot(merged.astype(BF16), wout_ref[...], preferred_element_type=F32)
    y = _layer_norm(DEEPNORM_ALPHA * x_ref[...] + mix, lnw_ref[...], lnb_ref[...])
    xo_ref[...] = y
    xb_ref[...] = y.astype(BF16)
    _store_rows(xp_ref, _pack_pair(y[:, :D_MODEL // 2], y[:, D_MODEL // 2:]))


def _merge(branches, gates, w_br, w_out, x, ln_w, ln_b, tm=256):
    t = x.shape[0]
    tm = min(tm, t)
    row = lambda i: (i, 0)
    const2 = lambda i: (0, 0)
    return pl.pallas_call(
        _merge_body,
        grid=(t // tm,),
        in_specs=[pl.BlockSpec((tm, BRANCH_W), row)] * 4 + [
            pl.BlockSpec((tm, N_BRANCHES * D_MODEL), row),
            pl.BlockSpec((N_BRANCHES, BRANCH_W, D_MODEL), lambda i: (0, 0, 0)),
            pl.BlockSpec((D_MODEL, D_MODEL), const2),
            pl.BlockSpec((tm, D_MODEL), row),
            pl.BlockSpec((1, D_MODEL), const2),
            pl.BlockSpec((1, D_MODEL), const2)],
        out_specs=[pl.BlockSpec((tm, D_MODEL), row), pl.BlockSpec((tm, D_MODEL), row),
                   pl.BlockSpec((tm * ROW_TILES, LANES), row)],
        out_shape=[jax.ShapeDtypeStruct((t, D_MODEL), F32), jax.ShapeDtypeStruct((t, D_MODEL), BF16),
                   jax.ShapeDtypeStruct((t * ROW_TILES, LANES), U32)],
        compiler_params=_cparams(("parallel",), 56),
        name="merge_ln",
    )(*branches, gates, w_br, w_out, x, ln_w, ln_b)


def _first_index(hit, idx, size, axis):
    return jnp.min(jnp.where(hit, idx, size), axis=axis, keepdims=True)


def _router_body(x_ref, wr_ref, bias_ref, before_ref, idx_ref, w_ref, rank_ref, cnt_ref, run_ref, *, tm):
    @pl.when(pl.program_id(0) == 0)
    def _():
        run_ref[...] = jnp.zeros_like(run_ref)

    per = N_EXPERTS // N_GROUPS
    logits = lax.dot_general(wr_ref[...], x_ref[...], _NT, precision=HI, preferred_element_type=F32)
    scores = _sigmoid(logits)
    biased = scores + bias_ref[:, 0:1]
    g3 = biased.reshape(N_GROUPS, per, tm)
    pos = lax.broadcasted_iota(I32, (N_GROUPS, per, tm), 1)
    m1 = jnp.max(g3, axis=1, keepdims=True)
    f1 = _first_index(g3 == m1, pos, per, 1)
    m2 = jnp.max(jnp.where(pos == f1, -jnp.inf, g3), axis=1, keepdims=True)
    gscore = (m1 + m2).reshape(N_GROUPS, tm)

    gpos = lax.broadcasted_iota(I32, (N_GROUPS, tm), 0)
    chosen = jnp.zeros((N_GROUPS, tm), F32)
    cur = gscore
    for _ in range(TOPK_GROUPS):
        m = jnp.max(cur, axis=0, keepdims=True)
        pick = gpos == _first_index(cur == m, gpos, N_GROUPS, 0)
        chosen = jnp.where(pick, 1.0, chosen)
        cur = jnp.where(pick, -jnp.inf, cur)
    ok = jnp.broadcast_to(chosen.reshape(N_GROUPS, 1, tm), (N_GROUPS, per, tm)).reshape(N_EXPERTS, tm)

    epos = lax.broadcasted_iota(I32, (N_EXPERTS, tm), 0)
    cur = jnp.where(ok > 0.5, biased, -jnp.inf)
    picks, idx_rows, w_rows = [], [], []
    member = jnp.zeros((N_EXPERTS, tm), F32)
    for _ in range(TOP_K):
        m = jnp.max(cur, axis=0, keepdims=True)
        f = _first_index(cur == m, epos, N_EXPERTS, 0)
        pick = epos == f
        picks.append(pick)
        idx_rows.append(f)
        w_rows.append(jnp.sum(jnp.where(pick, scores, 0.0), axis=0, keepdims=True))
        member = jnp.where(pick, 1.0, member)
        cur = jnp.where(pick, -jnp.inf, cur)
    w_sel = jnp.concatenate(w_rows, axis=0)
    w_ref[...] = ROUTED_SCALE * w_sel / jnp.sum(w_sel, axis=0, keepdims=True)
    idx_ref[...] = jnp.concatenate(idx_rows, axis=0)

    seen = jnp.dot(member.astype(BF16), before_ref[...], preferred_element_type=F32) + run_ref[:, 0:1]
    rank_rows = [jnp.sum(jnp.where(pk, seen, 0.0), axis=0, keepdims=True) for pk in picks]
    rank_ref[...] = jnp.concatenate(rank_rows, axis=0).astype(I32)
    run_ref[...] = run_ref[...] + jnp.sum(member, axis=1, keepdims=True)
    cnt_ref[...] = run_ref[...]


def _router(x, w_router_t, bias, tm=512):
    t = x.shape[0]
    tm = min(tm, t)
    before = jnp.asarray(np.triu(np.ones((tm, tm), np.float32), 1), dtype=BF16)
    slot = lambda i: (0, i)
    const = lambda i: (0, 0)
    return pl.pallas_call(
        functools.partial(_router_body, tm=tm),
        grid=(t // tm,),
        in_specs=[pl.BlockSpec((tm, D_MODEL), lambda i: (i, 0)),
                  pl.BlockSpec((N_EXPERTS, D_MODEL), const),
                  pl.BlockSpec((N_EXPERTS, LANES), const),
                  pl.BlockSpec((tm, tm), const)],
        out_specs=[pl.BlockSpec((TOP_K, tm), slot), pl.BlockSpec((TOP_K, tm), slot),
                   pl.BlockSpec((TOP_K, tm), slot), pl.BlockSpec((N_EXPERTS, LANES), const)],
        out_shape=[jax.ShapeDtypeStruct((TOP_K, t), I32), jax.ShapeDtypeStruct((TOP_K, t), F32),
                   jax.ShapeDtypeStruct((TOP_K, t), I32), jax.ShapeDtypeStruct((N_EXPERTS, LANES), F32)],
        scratch_shapes=[pltpu.VMEM((N_EXPERTS, LANES), F32)],
        compiler_params=_cparams(("arbitrary",)),
        name="router",
    )(x, w_router_t, bias, before)


def _dest_body(start_ref, idx_ref, rank_ref, dest_ref):
    idx = idx_ref[...]
    base = jnp.zeros(idx.shape, I32)
    for e in range(N_EXPERTS):
        base = jnp.where(idx == e, start_ref[e], base)
    dest_ref[...] = base + rank_ref[...]


def _dest_rows(pad_start, idx, rank, tm=2048):
    t = idx.shape[1]
    tm = min(tm, t)
    slot = lambda i, s: (0, i)
    return pl.pallas_call(
        _dest_body,
        grid_spec=pltpu.PrefetchScalarGridSpec(
            num_scalar_prefetch=1, grid=(t // tm,),
            in_specs=[pl.BlockSpec((TOP_K, tm), slot), pl.BlockSpec((TOP_K, tm), slot)],
            out_specs=pl.BlockSpec((TOP_K, tm), slot)),
        out_shape=jax.ShapeDtypeStruct((TOP_K, t), I32),
        compiler_params=_cparams(("parallel",)),
        name="dest_rows",
    )(pad_start, idx, rank)


def _dispatch_body(zb_ref, dest_ref, x_ref, xs_hbm, zero_ref, sem, zsem, *, tm, n_fill):
    @pl.when(pl.program_id(0) == 0)
    def _():
        zero_ref[...] = jnp.zeros_like(zero_ref)

        def fill(i, carry):
            @pl.when(zb_ref[i] >= 0)
            def _():
                r0 = pl.multiple_of(zb_ref[i] * (ROW_BLOCK * ROW_TILES), ROW_BLOCK * ROW_TILES)
                pltpu.make_async_copy(zero_ref, xs_hbm.at[pl.ds(r0, ROW_BLOCK * ROW_TILES)], zsem).start()
            return carry

        def drain(i, carry):
            @pl.when(zb_ref[i] >= 0)
            def _():
                pltpu.make_async_copy(zero_ref, xs_hbm.at[pl.ds(0, ROW_BLOCK * ROW_TILES)], zsem).wait()
            return carry

        lax.fori_loop(0, n_fill, fill, 0)
        lax.fori_loop(0, n_fill, drain, 0)

    def row(t, carry):
        for k in range(TOP_K):
            pltpu.make_async_copy(_row(x_ref, t), _row(xs_hbm, dest_ref[k, t]), sem).start(
                priority=k % 2)
        return carry

    lax.fori_loop(0, tm, row, 0)
    for _ in range(TOP_K):
        pltpu.make_async_copy(x_ref, xs_hbm.at[pl.ds(0, tm * ROW_TILES)], sem).wait()


def _dispatch(zero_blocks, dest, x_packed, n_rows, tm=256):
    t = x_packed.shape[0] // ROW_TILES
    tm = min(tm, t)
    return pl.pallas_call(
        functools.partial(_dispatch_body, tm=tm, n_fill=zero_blocks.shape[0]),
        grid_spec=pltpu.PrefetchScalarGridSpec(
            num_scalar_prefetch=1, grid=(t // tm,),
            in_specs=[pl.BlockSpec((TOP_K, tm), lambda i, zb: (0, i), memory_space=pltpu.SMEM),
                      pl.BlockSpec((tm * ROW_TILES, LANES), lambda i, zb: (i, 0))],
            out_specs=pl.BlockSpec(memory_space=pl.ANY),
            scratch_shapes=[pltpu.VMEM((ROW_BLOCK * ROW_TILES, LANES), U32), pltpu.SemaphoreType.DMA(()),
                            pltpu.SemaphoreType.DMA(())]),
        out_shape=jax.ShapeDtypeStruct((n_rows * ROW_TILES, LANES), U32),
        compiler_params=_cparams(("arbitrary",)),
        name="dispatch",
    )(zero_blocks, dest, x_packed)


def _experts_body(be_ref, nu_ref, nxt_ref, slot_ref, xs_ref, wg_hbm, wu_hbm, wd_hbm, ys_ref,
                  wgf_ref, wuf_ref, wdf_ref, wgb_ref, wub_ref, wdb_ref, sem, *, layer):
    b = pl.program_id(0)
    used = b < nu_ref[0]
    new_expert = jnp.logical_or(b == 0, be_ref[b] != be_ref[jnp.maximum(b - 1, 0)])

    def weight_copies(e, s):
        return (pltpu.make_async_copy(wg_hbm.at[layer, e], wgf_ref.at[s], sem.at[s]),
                pltpu.make_async_copy(wu_hbm.at[layer, e], wuf_ref.at[s], sem.at[s]),
                pltpu.make_async_copy(wd_hbm.at[layer, e], wdf_ref.at[s], sem.at[s]))

    @pl.when(b == 0)
    def _():
        for c in weight_copies(be_ref[0], slot_ref[0]):
            c.start()

    @pl.when(jnp.logical_and(used, new_expert))
    def _():
        s = slot_ref[b]
        for c in weight_copies(be_ref[b], s):
            c.wait()

        @pl.when(nxt_ref[b] >= 0)
        def _():
            for c in weight_copies(nxt_ref[b], 1 - s):
                c.start()

        wgb_ref[...] = wgf_ref[s].astype(BF16)
        wub_ref[...] = wuf_ref[s].astype(BF16)
        wdb_ref[...] = wdf_ref[s].astype(BF16)

    @pl.when(used)
    def _():
        half = D_MODEL // 2
        lo, hi = _unpack_pair(_load_rows(xs_ref))
        lo = lo.astype(BF16)
        hi = hi.astype(BF16)
        gate = (jnp.dot(lo, wgb_ref[:half, :], preferred_element_type=F32)
                + jnp.dot(hi, wgb_ref[half:, :], preferred_element_type=F32))
        up = (jnp.dot(lo, wub_ref[:half, :], preferred_element_type=F32)
              + jnp.dot(hi, wub_ref[half:, :], preferred_element_type=F32))
        h = (_silu(gate) * up).astype(BF16)
        y = jnp.dot(h, wdb_ref[...], preferred_element_type=F32)
        _store_rows(ys_ref, _pack_pair(y[:, :half], y[:, half:]))

    @pl.when(jnp.logical_not(used))
    def _():
        ys_ref[...] = jnp.zeros_like(ys_ref)


def _experts(block_expert, n_used, next_expert, slot, xs, wg, wu, wd, layer):
    n_rows = xs.shape[0] // ROW_TILES
    n_blocks = n_rows // ROW_BLOCK
    blk = lambda b, be, nu, nx, sl: (jnp.minimum(b, nu[0] - 1), 0)
    out_blk = lambda b, be, nu, nx, sl: (b, 0)
    hbm = pl.BlockSpec(memory_space=pl.ANY)
    return pl.pallas_call(
        functools.partial(_experts_body, layer=layer),
        grid_spec=pltpu.PrefetchScalarGridSpec(
            num_scalar_prefetch=4, grid=(n_blocks,),
            in_specs=[pl.BlockSpec((ROW_BLOCK * ROW_TILES, LANES), blk), hbm, hbm, hbm],
            out_specs=pl.BlockSpec((ROW_BLOCK * ROW_TILES, LANES), out_blk),
            scratch_shapes=[pltpu.VMEM((2, D_MODEL, EXPERT_W), F32), pltpu.VMEM((2, D_MODEL, EXPERT_W), F32),
                            pltpu.VMEM((2, EXPERT_W, D_MODEL), F32),
                            pltpu.VMEM((D_MODEL, EXPERT_W), BF16), pltpu.VMEM((D_MODEL, EXPERT_W), BF16),
                            pltpu.VMEM((EXPERT_W, D_MODEL), BF16), pltpu.SemaphoreType.DMA((2,))]),
        out_shape=jax.ShapeDtypeStruct((n_rows * ROW_TILES, LANES), U32),
        compiler_params=_cparams(("arbitrary",), 56),
        name="experts",
    )(block_expert, n_used, next_expert, slot, xs, wg, wu, wd)


def _combine_body(dest_ref, dnext_ref, w_ref, x_ref, xb_ref, sg_ref, su_ref, sd_ref, lnw_ref, lnb_ref, ys_hbm,
                  xo_ref, xb_out_ref, buf_ref, lo_ref, hi_ref, sem, *, tm, n_steps):
    i = pl.program_id(0)
    cur = i % 2

    def request(d_ref, s):
        def row(t, carry):
            for k in range(TOP_K):
                pltpu.make_async_copy(_row(ys_hbm, d_ref[k, t]), _row(buf_ref.at[s, k], t), sem.at[s]).start(
                    priority=k % 2)
            return carry

        lax.fori_loop(0, tm, row, 0)

    @pl.when(i == 0)
    def _():
        request(dest_ref, 0)

    @pl.when(i + 1 < n_steps)
    def _():
        request(dnext_ref, 1 - cur)

    xb = xb_ref[...]
    h = _silu(jnp.dot(xb, sg_ref[...], preferred_element_type=F32)) * jnp.dot(xb, su_ref[...],
                                                                           preferred_element_type=F32)
    shared = jnp.dot(h.astype(BF16), sd_ref[...], preferred_element_type=F32)

    for k in range(TOP_K):
        pltpu.make_async_copy(ys_hbm.at[pl.ds(0, tm * ROW_TILES)], buf_ref.at[cur, k], sem.at[cur]).wait()

    lo_acc = hi_acc = None
    for k in range(TOP_K):
        lo, hi = _unpack_pair(buf_ref[cur, k])
        wk = w_ref[:, k:k + 1]
        lo_acc = wk * lo if lo_acc is None else lo_acc + wk * lo
        hi_acc = wk * hi if hi_acc is None else hi_acc + wk * hi
    lo_ref[...] = lo_acc
    hi_ref[...] = hi_acc
    routed = jnp.concatenate([_load_rows(lo_ref), _load_rows(hi_ref)], axis=1)
    y = _layer_norm(DEEPNORM_ALPHA * x_ref[...] + routed + shared, lnw_ref[...], lnb_ref[...])
    xo_ref[...] = y
    xb_out_ref[...] = y.astype(BF16)


def _combine(dest, w_sel, x, xb, sg, su, sd, ln_w, ln_b, ys, tm=256):
    t = x.shape[0]
    tm = min(tm, t)
    n_steps = t // tm
    row = lambda i: (i, 0)
    slot = lambda i: (0, i)
    slot_next = lambda i: (0, jnp.minimum(i + 1, n_steps - 1))
    const = lambda i: (0, 0)
    return pl.pallas_call(
        functools.partial(_combine_body, tm=tm, n_steps=n_steps),
        grid=(n_steps,),
        in_specs=[pl.BlockSpec((TOP_K, tm), slot, memory_space=pltpu.SMEM),
                  pl.BlockSpec((TOP_K, tm), slot_next, memory_space=pltpu.SMEM),
                  pl.BlockSpec((tm * ROW_TILES, TOP_K), row),
                  pl.BlockSpec((tm, D_MODEL), row),
                  pl.BlockSpec((tm, D_MODEL), row),
                  pl.BlockSpec((D_MODEL, EXPERT_W), const),
                  pl.BlockSpec((D_MODEL, EXPERT_W), const),
                  pl.BlockSpec((EXPERT_W, D_MODEL), const),
                  pl.BlockSpec((1, D_MODEL), const),
                  pl.BlockSpec((1, D_MODEL), const),
                  pl.BlockSpec(memory_space=pl.ANY)],
        out_specs=[pl.BlockSpec((tm, D_MODEL), row), pl.BlockSpec((tm, D_MODEL), row)],
        out_shape=[jax.ShapeDtypeStruct((t, D_MODEL), F32), jax.ShapeDtypeStruct((t, D_MODEL), BF16)],
        scratch_shapes=[pltpu.VMEM((2, TOP_K, tm * ROW_TILES, LANES), U32),
                        pltpu.VMEM((tm * ROW_TILES, LANES), F32), pltpu.VMEM((tm * ROW_TILES, LANES), F32),
                        pltpu.SemaphoreType.DMA((2,))],
        compiler_params=_cparams(("arbitrary",), 56),
        name="combine_ln",
    )(dest, dest, w_sel, x, xb, sg, su, sd, ln_w, ln_b, ys)


def _pad_cols(a, width):
    return jnp.pad(a, [(0, 0)] * (a.ndim - 1) + [(0, width - a.shape[-1])])


def _pad_rows(a, height):
    return jnp.pad(a, ((0, height - a.shape[0]), (0, 0)))


def _pad_heads(a, heads, dk):
    lead = a.shape[:-1]
    padded = jnp.pad(a.reshape(*lead, heads, dk), [(0, 0)] * (len(lead) + 1) + [(0, HEAD_PAD - dk)])
    return padded.reshape(*lead, heads * HEAD_PAD)


def _relayout_in_weight(w_in):
    w_rw = w_in[..., RW_OFF:SB_OFF]
    w_sb = w_in[..., SB_OFF:GLA_OFF]
    w_gla = w_in[..., GLA_OFF:GATE_OFF]
    zcol = lambda n: jnp.zeros(w_in.shape[:-1] + (n,), w_in.dtype)
    return jnp.concatenate([
        w_in[..., HG_OFF:RW_OFF],
        w_rw[..., :1536], _pad_cols(w_rw[..., 1536:1568], LANES), _pad_cols(w_rw[..., 1568:1600], LANES),
        _pad_cols(w_rw[..., 1600:1696], LANES), zcol(RW_Z - 3 * BRANCH_W - 3 * LANES),
        w_sb[..., :BRANCH_W] * (SB_DIM ** -0.5), w_sb[..., BRANCH_W:],
        _pad_heads(w_gla[..., :256], 4, GLA_DK), _pad_heads(w_gla[..., 256:512], 4, GLA_DK),
        w_gla[..., 512:1024], w_gla[..., 1040:1552], _pad_cols(w_gla[..., 1024:1040], LANES),
        zcol(GLA_Z - 4 * BRANCH_W - LANES),
        w_in[..., GATE_OFF:]], axis=-1).astype(BF16)


def _token_mixing(xb, w_all, layer, lower_bound, hg_norm_w, rw_mu, rw_w0, rw_w2, rw_a0, rw_a2, rw_g2, rw_kk,
                  rw_ka, rw_rk, rw_ln_w, rw_ln_b, gla_g2, gla_gb, gla_norm_w):
    row = lambda a: a.reshape(1, -1).astype(F32)

    z_hg = _matmul(xb, w_all, layer, ZHG_OFF, 4 * BRANCH_W, F32, tn=2 * PROJ_TN)
    o_hg = _gated_mixer(z_hg, row(lower_bound), row(hg_norm_w), jnp.zeros((8, BRANCH_W), F32), "hgrn2")

    z_rw = _matmul(xb, w_all, layer, ZRW_OFF, RW_Z, F32, tn=2 * PROJ_TN)
    mu = jnp.concatenate([rw_mu[:1536], jnp.pad(rw_mu[1536:1568], (0, 96)), jnp.pad(rw_mu[1568:1600], (0, 96)),
                          jnp.pad(rw_mu[1600:1696], (0, 32 + RW_Z - 3 * BRANCH_W - 3 * LANES))]).reshape(1, RW_Z)
    vecs = jnp.stack([rw_w0, rw_a0, rw_kk, rw_ka, rw_rk, rw_ln_w, rw_ln_b, jnp.zeros_like(rw_w0)]).astype(F32)
    o_rw = _rwkv_mixer(z_rw, mu, vecs, _pad_rows(rw_w2, LANES), _pad_rows(rw_a2, LANES), _pad_rows(rw_g2, LANES))

    o_sb = _sb_attention(_matmul(xb, w_all, layer, ZSB_OFF, 3 * BRANCH_W, BF16))

    z_gla = _matmul(xb, w_all, layer, ZGLA_OFF, GLA_Z, F32)
    g2p = _pad_rows(_pad_heads(gla_g2, 4, GLA_DK), LANES)
    gbp = _pad_heads(gla_gb.reshape(1, -1), 4, GLA_DK)
    o_gla = _gated_mixer(z_gla, gbp, row(gla_norm_w), g2p, "gla")

    gates = _matmul(xb, w_all, layer, ZGATE_OFF, N_BRANCHES * D_MODEL, BF16, act="sigmoid", tn=1024)
    return (o_hg, o_rw, o_sb, o_gla), gates


def _moe(x, xb, xp, w_router, router_bias, we_gate, we_up, we_down, layer, ws_gate, ws_up, ws_down, ln_w, ln_b):
    t = x.shape[0]
    bias = jnp.broadcast_to(router_bias.astype(F32).reshape(N_EXPERTS, 1), (N_EXPERTS, LANES))
    idx, w_sel, rank, counts = _router(x, w_router.T.astype(F32), bias)

    cnt = counts[:, 0].astype(I32)
    padded = (cnt + ROW_BLOCK - 1) // ROW_BLOCK * ROW_BLOCK
    pad_end = jnp.cumsum(padded)
    pad_start = (pad_end - padded).astype(I32)
    n_blocks = t * TOP_K // ROW_BLOCK + N_EXPERTS
    n_used = (pad_end[-1:] // ROW_BLOCK).astype(I32)
    first_row = jnp.arange(n_blocks, dtype=I32) * ROW_BLOCK
    block_expert = jnp.minimum(jnp.sum(pad_end[None, :] <= first_row[:, None], axis=1), N_EXPERTS - 1).astype(I32)

    last_block = jnp.where(padded > 0, pad_end // ROW_BLOCK - 1, -1)
    tail = n_used[0] + jnp.arange(N_EXPERTS, dtype=I32)
    zero_blocks = jnp.concatenate([last_block, jnp.where(tail < n_blocks, tail, -1)]).astype(I32)

    has_rows = cnt > 0
    eid = jnp.arange(N_EXPERTS, dtype=I32)
    later = jnp.where(has_rows[None, :] & (eid[None, :] > eid[:, None]), eid[None, :], N_EXPERTS)
    next_used = jnp.min(later, axis=1)
    next_used = jnp.where(next_used < N_EXPERTS, next_used, -1).astype(I32)
    ordinal = jnp.cumsum(has_rows.astype(I32)) - 1
    of_block = block_expert[:, None] == eid[None, :]
    next_expert = jnp.sum(jnp.where(of_block, next_used[None, :], 0), axis=1).astype(I32)
    slot = (jnp.sum(jnp.where(of_block, ordinal[None, :], 0), axis=1) % 2).astype(I32)

    dest = _dest_rows(pad_start, idx, rank)
    xs = _dispatch(zero_blocks, dest, xp, n_blocks * ROW_BLOCK)
    ys = _experts(block_expert, n_used, next_expert, slot, xs, we_gate, we_up, we_down, layer)
    w_rows = jnp.repeat(w_sel.T, ROW_TILES, axis=0)
    return _combine(dest, w_rows, x, xb, ws_gate.astype(BF16), ws_up.astype(BF16), ws_down.astype(BF16),
                    ln_w.reshape(1, -1), ln_b.reshape(1, -1), ys)


def kernel(x, w_in, hg_lb_logits, hg_norm_w, rw_mu, rw_w0, rw_w2, rw_a0, rw_a2, rw_g2, rw_kk, rw_ka, rw_rk,
           rw_ln_w, rw_ln_b, gla_g2, gla_gb, gla_norm_w, w_br, w_out, ln1_w, ln1_b, w_router, router_bias,
           we_gate, we_up, we_down, ws_gate, ws_up, ws_down, ln2_w, ln2_b):
    bsz, t, d = x.shape
    cum = jnp.cumsum(jax.nn.softmax(hg_lb_logits.astype(F32), axis=0), axis=0)
    lower_bounds = cum - cum[0:1]
    w_all = _relayout_in_weight(w_in)
    outs = []
    for bi in range(bsz):
        xf = x[bi].astype(F32)
        xb = xf.astype(BF16)
        for l in range(DEPTH):
            branches, gates = _token_mixing(
                xb, w_all, l, lower_bounds[l], hg_norm_w[l], rw_mu[l], rw_w0[l], rw_w2[l], rw_a0[l], rw_a2[l],
                rw_g2[l], rw_kk[l], rw_ka[l], rw_rk[l], rw_ln_w[l], rw_ln_b[l], gla_g2[l], gla_gb[l],
                gla_norm_w[l])
            xf, xb, xp = _merge(branches, gates, w_br[l].astype(BF16), w_out[l].astype(BF16), xf,
                                ln1_w[l].reshape(1, -1), ln1_b[l].reshape(1, -1))
            xf, xb = _moe(xf, xb, xp, w_router[l], router_bias[l], we_gate, we_up, we_down, l,
                          ws_gate[l], ws_up[l], ws_down[l], ln2_w[l], ln2_b[l])
        outs.append(xf)
    return jnp.stack(outs).astype(x.dtype)
```

```python
import functools

import jax
import jax.numpy as jnp
import numpy as np
from jax import lax
from jax.experimental import pallas as pl
from jax.experimental.pallas import tpu as pltpu

F32 = jnp.float32
BF16 = jnp.bfloat16
I32 = jnp.int32
U32 = jnp.uint32
HI = lax.Precision.HIGHEST

D_MODEL = 2048
DEPTH = 2
BRANCH_W = 512
N_BRANCHES = 4
CHUNK = 64
SUB = 16
N_SUB = CHUNK // SUB
LANES = 128
HEAD_PAD = 128
RW_HEADS = 8
RW_DIM = 64
RW_PAIRS = RW_HEADS // 2
GATED_SPAN = 2
RW_SPAN = 2
GLA_DK = 64
GLA_TAU = 16.0
SB_DIM = 128
SB_BLOCK = 256
SB_PARTS = 2
SB_SKIP = 120.0
N_EXPERTS = 64
TOP_K = 8
N_GROUPS = 8
TOPK_GROUPS = 4
EXPERT_W = 512
ROUTED_SCALE = 2.5
ROW_BLOCK = 256
EXPERT_STEP_BLOCKS = 2
DEEPNORM_ALPHA = (2 * DEPTH) ** 0.25
LN_EPS = 1e-5
RW_GN_EPS = 64e-5
LOG2E = 1.4426950408889634

HG_OFF = 0
RW_OFF = 2048
SB_OFF = 3744
GLA_OFF = 5280
GATE_OFF = 6832
PROJ_TN = 512
RW_Z = 2048
GLA_Z = 2560
ZHG_OFF = 0
ZRW_OFF = ZHG_OFF + 4 * BRANCH_W
ZSB_OFF = ZRW_OFF + RW_Z
ZGLA_OFF = ZSB_OFF + 3 * BRANCH_W
ZGATE_OFF = ZGLA_OFF + GLA_Z
Z_TOTAL = ZGATE_OFF + N_BRANCHES * D_MODEL

_NT = (((1,), (1,)), ((), ()))
_TN = (((0,), (0,)), ((), ()))


def _cparams(sem, vmem_mb=48):
    return pltpu.CompilerParams(dimension_semantics=sem, vmem_limit_bytes=vmem_mb << 20)


def _sigmoid(x):
    return 1.0 / (1.0 + jnp.exp(-x))


def _log_sigmoid(x):
    return jnp.minimum(x, 0.0) - jnp.log1p(jnp.exp(-jnp.abs(x)))


def _silu(x):
    return x * _sigmoid(x)


def _layer_norm(y, w, b):
    mu = jnp.mean(y, axis=-1, keepdims=True)
    d = y - mu
    var = jnp.mean(d * d, axis=-1, keepdims=True)
    return d * lax.rsqrt(var + LN_EPS) * w + b


def _pack_pair(lo, hi):
    lo_b = lax.bitcast_convert_type(lo.astype(BF16).astype(F32), U32) >> 16
    hi_b = lax.bitcast_convert_type(hi.astype(BF16).astype(F32), U32) & jnp.uint32(0xFFFF0000)
    return lo_b | hi_b


def _unpack_pair(u):
    lo = lax.bitcast_convert_type(u << 16, F32)
    hi = lax.bitcast_convert_type(u & jnp.uint32(0xFFFF0000), F32)
    return lo, hi


ROW_TILES = D_MODEL // 2 // LANES


def _store_rows(ref, packed):
    r = packed.shape[0]
    for j in range(ROW_TILES):
        ref[pl.ds(j, r, stride=ROW_TILES), :] = packed[:, j * LANES:(j + 1) * LANES]


def _load_rows(ref):
    r = ref.shape[0] // ROW_TILES
    return jnp.concatenate([ref[pl.ds(j, r, stride=ROW_TILES), :] for j in range(ROW_TILES)], axis=1)


def _row(ref, i):
    return ref.at[pl.ds(pl.multiple_of(i * ROW_TILES, ROW_TILES), ROW_TILES)]


def _split(x):
    hi = x.astype(BF16)
    return hi, (x - hi.astype(F32)).astype(BF16)


def _dot2(x, ones_bf16):
    hi, lo = _split(x)
    return (jnp.dot(hi, ones_bf16, preferred_element_type=F32)
            + jnp.dot(lo, ones_bf16, preferred_element_type=F32))


def _ldot2(ones_bf16, x):
    hi, lo = _split(x)
    return (jnp.dot(ones_bf16, hi, preferred_element_type=F32)
            + jnp.dot(ones_bf16, lo, preferred_element_type=F32))


def _dot3(a, b):
    a_hi, a_lo = _split(a)
    b_hi, b_lo = _split(b)
    return (jnp.dot(a_hi, b_hi, preferred_element_type=F32) + jnp.dot(a_hi, b_lo, preferred_element_type=F32)
            + jnp.dot(a_lo, b_hi, preferred_element_type=F32))


def _mm_body(x_ref, w_ref, o_ref, *, act):
    acc = jnp.dot(x_ref[...], w_ref[...], preferred_element_type=F32)
    if act == "sigmoid":
        acc = _sigmoid(acc)
    o_ref[...] = acc.astype(o_ref.dtype)


def _matmul(x, w, layer, col0, n, out_dtype, act=None, tm=1024, tn=PROJ_TN):
    m, k = x.shape
    tm = min(tm, m)
    off = col0 // tn
    return pl.pallas_call(
        functools.partial(_mm_body, act=act),
        grid=(m // tm, n // tn),
        in_specs=[pl.BlockSpec((tm, k), lambda i, j: (i, 0)),
                  pl.BlockSpec((None, k, tn), lambda i, j: (layer, 0, off + j))],
        out_specs=pl.BlockSpec((tm, tn), lambda i, j: (i, j)),
        out_shape=jax.ShapeDtypeStruct((m, n), out_dtype),
        compiler_params=_cparams(("parallel", "parallel")),
        name="proj",
    )(x, w)


def _gated_span(q, k, v, g, st, tril):
    hs = range(len(q))
    lanes = [(ci, h) for ci in range(GATED_SPAN) for h in hs]
    ls = range(len(lanes))
    subs = range(N_SUB)
    dotf = functools.partial(jnp.dot, preferred_element_type=F32)
    blk = lambda x, i: x[i * SUB:(i + 1) * SUB]
    cut = lambda xs: [xs[h][ci * CHUNK:(ci + 1) * CHUNK] for ci, h in lanes]
    b_f = [_ldot2(tril, g[h] * LOG2E) for h in hs]
    qe_f = [(q[h] * jnp.exp2(b_f[h])).astype(BF16) for h in hs]
    vb_f = [v[h].astype(BF16) for h in hs]
    b, qc, kc, vb, qe = cut(b_f), cut(q), cut(k), cut(vb_f), cut(qe_f)
    blast = [b[i][CHUNK - 1:CHUNK, :] for i in ls]
    ends = [[b[i][(j + 1) * SUB - 1:(j + 1) * SUB, :] for j in subs] for i in ls]
    khat = [[blk(kc[i], j) * jnp.exp2(ends[i][j] - blk(b[i], j)) for j in subs] for i in ls]
    o_parts = [[None] * N_SUB for _ in ls]

    for j in range(N_SUB - 1):
        lo = (j + 1) * SUB
        qs = [(qc[i][lo:] * jnp.exp2(b[i][lo:] - ends[i][j])).astype(BF16) for i in ls]
        a = [lax.dot_general(qs[i], khat[i][j].astype(BF16), _NT, preferred_element_type=F32) for i in ls]
        pv = [dotf(a[i].astype(BF16), blk(vb[i], j)) for i in ls]
        for i in ls:
            for t in range(j + 1, N_SUB):
                piece = pv[i][(t - j - 1) * SUB:(t - j) * SUB]
                o_parts[i][t] = piece if o_parts[i][t] is None else o_parts[i][t] + piece

    lane = lax.broadcasted_iota(I32, (SUB, LANES), 1)
    trow = lax.broadcasted_iota(I32, (SUB, 1), 0)
    half = SUB // 2
    for t in subs:
        d_top = [jnp.zeros((half, LANES), F32) for _ in ls]
        d_bot = [jnp.zeros((half, LANES), F32) for _ in ls]
        for s in range(SUB):
            for i in ls:
                bi, qi, ki = blk(b[i], t), blk(qc[i], t), blk(kc[i], t)
                if s < half:
                    col = jnp.sum(qi * jnp.exp2(bi - bi[s:s + 1, :]) * ki[s:s + 1, :], axis=-1, keepdims=True)
                    d_top[i] = jnp.where(lane[:half] == s, col[:half], d_top[i])
                    d_bot[i] = jnp.where(lane[:half] == s, col[half:], d_bot[i])
                else:
                    col = jnp.sum(qi[half:] * jnp.exp2(bi[half:] - bi[s:s + 1, :]) * ki[s:s + 1, :],
                                  axis=-1, keepdims=True)
                    d_bot[i] = jnp.where(lane[:half] == s, col, d_bot[i])
        d = [jnp.concatenate([d_top[i], d_bot[i]], axis=0) for i in ls]
        pv = [dotf(jnp.where(lane <= trow, d[i], 0.0)[:, :SUB].astype(BF16), blk(vb[i], t)) for i in ls]
        for i in ls:
            o_parts[i][t] = pv[i] if o_parts[i][t] is None else o_parts[i][t] + pv[i]

    o_intra = [jnp.concatenate(o_parts[i], axis=0) for i in ls]
    kd = [jnp.concatenate([khat[i][j] * jnp.exp2(blast[i] - ends[i][j]) for j in subs], axis=0).astype(BF16)
          for i in ls]
    kv = [lax.dot_general(vb[i], kd[i], _TN, preferred_element_type=F32) for i in ls]
    dec = [jnp.exp2(blast[i]) for i in ls]

    state = list(st)
    o_ch = []
    for ci in range(GATED_SPAN):
        idx = [ci * len(q) + h for h in hs]
        o_ch.append([lax.dot_general(qe[idx[h]], state[h].astype(BF16), _NT, preferred_element_type=F32)
                     + o_intra[idx[h]] for h in hs])
        state = [state[h] * dec[idx[h]] + kv[idx[h]] for h in hs]
    o = [jnp.concatenate([o_ch[ci][h] for ci in range(GATED_SPAN)], axis=0) for h in hs]
    return o, state


def _gated_body(z_ref, aux_ref, nw_ref, g2_ref, tril_ref, o_ref, st_ref, *, mode, rows):
    @pl.when(pl.program_id(0) == 0)
    def _():
        st_ref[...] = jnp.zeros_like(st_ref)

    tril = tril_ref[...]
    span = GATED_SPAN * CHUNK

    def step(c, carry):
        r0 = pl.multiple_of(c * span, span)
        rs = pl.ds(r0, span)
        hs = range(4)
        sls = [slice(h * HEAD_PAD, (h + 1) * HEAD_PAD) for h in hs]
        zq = [z_ref[rs, h * HEAD_PAD:(h + 1) * HEAD_PAD] for h in hs]
        zk = [z_ref[rs, BRANCH_W + h * HEAD_PAD:BRANCH_W + (h + 1) * HEAD_PAD] for h in hs]
        v = [z_ref[rs, 2 * BRANCH_W + h * HEAD_PAD:2 * BRANCH_W + (h + 1) * HEAD_PAD] for h in hs]
        if mode == "hgrn2":
            q = [_silu(zq[h]) for h in hs]
            k = [(1.0 - aux_ref[0:1, sls[h]]) * _sigmoid(-zk[h]) for h in hs]
            g = [jnp.log1p(-k[h]) for h in hs]
        else:
            q = [zq[h] * (GLA_DK ** -0.5) for h in hs]
            k = zk
            la = _dot3(z_ref[rs, 4 * BRANCH_W:4 * BRANCH_W + LANES], g2_ref[...]) + aux_ref[...]
            g = [_log_sigmoid(la[:, sls[h]]) * (1.0 / GLA_TAU) for h in hs]
        o, st_new = _gated_span(q, k, v, g, [st_ref[h] for h in hs], tril)
        for h in hs:
            st_ref[h] = st_new[h]
            gate = z_ref[rs, 3 * BRANCH_W + h * HEAD_PAD:3 * BRANCH_W + (h + 1) * HEAD_PAD]
            on = o[h] * lax.rsqrt(jnp.mean(o[h] * o[h], axis=-1, keepdims=True) + LN_EPS)
            o_ref[rs, sls[h]] = (on * nw_ref[0:1, sls[h]] * _silu(gate)).astype(o_ref.dtype)
        return carry

    lax.fori_loop(0, rows // span, step, 0)


def _gated_mixer(z, aux, norm_w, g2, mode, rows=256):
    t, wz = z.shape
    rows = min(rows, t)
    span = GATED_SPAN * CHUNK
    pos = np.arange(span)
    tril = jnp.asarray(((pos[:, None] >= pos[None, :]) & (pos[:, None] // CHUNK == pos[None, :] // CHUNK))
                       .astype(np.float32), dtype=BF16)
    return pl.pallas_call(
        functools.partial(_gated_body, mode=mode, rows=rows),
        grid=(t // rows,),
        in_specs=[pl.BlockSpec((rows, wz), lambda i: (i, 0)),
                  pl.BlockSpec((1, BRANCH_W), lambda i: (0, 0)),
                  pl.BlockSpec((1, BRANCH_W), lambda i: (0, 0)),
                  pl.BlockSpec(g2.shape, lambda i: (0, 0)),
                  pl.BlockSpec((span, span), lambda i: (0, 0))],
        out_specs=pl.BlockSpec((rows, BRANCH_W), lambda i: (i, 0)),
        out_shape=jax.ShapeDtypeStruct((t, BRANCH_W), BF16),
        scratch_shapes=[pltpu.VMEM((4, HEAD_PAD, HEAD_PAD), F32)],
        compiler_params=_cparams(("arbitrary",)),
        name="gated_" + mode,
    )(z, aux, norm_w, g2, tril)


def _rwkv_body(z_ref, mu_ref, vec_ref, w2_ref, a2_ref, g2_ref, tril_ref, ones2_ref, o_ref,
               prev_ref, st_ref, *, rows):
    @pl.when(pl.program_id(0) == 0)
    def _():
        prev_ref[...] = jnp.zeros_like(prev_ref)
        st_ref[...] = jnp.zeros_like(st_ref)

    tril = tril_ref[...]
    ones2 = ones2_ref[...]
    row128 = lax.broadcasted_iota(I32, (2 * CHUNK, 2 * CHUNK), 0)
    col128 = lax.broadcasted_iota(I32, (2 * CHUNK, 2 * CHUNK), 1)
    rt = jnp.where(row128 >= CHUNK, row128 - CHUNK, row128)
    ct = jnp.where(col128 >= CHUNK, col128 - CHUNK, col128)
    strict = rt > ct
    incl = rt >= ct
    eye = row128 == col128
    head0 = lax.broadcasted_iota(I32, (1, LANES), 1) < RW_DIM
    span = RW_SPAN * CHUNK
    first_row = lax.broadcasted_iota(I32, (span, 1), 0) == 0

    def stack(x):
        return jnp.concatenate([jnp.where(head0, x, 0.0), jnp.where(head0, 0.0, x)], axis=0)

    def step(c, carry):
        r0 = pl.multiple_of(c * span, span)
        rs = pl.ds(r0, span)
        z = z_ref[rs, :]
        zprev = jnp.where(first_row, prev_ref[...], pltpu.roll(z, 1, axis=0))
        prev_ref[...] = z[span - 1:span, :]
        zs = z + (zprev - z) * mu_ref[...]
        lw = zs[:, 3 * BRANCH_W:3 * BRANCH_W + LANES]
        la = zs[:, 3 * BRANCH_W + LANES:3 * BRANCH_W + 2 * LANES]
        lg = zs[:, 3 * BRANCH_W + 2 * LANES:3 * BRANCH_W + 3 * LANES]
        wl = -(vec_ref[0:1, :] + _dot3(jnp.tanh(lw), w2_ref[...]))
        w_raw = -(jnp.maximum(wl, 0.0) + jnp.log1p(jnp.exp(-jnp.abs(wl)))) - 0.5
        logw_all = -jnp.exp(w_raw)
        iclr_all = _sigmoid(vec_ref[1:2, :] + _dot3(la, a2_ref[...]))
        gate_all = _dot3(_sigmoid(lg), g2_ref[...])
        cum_all = _ldot2(tril, logw_all)

        pairs = range(RW_PAIRS)
        sls = [slice(p * LANES, (p + 1) * LANES) for p in pairs]
        dotf = functools.partial(jnp.dot, preferred_element_type=F32)
        c2 = 2 * CHUNK
        r_f = [zs[:, p * LANES:(p + 1) * LANES] for p in pairs]
        k_f = [zs[:, BRANCH_W + p * LANES:BRANCH_W + (p + 1) * LANES] for p in pairs]
        v_f = [zs[:, 2 * BRANCH_W + p * LANES:2 * BRANCH_W + (p + 1) * LANES] for p in pairs]
        cum_f = [cum_all[:, s] for s in sls]
        iclr_f = [iclr_all[:, s] for s in sls]
        kkr = [k_f[p] * vec_ref[2:3, sls[p]] for p in pairs]
        ssq = [_dot2(kkr[p] * kkr[p], ones2) for p in pairs]
        kk_f = [kkr[p] / jnp.maximum(jnp.sqrt(ssq[p]), 1e-12) for p in pairs]
        kmod_f = [k_f[p] * (1.0 + (iclr_f[p] - 1.0) * vec_ref[3:4, sls[p]]) for p in pairs]
        bb_f = [kk_f[p] * iclr_f[p] for p in pairs]
        eneg_f = [jnp.exp(-cum_f[p]) for p in pairs]
        a_f = [-kk_f[p] * jnp.exp(cum_f[p] - logw_all[:, sls[p]]) for p in pairs]
        rd_f = [r_f[p] * jnp.exp(cum_f[p]) for p in pairs]
        bn_f = [bb_f[p] * eneg_f[p] for p in pairs]
        kn_f = [kmod_f[p] * eneg_f[p] for p in pairs]

        lanes = [(ci, p) for ci in range(RW_SPAN) for p in pairs]
        ls = range(len(lanes))
        cut = lambda xs: [xs[p][ci * CHUNK:(ci + 1) * CHUNK] for ci, p in lanes]
        cum, bb, kmod = cut(cum_f), cut(bb_f), cut(kmod_f)
        clast = [cum[i][CHUNK - 1:CHUNK, :] for i in ls]
        e_end = [jnp.exp(clast[i] - cum[i]) for i in ls]
        a_s = [stack(x) for x in cut(a_f)]
        r_s = [stack(x) for x in cut(rd_f)]
        b_s = [stack(x) for x in cut(bn_f)]
        k_s = [stack(x) for x in cut(kn_f)]
        bh_s = [stack(bb[i] * e_end[i]).astype(BF16) for i in ls]
        kh_s = [stack(kmod[i] * e_end[i]).astype(BF16) for i in ls]
        v_sb = [stack(x).astype(BF16) for x in cut(v_f)]

        left = [jnp.concatenate([a_s[i], r_s[i]], axis=0).astype(BF16) for i in ls]
        right = [jnp.concatenate([b_s[i], k_s[i]], axis=0).astype(BF16) for i in ls]
        gram = [lax.dot_general(left[i], right[i], _NT, preferred_element_type=F32) for i in ls]
        n_ab = [jnp.where(strict, gram[i][:c2, :c2], 0.0) for i in ls]
        a_ak = [jnp.where(strict, gram[i][:c2, c2:], 0.0).astype(BF16) for i in ls]
        g_bk = [jnp.concatenate([jnp.where(incl, gram[i][c2:, :c2], 0.0),
                                 jnp.where(incl, gram[i][c2:, c2:], 0.0)], axis=1).astype(BF16) for i in ls]

        x = [jnp.where(eye, 1.0, 0.0) + n_ab[i] for i in ls]
        pw = n_ab
        for _ in range(5):
            pwb = [pw[i].astype(BF16) for i in ls]
            pw = [dotf(pwb[i], pwb[i]) for i in ls]
            x = [x[i] + dotf(pw[i].astype(BF16), x[i].astype(BF16)) for i in ls]

        av = [dotf(a_ak[i], v_sb[i]) for i in ls]
        pqb = [dotf(x[i].astype(BF16), jnp.concatenate([a_s[i], av[i]], axis=1).astype(BF16)).astype(BF16)
               for i in ls]
        zero_blk = jnp.zeros((c2, LANES), BF16)
        top = [dotf(g_bk[i], jnp.concatenate([pqb[i], jnp.concatenate([zero_blk, v_sb[i]], axis=1)], axis=0))
               for i in ls]
        bot = [lax.dot_general(bh_s[i], pqb[i], _TN, preferred_element_type=F32) for i in ls]
        kv = [lax.dot_general(kh_s[i], v_sb[i], _TN, preferred_element_type=F32) for i in ls]
        r2 = [(r_s[i] + top[i][:, :LANES]).astype(BF16) for i in ls]
        tm = [(bot[i][:, :LANES] + jnp.where(eye, jnp.exp(clast[i]), 0.0)).astype(BF16) for i in ls]

        state = [st_ref[p] for p in pairs]
        o_ch = []
        for ci in range(RW_SPAN):
            idx = [ci * RW_PAIRS + p for p in pairs]
            s_old = [state[p].astype(BF16) for p in pairs]
            o_st = [dotf(r2[idx[p]], s_old[p]) + top[idx[p]][:, LANES:] for p in pairs]
            state = [dotf(tm[idx[p]], s_old[p]) + bot[idx[p]][:, LANES:] + kv[idx[p]] for p in pairs]
            o_ch.append([o_st[p][:CHUNK] + o_st[p][CHUNK:] for p in pairs])
        for p in pairs:
            st_ref[p] = state[p]
        o = [jnp.concatenate([o_ch[ci][p] for ci in range(RW_SPAN)], axis=0) for p in pairs]

        inv_n = 1.0 / RW_DIM
        mean = [_dot2(o[p], ones2) * inv_n for p in pairs]
        dlt = [o[p] - mean[p] for p in pairs]
        var = [_dot2(dlt[p] * dlt[p], ones2) * inv_n for p in pairs]
        bonus = [_dot2(r_f[p] * kmod_f[p] * vec_ref[4:5, sls[p]], ones2) * v_f[p] for p in pairs]
        for p in pairs:
            on = dlt[p] * lax.rsqrt(var[p] + RW_GN_EPS) * vec_ref[5:6, sls[p]] + vec_ref[6:7, sls[p]]
            o_ref[rs, sls[p]] = ((on + bonus[p]) * gate_all[:, sls[p]]).astype(o_ref.dtype)
        return carry

    lax.fori_loop(0, rows // span, step, 0)


def _rwkv_mixer(z, mu, vecs, w2, a2, g2, rows=256):
    t = z.shape[0]
    rows = min(rows, t)
    span = RW_SPAN * CHUNK
    pos = np.arange(span)
    tril = jnp.asarray(((pos[:, None] >= pos[None, :]) & (pos[:, None] // CHUNK == pos[None, :] // CHUNK))
                       .astype(np.float32), dtype=BF16)
    hid = np.arange(LANES) // RW_DIM
    ones2 = jnp.asarray((hid[:, None] == hid[None, :]).astype(np.float32), dtype=BF16)
    const = lambda i: (0, 0)
    return pl.pallas_call(
        functools.partial(_rwkv_body, rows=rows),
        grid=(t // rows,),
        in_specs=[pl.BlockSpec((rows, RW_Z), lambda i: (i, 0)),
                  pl.BlockSpec((1, RW_Z), const),
                  pl.BlockSpec((8, BRANCH_W), const),
                  pl.BlockSpec((LANES, BRANCH_W), const),
                  pl.BlockSpec((LANES, BRANCH_W), const),
                  pl.BlockSpec((LANES, BRANCH_W), const),
                  pl.BlockSpec((span, span), const),
                  pl.BlockSpec((LANES, LANES), const)],
        out_specs=pl.BlockSpec((rows, BRANCH_W), lambda i: (i, 0)),
        out_shape=jax.ShapeDtypeStruct((t, BRANCH_W), BF16),
        scratch_shapes=[pltpu.VMEM((1, RW_Z), F32), pltpu.VMEM((RW_PAIRS, LANES, LANES), F32)],
        compiler_params=_cparams(("arbitrary",)),
        name="rwkv7",
    )(z, mu, vecs, w2, a2, g2, tril, ones2)


def _sb_body(q_ref, k_ref, v_ref, upper_ref, o_ref, acc_ref, car_ref):
    i = pl.program_id(1)
    q = q_ref[...]
    upper = upper_ref[...]
    acc_ref[...] = jnp.zeros_like(acc_ref)
    car_ref[...] = jnp.zeros_like(car_ref)
    parts = range(SB_PARTS)
    rows = SB_BLOCK // SB_PARTS
    cut = lambda a, p: a[p * rows:(p + 1) * rows]
    dotf = functools.partial(jnp.dot, preferred_element_type=F32)
    qpos = [i * SB_BLOCK + p * rows + lax.broadcasted_iota(I32, (rows, SB_BLOCK), 0) for p in parts]
    kidx = lax.broadcasted_iota(I32, (rows, SB_BLOCK), 1)

    def body(state):
        j, _ = state
        k0 = pl.multiple_of(j * SB_BLOCK, SB_BLOCK)
        kj = k_ref[pl.ds(k0, SB_BLOCK), :]
        vj = v_ref[pl.ds(k0, SB_BLOCK), :]
        z = [lax.dot_general(cut(q, p), kj, _NT, preferred_element_type=F32) for p in parts]
        strict = [(kidx + j * SB_BLOCK) < qpos[p] for p in parts]
        lk = [jnp.where(strict[p], jnp.minimum(-z[p], 0.0) - jnp.log1p(jnp.exp(-jnp.abs(z[p]))), 0.0)
              for p in parts]
        hi = [lk[p].astype(BF16) for p in parts]
        lo = [(lk[p] - hi[p].astype(F32)).astype(BF16) for p in parts]
        later = [dotf(hi[p], upper) + dotf(lo[p], upper) for p in parts]
        car = [car_ref[p] for p in parts]
        w = [jnp.where(strict[p], jnp.exp(z[p] + lk[p] + later[p] + car[p]), 0.0).astype(BF16) for p in parts]
        pv = [dotf(w[p], vj) for p in parts]
        top = None
        for p in parts:
            acc_ref[p] += pv[p]
            c = car[p] + jnp.sum(lk[p], axis=-1, keepdims=True)
            car_ref[p] = c
            top = jnp.max(c) if top is None else jnp.maximum(top, jnp.max(c))
        return j - 1, top

    def cond(state):
        j, top = state
        return jnp.logical_and(j >= 0, top > -SB_SKIP)

    lax.while_loop(cond, body, (i, jnp.float32(0.0)))
    for p in parts:
        o_ref[p * rows:(p + 1) * rows, :] = acc_ref[p].astype(o_ref.dtype)


def _sb_attention(z):
    t = z.shape[0]
    heads = BRANCH_W // SB_DIM
    upper = jnp.asarray(np.triu(np.ones((SB_BLOCK, SB_BLOCK), np.float32), 1).T, dtype=BF16)
    return pl.pallas_call(
        _sb_body,
        grid=(heads, t // SB_BLOCK),
        in_specs=[pl.BlockSpec((SB_BLOCK, SB_DIM), lambda h, i: (i, h)),
                  pl.BlockSpec((t, SB_DIM), lambda h, i: (0, heads + h)),
                  pl.BlockSpec((t, SB_DIM), lambda h, i: (0, 2 * heads + h)),
                  pl.BlockSpec((SB_BLOCK, SB_BLOCK), lambda h, i: (0, 0))],
        out_specs=pl.BlockSpec((SB_BLOCK, SB_DIM), lambda h, i: (i, h)),
        out_shape=jax.ShapeDtypeStruct((t, heads * SB_DIM), BF16),
        scratch_shapes=[pltpu.VMEM((SB_PARTS, SB_BLOCK // SB_PARTS, SB_DIM), F32),
                        pltpu.VMEM((SB_PARTS, SB_BLOCK // SB_PARTS, 1), F32)],
        compiler_params=_cparams(("arbitrary", "arbitrary")),
        name="stick_breaking",
    )(z, z, z, upper)


def _merge_body(b0_ref, b1_ref, b2_ref, b3_ref, gate_ref, wbr_ref, wout_ref, x_ref, lnw_ref, lnb_ref,
                xo_ref, xb_ref, xp_ref):
    merged = None
    for g, b_ref in enumerate((b0_ref, b1_ref, b2_ref, b3_ref)):
        y = jnp.dot(b_ref[...], wbr_ref[g], preferred_element_type=F32)
        y = y * gate_ref[:, g * D_MODEL:(g + 1) * D_MODEL].astype(F32)
        merged = y if merged is None else merged + y
    mix = jnp.dot(merged.astype(BF16), wout_ref[...], preferred_element_type=F32)
    y = _layer_norm(DEEPNORM_ALPHA * x_ref[...] + mix, lnw_ref[...], lnb_ref[...])
    xo_ref[...] = y
    xb_ref[...] = y.astype(BF16)
    _store_rows(xp_ref, _pack_pair(y[:, :D_MODEL // 2], y[:, D_MODEL // 2:]))


def _merge(branches, gates, w_br, w_out, x, ln_w, ln_b, tm=256):
    t = x.shape[0]
    tm = min(tm, t)
    row = lambda i: (i, 0)
    const2 = lambda i: (0, 0)
    return pl.pallas_call(
        _merge_body,
        grid=(t // tm,),
        in_specs=[pl.BlockSpec((tm, BRANCH_W), row)] * 4 + [
            pl.BlockSpec((tm, N_BRANCHES * D_MODEL), row),
            pl.BlockSpec((N_BRANCHES, BRANCH_W, D_MODEL), lambda i: (0, 0, 0)),
            pl.BlockSpec((D_MODEL, D_MODEL), const2),
            pl.BlockSpec((tm, D_MODEL), row),
            pl.BlockSpec((1, D_MODEL), const2),
            pl.BlockSpec((1, D_MODEL), const2)],
        out_specs=[pl.BlockSpec((tm, D_MODEL), row), pl.BlockSpec((tm, D_MODEL), row),
                   pl.BlockSpec((tm * ROW_TILES, LANES), row)],
        out_shape=[jax.ShapeDtypeStruct((t, D_MODEL), F32), jax.ShapeDtypeStruct((t, D_MODEL), BF16),
                   jax.ShapeDtypeStruct((t * ROW_TILES, LANES), U32)],
        compiler_params=_cparams(("parallel",), 56),
        name="merge_ln",
    )(*branches, gates, w_br, w_out, x, ln_w, ln_b)


def _first_index(hit, idx, size, axis):
    return jnp.min(jnp.where(hit, idx, size), axis=axis, keepdims=True)


def _router_body(x_ref, wr_ref, bias_ref, before_ref, idx_ref, w_ref, rank_ref, cnt_ref, run_ref, *, tm):
    @pl.when(pl.program_id(0) == 0)
    def _():
        run_ref[...] = jnp.zeros_like(run_ref)

    per = N_EXPERTS // N_GROUPS
    logits = lax.dot_general(wr_ref[...], x_ref[...], _NT, precision=HI, preferred_element_type=F32)
    scores = _sigmoid(logits)
    biased = scores + bias_ref[:, 0:1]
    g3 = biased.reshape(N_GROUPS, per, tm)
    pos = lax.broadcasted_iota(I32, (N_GROUPS, per, tm), 1)
    m1 = jnp.max(g3, axis=1, keepdims=True)
    f1 = _first_index(g3 == m1, pos, per, 1)
    m2 = jnp.max(jnp.where(pos == f1, -jnp.inf, g3), axis=1, keepdims=True)
    gscore = (m1 + m2).reshape(N_GROUPS, tm)

    gpos = lax.broadcasted_iota(I32, (N_GROUPS, tm), 0)
    chosen = jnp.zeros((N_GROUPS, tm), F32)
    cur = gscore
    for _ in range(TOPK_GROUPS):
        m = jnp.max(cur, axis=0, keepdims=True)
        pick = gpos == _first_index(cur == m, gpos, N_GROUPS, 0)
        chosen = jnp.where(pick, 1.0, chosen)
        cur = jnp.where(pick, -jnp.inf, cur)
    ok = jnp.broadcast_to(chosen.reshape(N_GROUPS, 1, tm), (N_GROUPS, per, tm)).reshape(N_EXPERTS, tm)

    epos = lax.broadcasted_iota(I32, (N_EXPERTS, tm), 0)
    cur = jnp.where(ok > 0.5, biased, -jnp.inf)
    picks, idx_rows, w_rows = [], [], []
    member = jnp.zeros((N_EXPERTS, tm), F32)
    for _ in range(TOP_K):
        m = jnp.max(cur, axis=0, keepdims=True)
        f = _first_index(cur == m, epos, N_EXPERTS, 0)
        pick = epos == f
        picks.append(pick)
        idx_rows.append(f)
        w_rows.append(jnp.sum(jnp.where(pick, scores, 0.0), axis=0, keepdims=True))
        member = jnp.where(pick, 1.0, member)
        cur = jnp.where(pick, -jnp.inf, cur)
    w_sel = jnp.concatenate(w_rows, axis=0)
    w_ref[...] = ROUTED_SCALE * w_sel / jnp.sum(w_sel, axis=0, keepdims=True)
    idx_ref[...] = jnp.concatenate(idx_rows, axis=0)

    seen = jnp.dot(member.astype(BF16), before_ref[...], preferred_element_type=F32) + run_ref[:, 0:1]
    rank_rows = [jnp.sum(jnp.where(pk, seen, 0.0), axis=0, keepdims=True) for pk in picks]
    rank_ref[...] = jnp.concatenate(rank_rows, axis=0).astype(I32)
    run_ref[...] = run_ref[...] + jnp.sum(member, axis=1, keepdims=True)
    cnt_ref[...] = run_ref[...]


def _router(x, w_router_t, bias, tm=512):
    t = x.shape[0]
    tm = min(tm, t)
    before = jnp.asarray(np.triu(np.ones((tm, tm), np.float32), 1), dtype=BF16)
    slot = lambda i: (0, i)
    const = lambda i: (0, 0)
    return pl.pallas_call(
        functools.partial(_router_body, tm=tm),
        grid=(t // tm,),
        in_specs=[pl.BlockSpec((tm, D_MODEL), lambda i: (i, 0)),
                  pl.BlockSpec((N_EXPERTS, D_MODEL), const),
                  pl.BlockSpec((N_EXPERTS, LANES), const),
                  pl.BlockSpec((tm, tm), const)],
        out_specs=[pl.BlockSpec((TOP_K, tm), slot), pl.BlockSpec((TOP_K, tm), slot),
                   pl.BlockSpec((TOP_K, tm), slot), pl.BlockSpec((N_EXPERTS, LANES), const)],
        out_shape=[jax.ShapeDtypeStruct((TOP_K, t), I32), jax.ShapeDtypeStruct((TOP_K, t), F32),
                   jax.ShapeDtypeStruct((TOP_K, t), I32), jax.ShapeDtypeStruct((N_EXPERTS, LANES), F32)],
        scratch_shapes=[pltpu.VMEM((N_EXPERTS, LANES), F32)],
        compiler_params=_cparams(("arbitrary",)),
        name="router",
    )(x, w_router_t, bias, before)


def _dest_body(start_ref, idx_ref, rank_ref, dest_ref):
    idx = idx_ref[...]
    base = jnp.zeros(idx.shape, I32)
    for e in range(N_EXPERTS):
        base = jnp.where(idx == e, start_ref[e], base)
    dest_ref[...] = base + rank_ref[...]


def _dest_rows(pad_start, idx, rank, tm=2048):
    t = idx.shape[1]
    tm = min(tm, t)
    slot = lambda i, s: (0, i)
    return pl.pallas_call(
        _dest_body,
        grid_spec=pltpu.PrefetchScalarGridSpec(
            num_scalar_prefetch=1, grid=(t // tm,),
            in_specs=[pl.BlockSpec((TOP_K, tm), slot), pl.BlockSpec((TOP_K, tm), slot)],
            out_specs=pl.BlockSpec((TOP_K, tm), slot)),
        out_shape=jax.ShapeDtypeStruct((TOP_K, t), I32),
        compiler_params=_cparams(("parallel",)),
        name="dest_rows",
    )(pad_start, idx, rank)


def _dispatch_body(zb_ref, dest_ref, x_ref, xs_hbm, zero_ref, sem, zsem, *, tm, n_fill):
    @pl.when(pl.program_id(0) == 0)
    def _():
        zero_ref[...] = jnp.zeros_like(zero_ref)

        def fill(i, carry):
            @pl.when(zb_ref[i] >= 0)
            def _():
                r0 = pl.multiple_of(zb_ref[i] * (ROW_BLOCK * ROW_TILES), ROW_BLOCK * ROW_TILES)
                pltpu.make_async_copy(zero_ref, xs_hbm.at[pl.ds(r0, ROW_BLOCK * ROW_TILES)], zsem).start()
            return carry

        def drain(i, carry):
            @pl.when(zb_ref[i] >= 0)
            def _():
                pltpu.make_async_copy(zero_ref, xs_hbm.at[pl.ds(0, ROW_BLOCK * ROW_TILES)], zsem).wait()
            return carry

        lax.fori_loop(0, n_fill, fill, 0)
        lax.fori_loop(0, n_fill, drain, 0)

    def row(t, carry):
        for k in range(TOP_K):
            pltpu.make_async_copy(_row(x_ref, t), _row(xs_hbm, dest_ref[k, t]), sem).start(
                priority=k % 2)
        return carry

    lax.fori_loop(0, tm, row, 0)
    for _ in range(TOP_K):
        pltpu.make_async_copy(x_ref, xs_hbm.at[pl.ds(0, tm * ROW_TILES)], sem).wait()


def _dispatch(zero_blocks, dest, x_packed, n_rows, tm=256):
    t = x_packed.shape[0] // ROW_TILES
    tm = min(tm, t)
    return pl.pallas_call(
        functools.partial(_dispatch_body, tm=tm, n_fill=zero_blocks.shape[0]),
        grid_spec=pltpu.PrefetchScalarGridSpec(
            num_scalar_prefetch=1, grid=(t // tm,),
            in_specs=[pl.BlockSpec((TOP_K, tm), lambda i, zb: (0, i), memory_space=pltpu.SMEM),
                      pl.BlockSpec((tm * ROW_TILES, LANES), lambda i, zb: (i, 0))],
            out_specs=pl.BlockSpec(memory_space=pl.ANY),
            scratch_shapes=[pltpu.VMEM((ROW_BLOCK * ROW_TILES, LANES), U32), pltpu.SemaphoreType.DMA(()),
                            pltpu.SemaphoreType.DMA(())]),
        out_shape=jax.ShapeDtypeStruct((n_rows * ROW_TILES, LANES), U32),
        compiler_params=_cparams(("arbitrary",)),
        name="dispatch",
    )(zero_blocks, dest, x_packed)


def _experts_body(be_ref, nu_ref, nxt_ref, slot_ref, xs_ref, wg_hbm, wu_hbm, wd_hbm, ys_ref,
                  wgf_ref, wuf_ref, wdf_ref, wgb_ref, wub_ref, wdb_ref, sem, *, layer):
    def weight_copies(e, s):
        return (pltpu.make_async_copy(wg_hbm.at[layer, e], wgf_ref.at[s], sem.at[s]),
                pltpu.make_async_copy(wu_hbm.at[layer, e], wuf_ref.at[s], sem.at[s]),
                pltpu.make_async_copy(wd_hbm.at[layer, e], wdf_ref.at[s], sem.at[s]))

    @pl.when(pl.program_id(0) == 0)
    def _():
        for c in weight_copies(be_ref[0], slot_ref[0]):
            c.start()

    def block(b, xs_blk, ys_blk):
        used = b < nu_ref[0]
        new_expert = jnp.logical_or(b == 0, be_ref[b] != be_ref[jnp.maximum(b - 1, 0)])

        @pl.when(jnp.logical_and(used, new_expert))
        def _():
            s = slot_ref[b]
            for c in weight_copies(be_ref[b], s):
                c.wait()

            @pl.when(nxt_ref[b] >= 0)
            def _():
                for c in weight_copies(nxt_ref[b], 1 - s):
                    c.start()

            wgb_ref[...] = wgf_ref[s].astype(BF16)
            wub_ref[...] = wuf_ref[s].astype(BF16)
            wdb_ref[...] = wdf_ref[s].astype(BF16)

        @pl.when(used)
        def _():
            half = D_MODEL // 2
            lo, hi = _unpack_pair(_load_rows(xs_blk))
            lo = lo.astype(BF16)
            hi = hi.astype(BF16)
            gate = (jnp.dot(lo, wgb_ref[:half, :], preferred_element_type=F32)
                    + jnp.dot(hi, wgb_ref[half:, :], preferred_element_type=F32))
            up = (jnp.dot(lo, wub_ref[:half, :], preferred_element_type=F32)
                  + jnp.dot(hi, wub_ref[half:, :], preferred_element_type=F32))
            h = (_silu(gate) * up).astype(BF16)
            y = jnp.dot(h, wdb_ref[...], preferred_element_type=F32)
            _store_rows(ys_blk, _pack_pair(y[:, :half], y[:, half:]))

        @pl.when(jnp.logical_not(used))
        def _():
            ys_blk[...] = jnp.zeros_like(ys_blk)

    rows = ROW_BLOCK * ROW_TILES
    for sub in range(EXPERT_STEP_BLOCKS):
        block(pl.program_id(0) * EXPERT_STEP_BLOCKS + sub,
              xs_ref.at[pl.ds(sub * rows, rows)], ys_ref.at[pl.ds(sub * rows, rows)])


def _experts(block_expert, n_used, next_expert, slot, xs, wg, wu, wd, layer):
    n_rows = xs.shape[0] // ROW_TILES
    n_steps = n_rows // (ROW_BLOCK * EXPERT_STEP_BLOCKS)
    step_rows = EXPERT_STEP_BLOCKS * ROW_BLOCK * ROW_TILES
    blk = lambda i, be, nu, nx, sl: (jnp.minimum(i, (nu[0] - 1) // EXPERT_STEP_BLOCKS), 0)
    out_blk = lambda i, be, nu, nx, sl: (i, 0)
    hbm = pl.BlockSpec(memory_space=pl.ANY)
    return pl.pallas_call(
        functools.partial(_experts_body, layer=layer),
        grid_spec=pltpu.PrefetchScalarGridSpec(
            num_scalar_prefetch=4, grid=(n_steps,),
            in_specs=[pl.BlockSpec((step_rows, LANES), blk), hbm, hbm, hbm],
            out_specs=pl.BlockSpec((step_rows, LANES), out_blk),
            scratch_shapes=[pltpu.VMEM((2, D_MODEL, EXPERT_W), F32), pltpu.VMEM((2, D_MODEL, EXPERT_W), F32),
                            pltpu.VMEM((2, EXPERT_W, D_MODEL), F32),
                            pltpu.VMEM((D_MODEL, EXPERT_W), BF16), pltpu.VMEM((D_MODEL, EXPERT_W), BF16),
                            pltpu.VMEM((EXPERT_W, D_MODEL), BF16), pltpu.SemaphoreType.DMA((2,))]),
        out_shape=jax.ShapeDtypeStruct((n_rows * ROW_TILES, LANES), U32),
        compiler_params=_cparams(("arbitrary",), 56),
        name="experts",
    )(block_expert, n_used, next_expert, slot, xs, wg, wu, wd)


def _combine_body(dest_ref, dnext_ref, w_ref, x_ref, xb_ref, sg_ref, su_ref, sd_ref, lnw_ref, lnb_ref, ys_hbm,
                  xo_ref, xb_out_ref, buf_ref, lo_ref, hi_ref, sem, *, tm, n_steps):
    i = pl.program_id(0)
    cur = i % 2

    def request(d_ref, s):
        def row(t, carry):
            for k in range(TOP_K):
                pltpu.make_async_copy(_row(ys_hbm, d_ref[k, t]), _row(buf_ref.at[s, k], t), sem.at[s]).start(
                    priority=k % 2)
            return carry

        lax.fori_loop(0, tm, row, 0)

    @pl.when(i == 0)
    def _():
        request(dest_ref, 0)

    @pl.when(i + 1 < n_steps)
    def _():
        request(dnext_ref, 1 - cur)

    for k in range(TOP_K):
        pltpu.make_async_copy(ys_hbm.at[pl.ds(0, tm * ROW_TILES)], buf_ref.at[cur, k], sem.at[cur]).wait()

    xb = xb_ref[...]
    h = _silu(jnp.dot(xb, sg_ref[...], preferred_element_type=F32)) * jnp.dot(xb, su_ref[...],
                                                                           preferred_element_type=F32)
    shared = jnp.dot(h.astype(BF16), sd_ref[...], preferred_element_type=F32)

    lo_acc = hi_acc = None
    for k in range(TOP_K):
        lo, hi = _unpack_pair(buf_ref[cur, k])
        wk = w_ref[:, k:k + 1]
        lo_acc = wk * lo if lo_acc is None else lo_acc + wk * lo
        hi_acc = wk * hi if hi_acc is None else hi_acc + wk * hi
    lo_ref[...] = lo_acc
    hi_ref[...] = hi_acc
    routed = jnp.concatenate([_load_rows(lo_ref), _load_rows(hi_ref)], axis=1)
    y = _layer_norm(DEEPNORM_ALPHA * x_ref[...] + routed + shared, lnw_ref[...], lnb_ref[...])
    xo_ref[...] = y
    xb_out_ref[...] = y.astype(BF16)


def _combine(dest, w_sel, x, xb, sg, su, sd, ln_w, ln_b, ys, tm=256):
    t = x.shape[0]
    tm = min(tm, t)
    n_steps = t // tm
    row = lambda i: (i, 0)
    slot = lambda i: (0, i)
    slot_next = lambda i: (0, jnp.minimum(i + 1, n_steps - 1))
    const = lambda i: (0, 0)
    return pl.pallas_call(
        functools.partial(_combine_body, tm=tm, n_steps=n_steps),
        grid=(n_steps,),
        in_specs=[pl.BlockSpec((TOP_K, tm), slot, memory_space=pltpu.SMEM),
                  pl.BlockSpec((TOP_K, tm), slot_next, memory_space=pltpu.SMEM),
                  pl.BlockSpec((tm * ROW_TILES, TOP_K), row),
                  pl.BlockSpec((tm, D_MODEL), row),
                  pl.BlockSpec((tm, D_MODEL), row),
                  pl.BlockSpec((D_MODEL, EXPERT_W), const),
                  pl.BlockSpec((D_MODEL, EXPERT_W), const),
                  pl.BlockSpec((EXPERT_W, D_MODEL), const),
                  pl.BlockSpec((1, D_MODEL), const),
                  pl.BlockSpec((1, D_MODEL), const),
                  pl.BlockSpec(memory_space=pl.ANY)],
        out_specs=[pl.BlockSpec((tm, D_MODEL), row), pl.BlockSpec((tm, D_MODEL), row)],
        out_shape=[jax.ShapeDtypeStruct((t, D_MODEL), F32), jax.ShapeDtypeStruct((t, D_MODEL), BF16)],
        scratch_shapes=[pltpu.VMEM((2, TOP_K, tm * ROW_TILES, LANES), U32),
                        pltpu.VMEM((tm * ROW_TILES, LANES), F32), pltpu.VMEM((tm * ROW_TILES, LANES), F32),
                        pltpu.SemaphoreType.DMA((2,))],
        compiler_params=_cparams(("arbitrary",), 56),
        name="combine_ln",
    )(dest, dest, w_sel, x, xb, sg, su, sd, ln_w, ln_b, ys)


def _pad_cols(a, width):
    return jnp.pad(a, [(0, 0)] * (a.ndim - 1) + [(0, width - a.shape[-1])])


def _pad_rows(a, height):
    return jnp.pad(a, ((0, height - a.shape[0]), (0, 0)))


def _pad_heads(a, heads, dk):
    lead = a.shape[:-1]
    padded = jnp.pad(a.reshape(*lead, heads, dk), [(0, 0)] * (len(lead) + 1) + [(0, HEAD_PAD - dk)])
    return padded.reshape(*lead, heads * HEAD_PAD)


def _relayout_in_weight(w_in):
    w_rw = w_in[..., RW_OFF:SB_OFF]
    w_sb = w_in[..., SB_OFF:GLA_OFF]
    w_gla = w_in[..., GLA_OFF:GATE_OFF]
    zcol = lambda n: jnp.zeros(w_in.shape[:-1] + (n,), w_in.dtype)
    return jnp.concatenate([
        w_in[..., HG_OFF:RW_OFF],
        w_rw[..., :1536], _pad_cols(w_rw[..., 1536:1568], LANES), _pad_cols(w_rw[..., 1568:1600], LANES),
        _pad_cols(w_rw[..., 1600:1696], LANES), zcol(RW_Z - 3 * BRANCH_W - 3 * LANES),
        w_sb[..., :BRANCH_W] * (SB_DIM ** -0.5), w_sb[..., BRANCH_W:],
        _pad_heads(w_gla[..., :256], 4, GLA_DK), _pad_heads(w_gla[..., 256:512], 4, GLA_DK),
        w_gla[..., 512:1024], w_gla[..., 1040:1552], _pad_cols(w_gla[..., 1024:1040], LANES),
        zcol(GLA_Z - 4 * BRANCH_W - LANES),
        w_in[..., GATE_OFF:]], axis=-1).astype(BF16)


def _token_mixing(xb, w_all, layer, lower_bound, hg_norm_w, rw_mu, rw_w0, rw_w2, rw_a0, rw_a2, rw_g2, rw_kk,
                  rw_ka, rw_rk, rw_ln_w, rw_ln_b, gla_g2, gla_gb, gla_norm_w):
    row = lambda a: a.reshape(1, -1).astype(F32)

    z_hg = _matmul(xb, w_all, layer, ZHG_OFF, 4 * BRANCH_W, F32, tn=2 * PROJ_TN)
    o_hg = _gated_mixer(z_hg, row(lower_bound), row(hg_norm_w), jnp.zeros((8, BRANCH_W), F32), "hgrn2")

    z_rw = _matmul(xb, w_all, layer, ZRW_OFF, RW_Z, F32, tn=2 * PROJ_TN)
    mu = jnp.concatenate([rw_mu[:1536], jnp.pad(rw_mu[1536:1568], (0, 96)), jnp.pad(rw_mu[1568:1600], (0, 96)),
                          jnp.pad(rw_mu[1600:1696], (0, 32 + RW_Z - 3 * BRANCH_W - 3 * LANES))]).reshape(1, RW_Z)
    vecs = jnp.stack([rw_w0, rw_a0, rw_kk, rw_ka, rw_rk, rw_ln_w, rw_ln_b, jnp.zeros_like(rw_w0)]).astype(F32)
    o_rw = _rwkv_mixer(z_rw, mu, vecs, _pad_rows(rw_w2, LANES), _pad_rows(rw_a2, LANES), _pad_rows(rw_g2, LANES))

    o_sb = _sb_attention(_matmul(xb, w_all, layer, ZSB_OFF, 3 * BRANCH_W, BF16))

    z_gla = _matmul(xb, w_all, layer, ZGLA_OFF, GLA_Z, F32)
    g2p = _pad_rows(_pad_heads(gla_g2, 4, GLA_DK), LANES)
    gbp = _pad_heads(gla_gb.reshape(1, -1), 4, GLA_DK)
    o_gla = _gated_mixer(z_gla, gbp, row(gla_norm_w), g2p, "gla")

    gates = _matmul(xb, w_all, layer, ZGATE_OFF, N_BRANCHES * D_MODEL, BF16, act="sigmoid", tn=1024)
    return (o_hg, o_rw, o_sb, o_gla), gates


def _moe(x, xb, xp, w_router, router_bias, we_gate, we_up, we_down, layer, ws_gate, ws_up, ws_down, ln_w, ln_b):
    t = x.shape[0]
    bias = jnp.broadcast_to(router_bias.astype(F32).reshape(N_EXPERTS, 1), (N_EXPERTS, LANES))
    idx, w_sel, rank, counts = _router(x, w_router.T.astype(F32), bias)

    cnt = counts[:, 0].astype(I32)
    padded = (cnt + ROW_BLOCK - 1) // ROW_BLOCK * ROW_BLOCK
    pad_end = jnp.cumsum(padded)
    pad_start = (pad_end - padded).astype(I32)
    n_blocks = -(-(t * TOP_K // ROW_BLOCK + N_EXPERTS) // EXPERT_STEP_BLOCKS) * EXPERT_STEP_BLOCKS
    n_used = (pad_end[-1:] // ROW_BLOCK).astype(I32)
    first_row = jnp.arange(n_blocks, dtype=I32) * ROW_BLOCK
    block_expert = jnp.minimum(jnp.sum(pad_end[None, :] <= first_row[:, None], axis=1), N_EXPERTS - 1).astype(I32)

    last_block = jnp.where(padded > 0, pad_end // ROW_BLOCK - 1, -1)
    tail = n_used[0] + jnp.arange(N_EXPERTS, dtype=I32)
    zero_blocks = jnp.concatenate([last_block, jnp.where(tail < n_blocks, tail, -1)]).astype(I32)

    has_rows = cnt > 0
    eid = jnp.arange(N_EXPERTS, dtype=I32)
    later = jnp.where(has_rows[None, :] & (eid[None, :] > eid[:, None]), eid[None, :], N_EXPERTS)
    next_used = jnp.min(later, axis=1)
    next_used = jnp.where(next_used < N_EXPERTS, next_used, -1).astype(I32)
    ordinal = jnp.cumsum(has_rows.astype(I32)) - 1
    of_block = block_expert[:, None] == eid[None, :]
    next_expert = jnp.sum(jnp.where(of_block, next_used[None, :], 0), axis=1).astype(I32)
    slot = (jnp.sum(jnp.where(of_block, ordinal[None, :], 0), axis=1) % 2).astype(I32)

    dest = _dest_rows(pad_start, idx, rank)
    xs = _dispatch(zero_blocks, dest, xp, n_blocks * ROW_BLOCK)
    ys = _experts(block_expert, n_used, next_expert, slot, xs, we_gate, we_up, we_down, layer)
    w_rows = jnp.repeat(w_sel.T, ROW_TILES, axis=0)
    return _combine(dest, w_rows, x, xb, ws_gate.astype(BF16), ws_up.astype(BF16), ws_down.astype(BF16),
                    ln_w.reshape(1, -1), ln_b.reshape(1, -1), ys)


def kernel(x, w_in, hg_lb_logits, hg_norm_w, rw_mu, rw_w0, rw_w2, rw_a0, rw_a2, rw_g2, rw_kk, rw_ka, rw_rk,
           rw_ln_w, rw_ln_b, gla_g2, gla_gb, gla_norm_w, w_br, w_out, ln1_w, ln1_b, w_router, router_bias,
           we_gate, we_up, we_down, ws_gate, ws_up, ws_down, ln2_w, ln2_b):
    bsz, t, d = x.shape
    cum = jnp.cumsum(jax.nn.softmax(hg_lb_logits.astype(F32), axis=0), axis=0)
    lower_bounds = cum - cum[0:1]
    w_all = _relayout_in_weight(w_in)
    outs = []
    for bi in range(bsz):
        xf = x[bi].astype(F32)
        xb = xf.astype(BF16)
        for l in range(DEPTH):
            branches, gates = _token_mixing(
                xb, w_all, l, lower_bounds[l], hg_norm_w[l], rw_mu[l], rw_w0[l], rw_w2[l], rw_a0[l], rw_a2[l],
                rw_g2[l], rw_kk[l], rw_ka[l], rw_rk[l], rw_ln_w[l], rw_ln_b[l], gla_g2[l], gla_gb[l],
                gla_norm_w[l])
            xf, xb, xp = _merge(branches, gates, w_br[l].astype(BF16), w_out[l].astype(BF16), xf,
                                ln1_w[l].reshape(1, -1), ln1_b[l].reshape(1, -1))
            xf, xb = _moe(xf, xb, xp, w_router[l], router_bias[l], we_gate, we_up, we_down, l,
                          ws_gate[l], ws_up[l], ws_down[l], ln2_w[l], ln2_b[l])
        outs.append(xf)
    return jnp.stack(outs).astype(x.dtype)
```

```python
import functools

import jax
import jax.numpy as jnp
import numpy as np
from jax import lax
from jax.experimental import pallas as pl
from jax.experimental.pallas import tpu as pltpu

F32 = jnp.float32
BF16 = jnp.bfloat16
I32 = jnp.int32
U32 = jnp.uint32

D_MODEL = 2048
DEPTH = 2
BRANCH_W = 512
N_BRANCHES = 4
CHUNK = 64
SUB = 16
N_SUB = CHUNK // SUB
LANES = 128
HEAD_PAD = 128
RW_HEADS = 8
RW_DIM = 64
RW_PAIRS = RW_HEADS // 2
GATED_SPAN = 4
RW_SPAN = 4
GLA_DK = 64
GLA_TAU = 16.0
SB_DIM = 128
SB_BLOCK = 256
SB_PARTS = 2
SB_SKIP = 120.0
N_EXPERTS = 64
TOP_K = 8
N_GROUPS = 8
TOPK_GROUPS = 4
EXPERT_W = 512
ROUTED_SCALE = 2.5
ROW_BLOCK = 256
EXPERT_STEP_BLOCKS = 2
MERGE_TN = 512
DEEPNORM_ALPHA = (2 * DEPTH) ** 0.25
LN_EPS = 1e-5
RW_GN_EPS = 64e-5
LOG2E = 1.4426950408889634

HG_OFF = 0
RW_OFF = 2048
SB_OFF = 3744
GLA_OFF = 5280
GATE_OFF = 6832
PROJ_TN = 512
RW_Z = 2048
GLA_Z = 2560
ZHG_OFF = 0
ZRW_OFF = ZHG_OFF + 4 * BRANCH_W
ZSB_OFF = ZRW_OFF + RW_Z
ZGLA_OFF = ZSB_OFF + 3 * BRANCH_W
ZGATE_OFF = ZGLA_OFF + GLA_Z
Z_TOTAL = ZGATE_OFF + N_BRANCHES * D_MODEL

_NT = (((1,), (1,)), ((), ()))
_TN = (((0,), (0,)), ((), ()))


def _cparams(sem, vmem_mb=48):
    return pltpu.CompilerParams(dimension_semantics=sem, vmem_limit_bytes=vmem_mb << 20)


def _sigmoid(x):
    return 1.0 / (1.0 + jnp.exp(-x))


def _log_sigmoid(x):
    return jnp.minimum(x, 0.0) - jnp.log1p(jnp.exp(-jnp.abs(x)))


def _silu(x):
    return x * _sigmoid(x)


def _layer_norm(y, w, b):
    mu = jnp.mean(y, axis=-1, keepdims=True)
    d = y - mu
    var = jnp.mean(d * d, axis=-1, keepdims=True)
    return d * lax.rsqrt(var + LN_EPS) * w + b


def _pack_pair(lo, hi):
    lo_b = lax.bitcast_convert_type(lo.astype(BF16).astype(F32), U32) >> 16
    hi_b = lax.bitcast_convert_type(hi.astype(BF16).astype(F32), U32) & jnp.uint32(0xFFFF0000)
    return lo_b | hi_b


def _unpack_pair(u):
    lo = lax.bitcast_convert_type(u << 16, F32)
    hi = lax.bitcast_convert_type(u & jnp.uint32(0xFFFF0000), F32)
    return lo, hi


ROW_TILES = D_MODEL // 2 // LANES


def _store_rows(ref, packed):
    r = packed.shape[0]
    for j in range(ROW_TILES):
        ref[pl.ds(j, r, stride=ROW_TILES), :] = packed[:, j * LANES:(j + 1) * LANES]


def _load_rows(ref):
    r = ref.shape[0] // ROW_TILES
    return jnp.concatenate([ref[pl.ds(j, r, stride=ROW_TILES), :] for j in range(ROW_TILES)], axis=1)


def _row(ref, i):
    return ref.at[pl.ds(pl.multiple_of(i * ROW_TILES, ROW_TILES), ROW_TILES)]


def _split(x):
    hi = x.astype(BF16)
    return hi, (x - hi.astype(F32)).astype(BF16)


def _dot2(x, ones_bf16):
    hi, lo = _split(x)
    return (jnp.dot(hi, ones_bf16, preferred_element_type=F32)
            + jnp.dot(lo, ones_bf16, preferred_element_type=F32))


def _ldot2(ones_bf16, x):
    hi, lo = _split(x)
    return (jnp.dot(ones_bf16, hi, preferred_element_type=F32)
            + jnp.dot(ones_bf16, lo, preferred_element_type=F32))


def _dot3(a, b):
    a_hi, a_lo = _split(a)
    b_hi, b_lo = _split(b)
    return (jnp.dot(a_hi, b_hi, preferred_element_type=F32) + jnp.dot(a_hi, b_lo, preferred_element_type=F32)
            + jnp.dot(a_lo, b_hi, preferred_element_type=F32))


def _mm_body(x_ref, w_ref, o_ref, *, act):
    acc = jnp.dot(x_ref[...], w_ref[...], preferred_element_type=F32)
    if act == "sigmoid":
        acc = _sigmoid(acc)
    o_ref[...] = acc.astype(o_ref.dtype)


def _matmul(x, w, layer, col0, n, out_dtype, act=None, tm=1024, tn=PROJ_TN):
    m, k = x.shape
    tm = min(tm, m)
    off = col0 // tn
    return pl.pallas_call(
        functools.partial(_mm_body, act=act),
        grid=(m // tm, n // tn),
        in_specs=[pl.BlockSpec((tm, k), lambda i, j: (i, 0)),
                  pl.BlockSpec((None, k, tn), lambda i, j: (layer, 0, off + j))],
        out_specs=pl.BlockSpec((tm, tn), lambda i, j: (i, j)),
        out_shape=jax.ShapeDtypeStruct((m, n), out_dtype),
        compiler_params=_cparams(("parallel", "parallel")),
        name="proj",
    )(x, w)


def _gated_span(q, k, v, g, st, tril):
    hs = range(len(q))
    lanes = [(ci, h) for ci in range(GATED_SPAN) for h in hs]
    ls = range(len(lanes))
    subs = range(N_SUB)
    dotf = functools.partial(jnp.dot, preferred_element_type=F32)
    blk = lambda x, i: x[i * SUB:(i + 1) * SUB]
    cut = lambda xs: [xs[h][ci * CHUNK:(ci + 1) * CHUNK] for ci, h in lanes]
    b_f = [_ldot2(tril, g[h] * LOG2E) for h in hs]
    qe_f = [(q[h] * jnp.exp2(b_f[h])).astype(BF16) for h in hs]
    vb_f = [v[h].astype(BF16) for h in hs]
    b, qc, kc, vb, qe = cut(b_f), cut(q), cut(k), cut(vb_f), cut(qe_f)
    blast = [b[i][CHUNK - 1:CHUNK, :] for i in ls]
    ends = [[b[i][(j + 1) * SUB - 1:(j + 1) * SUB, :] for j in subs] for i in ls]
    khat = [[blk(kc[i], j) * jnp.exp2(ends[i][j] - blk(b[i], j)) for j in subs] for i in ls]
    o_parts = [[None] * N_SUB for _ in ls]

    for j in range(N_SUB - 1):
        lo = (j + 1) * SUB
        qs = [(qc[i][lo:] * jnp.exp2(b[i][lo:] - ends[i][j])).astype(BF16) for i in ls]
        a = [lax.dot_general(qs[i], khat[i][j].astype(BF16), _NT, preferred_element_type=F32) for i in ls]
        pv = [dotf(a[i].astype(BF16), blk(vb[i], j)) for i in ls]
        for i in ls:
            for t in range(j + 1, N_SUB):
                piece = pv[i][(t - j - 1) * SUB:(t - j) * SUB]
                o_parts[i][t] = piece if o_parts[i][t] is None else o_parts[i][t] + piece

    lane = lax.broadcasted_iota(I32, (SUB, LANES), 1)
    trow = lax.broadcasted_iota(I32, (SUB, 1), 0)
    half = SUB // 2
    for t in subs:
        d_top = [jnp.zeros((half, LANES), F32) for _ in ls]
        d_bot = [jnp.zeros((half, LANES), F32) for _ in ls]
        for s in range(SUB):
            for i in ls:
                bi, qi, ki = blk(b[i], t), blk(qc[i], t), blk(kc[i], t)
                if s < half:
                    col = jnp.sum(qi * jnp.exp2(bi - bi[s:s + 1, :]) * ki[s:s + 1, :], axis=-1, keepdims=True)
                    d_top[i] = jnp.where(lane[:half] == s, col[:half], d_top[i])
                    d_bot[i] = jnp.where(lane[:half] == s, col[half:], d_bot[i])
                else:
                    col = jnp.sum(qi[half:] * jnp.exp2(bi[half:] - bi[s:s + 1, :]) * ki[s:s + 1, :],
                                  axis=-1, keepdims=True)
                    d_bot[i] = jnp.where(lane[:half] == s, col, d_bot[i])
        d = [jnp.concatenate([d_top[i], d_bot[i]], axis=0) for i in ls]
        pv = [dotf(jnp.where(lane <= trow, d[i], 0.0)[:, :SUB].astype(BF16), blk(vb[i], t)) for i in ls]
        for i in ls:
            o_parts[i][t] = pv[i] if o_parts[i][t] is None else o_parts[i][t] + pv[i]

    o_intra = [jnp.concatenate(o_parts[i], axis=0) for i in ls]
    kd = [jnp.concatenate([khat[i][j] * jnp.exp2(blast[i] - ends[i][j]) for j in subs], axis=0).astype(BF16)
          for i in ls]
    kv = [lax.dot_general(vb[i], kd[i], _TN, preferred_element_type=F32) for i in ls]
    dec = [jnp.exp2(blast[i]) for i in ls]

    state = list(st)
    o_ch = []
    for ci in range(GATED_SPAN):
        idx = [ci * len(q) + h for h in hs]
        o_ch.append([lax.dot_general(qe[idx[h]], state[h].astype(BF16), _NT, preferred_element_type=F32)
                     + o_intra[idx[h]] for h in hs])
        state = [state[h] * dec[idx[h]] + kv[idx[h]] for h in hs]
    o = [jnp.concatenate([o_ch[ci][h] for ci in range(GATED_SPAN)], axis=0) for h in hs]
    return o, state


def _gated_body(z_ref, aux_ref, nw_ref, g2_ref, tril_ref, o_ref, st_ref, *, mode, rows):
    @pl.when(pl.program_id(0) == 0)
    def _():
        st_ref[...] = jnp.zeros_like(st_ref)

    tril = tril_ref[...]
    span = GATED_SPAN * CHUNK

    def step(c, carry):
        r0 = pl.multiple_of(c * span, span)
        rs = pl.ds(r0, span)
        hs = range(4)
        sls = [slice(h * HEAD_PAD, (h + 1) * HEAD_PAD) for h in hs]
        zq = [z_ref[rs, h * HEAD_PAD:(h + 1) * HEAD_PAD] for h in hs]
        zk = [z_ref[rs, BRANCH_W + h * HEAD_PAD:BRANCH_W + (h + 1) * HEAD_PAD] for h in hs]
        v = [z_ref[rs, 2 * BRANCH_W + h * HEAD_PAD:2 * BRANCH_W + (h + 1) * HEAD_PAD] for h in hs]
        if mode == "hgrn2":
            q = [_silu(zq[h]) for h in hs]
            k = [(1.0 - aux_ref[0:1, sls[h]]) * _sigmoid(-zk[h]) for h in hs]
            g = [jnp.log1p(-k[h]) for h in hs]
        else:
            q = [zq[h] * (GLA_DK ** -0.5) for h in hs]
            k = zk
            la = _dot3(z_ref[rs, 4 * BRANCH_W:4 * BRANCH_W + LANES], g2_ref[...]) + aux_ref[...]
            g = [_log_sigmoid(la[:, sls[h]]) * (1.0 / GLA_TAU) for h in hs]
        o, st_new = _gated_span(q, k, v, g, [st_ref[h] for h in hs], tril)
        for h in hs:
            st_ref[h] = st_new[h]
            gate = z_ref[rs, 3 * BRANCH_W + h * HEAD_PAD:3 * BRANCH_W + (h + 1) * HEAD_PAD]
            on = o[h] * lax.rsqrt(jnp.mean(o[h] * o[h], axis=-1, keepdims=True) + LN_EPS)
            o_ref[rs, sls[h]] = (on * nw_ref[0:1, sls[h]] * _silu(gate)).astype(o_ref.dtype)
        return carry

    lax.fori_loop(0, rows // span, step, 0)


def _gated_mixer(z, aux, norm_w, g2, mode, rows=256):
    t, wz = z.shape
    rows = min(rows, t)
    span = GATED_SPAN * CHUNK
    pos = np.arange(span)
    tril = jnp.asarray(((pos[:, None] >= pos[None, :]) & (pos[:, None] // CHUNK == pos[None, :] // CHUNK))
                       .astype(np.float32), dtype=BF16)
    return pl.pallas_call(
        functools.partial(_gated_body, mode=mode, rows=rows),
        grid=(t // rows,),
        in_specs=[pl.BlockSpec((rows, wz), lambda i: (i, 0)),
                  pl.BlockSpec((1, BRANCH_W), lambda i: (0, 0)),
                  pl.BlockSpec((1, BRANCH_W), lambda i: (0, 0)),
                  pl.BlockSpec(g2.shape, lambda i: (0, 0)),
                  pl.BlockSpec((span, span), lambda i: (0, 0))],
        out_specs=pl.BlockSpec((rows, BRANCH_W), lambda i: (i, 0)),
        out_shape=jax.ShapeDtypeStruct((t, BRANCH_W), BF16),
        scratch_shapes=[pltpu.VMEM((4, HEAD_PAD, HEAD_PAD), F32)],
        compiler_params=_cparams(("arbitrary",)),
        name="gated_" + mode,
    )(z, aux, norm_w, g2, tril)


def _rwkv_body(z_ref, mu_ref, vec_ref, w2_ref, a2_ref, g2_ref, tril_ref, ones2_ref, o_ref,
               prev_ref, st_ref, *, rows):
    @pl.when(pl.program_id(0) == 0)
    def _():
        prev_ref[...] = jnp.zeros_like(prev_ref)
        st_ref[...] = jnp.zeros_like(st_ref)

    tril = tril_ref[...]
    ones2 = ones2_ref[...]
    row128 = lax.broadcasted_iota(I32, (2 * CHUNK, 2 * CHUNK), 0)
    col128 = lax.broadcasted_iota(I32, (2 * CHUNK, 2 * CHUNK), 1)
    rt = jnp.where(row128 >= CHUNK, row128 - CHUNK, row128)
    ct = jnp.where(col128 >= CHUNK, col128 - CHUNK, col128)
    strict = rt > ct
    incl = rt >= ct
    eye = row128 == col128
    head0 = lax.broadcasted_iota(I32, (1, LANES), 1) < RW_DIM
    span = RW_SPAN * CHUNK
    first_row = lax.broadcasted_iota(I32, (span, 1), 0) == 0

    def stack(x):
        return jnp.concatenate([jnp.where(head0, x, 0.0), jnp.where(head0, 0.0, x)], axis=0)

    def step(c, carry):
        r0 = pl.multiple_of(c * span, span)
        rs = pl.ds(r0, span)
        z = z_ref[rs, :]
        zprev = jnp.where(first_row, prev_ref[...], pltpu.roll(z, 1, axis=0))
        prev_ref[...] = z[span - 1:span, :]
        zs = z + (zprev - z) * mu_ref[...]
        lw = zs[:, 3 * BRANCH_W:3 * BRANCH_W + LANES]
        la = zs[:, 3 * BRANCH_W + LANES:3 * BRANCH_W + 2 * LANES]
        lg = zs[:, 3 * BRANCH_W + 2 * LANES:3 * BRANCH_W + 3 * LANES]
        wl = -(vec_ref[0:1, :] + _dot3(jnp.tanh(lw), w2_ref[...]))
        w_raw = -(jnp.maximum(wl, 0.0) + jnp.log1p(jnp.exp(-jnp.abs(wl)))) - 0.5
        logw_all = -jnp.exp(w_raw)
        iclr_all = _sigmoid(vec_ref[1:2, :] + _dot3(la, a2_ref[...]))
        gate_all = _dot3(_sigmoid(lg), g2_ref[...])
        cum_all = _ldot2(tril, logw_all)

        pairs = range(RW_PAIRS)
        sls = [slice(p * LANES, (p + 1) * LANES) for p in pairs]
        dotf = functools.partial(jnp.dot, preferred_element_type=F32)
        c2 = 2 * CHUNK
        r_f = [zs[:, p * LANES:(p + 1) * LANES] for p in pairs]
        k_f = [zs[:, BRANCH_W + p * LANES:BRANCH_W + (p + 1) * LANES] for p in pairs]
        v_f = [zs[:, 2 * BRANCH_W + p * LANES:2 * BRANCH_W + (p + 1) * LANES] for p in pairs]
        cum_f = [cum_all[:, s] for s in sls]
        iclr_f = [iclr_all[:, s] for s in sls]
        kkr = [k_f[p] * vec_ref[2:3, sls[p]] for p in pairs]
        ssq = [_dot2(kkr[p] * kkr[p], ones2) for p in pairs]
        kk_f = [kkr[p] / jnp.maximum(jnp.sqrt(ssq[p]), 1e-12) for p in pairs]
        kmod_f = [k_f[p] * (1.0 + (iclr_f[p] - 1.0) * vec_ref[3:4, sls[p]]) for p in pairs]
        bb_f = [kk_f[p] * iclr_f[p] for p in pairs]
        eneg_f = [jnp.exp(-cum_f[p]) for p in pairs]
        a_f = [-kk_f[p] * jnp.exp(cum_f[p] - logw_all[:, sls[p]]) for p in pairs]
        rd_f = [r_f[p] * jnp.exp(cum_f[p]) for p in pairs]
        bn_f = [bb_f[p] * eneg_f[p] for p in pairs]
        kn_f = [kmod_f[p] * eneg_f[p] for p in pairs]

        lanes = [(ci, p) for ci in range(RW_SPAN) for p in pairs]
        ls = range(len(lanes))
        cut = lambda xs: [xs[p][ci * CHUNK:(ci + 1) * CHUNK] for ci, p in lanes]
        cum, bb, kmod = cut(cum_f), cut(bb_f), cut(kmod_f)
        clast = [cum[i][CHUNK - 1:CHUNK, :] for i in ls]
        e_end = [jnp.exp(clast[i] - cum[i]) for i in ls]
        a_s = [stack(x) for x in cut(a_f)]
        r_s = [stack(x) for x in cut(rd_f)]
        b_s = [stack(x) for x in cut(bn_f)]
        k_s = [stack(x) for x in cut(kn_f)]
        bh_s = [stack(bb[i] * e_end[i]).astype(BF16) for i in ls]
        kh_s = [stack(kmod[i] * e_end[i]).astype(BF16) for i in ls]
        v_sb = [stack(x).astype(BF16) for x in cut(v_f)]

        left = [jnp.concatenate([a_s[i], r_s[i]], axis=0).astype(BF16) for i in ls]
        right = [jnp.concatenate([b_s[i], k_s[i]], axis=0).astype(BF16) for i in ls]
        gram = [lax.dot_general(left[i], right[i], _NT, preferred_element_type=F32) for i in ls]
        n_ab = [jnp.where(strict, gram[i][:c2, :c2], 0.0) for i in ls]
        a_ak = [jnp.where(strict, gram[i][:c2, c2:], 0.0).astype(BF16) for i in ls]
        g_bk = [jnp.concatenate([jnp.where(incl, gram[i][c2:, :c2], 0.0),
                                 jnp.where(incl, gram[i][c2:, c2:], 0.0)], axis=1).astype(BF16) for i in ls]

        x = [jnp.where(eye, 1.0, 0.0) + n_ab[i] for i in ls]
        pw = n_ab
        for _ in range(5):
            pwb = [pw[i].astype(BF16) for i in ls]
            pw = [dotf(pwb[i], pwb[i]) for i in ls]
            x = [x[i] + dotf(pw[i].astype(BF16), x[i].astype(BF16)) for i in ls]

        av = [dotf(a_ak[i], v_sb[i]) for i in ls]
        pqb = [dotf(x[i].astype(BF16), jnp.concatenate([a_s[i], av[i]], axis=1).astype(BF16)).astype(BF16)
               for i in ls]
        zero_blk = jnp.zeros((c2, LANES), BF16)
        top = [dotf(g_bk[i], jnp.concatenate([pqb[i], jnp.concatenate([zero_blk, v_sb[i]], axis=1)], axis=0))
               for i in ls]
        bot = [lax.dot_general(bh_s[i], pqb[i], _TN, preferred_element_type=F32) for i in ls]
        kv = [lax.dot_general(kh_s[i], v_sb[i], _TN, preferred_element_type=F32) for i in ls]
        r2 = [(r_s[i] + top[i][:, :LANES]).astype(BF16) for i in ls]
        tm = [(bot[i][:, :LANES] + jnp.where(eye, jnp.exp(clast[i]), 0.0)).astype(BF16) for i in ls]

        state = [st_ref[p] for p in pairs]
        o_ch = []
        for ci in range(RW_SPAN):
            idx = [ci * RW_PAIRS + p for p in pairs]
            s_old = [state[p].astype(BF16) for p in pairs]
            o_st = [dotf(r2[idx[p]], s_old[p]) + top[idx[p]][:, LANES:] for p in pairs]
            state = [dotf(tm[idx[p]], s_old[p]) + bot[idx[p]][:, LANES:] + kv[idx[p]] for p in pairs]
            o_ch.append([o_st[p][:CHUNK] + o_st[p][CHUNK:] for p in pairs])
        for p in pairs:
            st_ref[p] = state[p]
        o = [jnp.concatenate([o_ch[ci][p] for ci in range(RW_SPAN)], axis=0) for p in pairs]

        inv_n = 1.0 / RW_DIM
        mean = [_dot2(o[p], ones2) * inv_n for p in pairs]
        dlt = [o[p] - mean[p] for p in pairs]
        var = [_dot2(dlt[p] * dlt[p], ones2) * inv_n for p in pairs]
        bonus = [_dot2(r_f[p] * kmod_f[p] * vec_ref[4:5, sls[p]], ones2) * v_f[p] for p in pairs]
        for p in pairs:
            on = dlt[p] * lax.rsqrt(var[p] + RW_GN_EPS) * vec_ref[5:6, sls[p]] + vec_ref[6:7, sls[p]]
            o_ref[rs, sls[p]] = ((on + bonus[p]) * gate_all[:, sls[p]]).astype(o_ref.dtype)
        return carry

    lax.fori_loop(0, rows // span, step, 0)


def _rwkv_mixer(z, mu, vecs, w2, a2, g2, rows=256):
    t = z.shape[0]
    rows = min(rows, t)
    span = RW_SPAN * CHUNK
    pos = np.arange(span)
    tril = jnp.asarray(((pos[:, None] >= pos[None, :]) & (pos[:, None] // CHUNK == pos[None, :] // CHUNK))
                       .astype(np.float32), dtype=BF16)
    hid = np.arange(LANES) // RW_DIM
    ones2 = jnp.asarray((hid[:, None] == hid[None, :]).astype(np.float32), dtype=BF16)
    const = lambda i: (0, 0)
    return pl.pallas_call(
        functools.partial(_rwkv_body, rows=rows),
        grid=(t // rows,),
        in_specs=[pl.BlockSpec((rows, RW_Z), lambda i: (i, 0)),
                  pl.BlockSpec((1, RW_Z), const),
                  pl.BlockSpec((8, BRANCH_W), const),
                  pl.BlockSpec((LANES, BRANCH_W), const),
                  pl.BlockSpec((LANES, BRANCH_W), const),
                  pl.BlockSpec((LANES, BRANCH_W), const),
                  pl.BlockSpec((span, span), const),
                  pl.BlockSpec((LANES, LANES), const)],
        out_specs=pl.BlockSpec((rows, BRANCH_W), lambda i: (i, 0)),
        out_shape=jax.ShapeDtypeStruct((t, BRANCH_W), BF16),
        scratch_shapes=[pltpu.VMEM((1, RW_Z), F32), pltpu.VMEM((RW_PAIRS, LANES, LANES), F32)],
        compiler_params=_cparams(("arbitrary",)),
        name="rwkv7",
    )(z, mu, vecs, w2, a2, g2, tril, ones2)


def _sb_body(q_ref, k_ref, v_ref, upper_ref, o_ref, acc_ref, car_ref):
    i = pl.program_id(1)
    q = q_ref[...]
    upper = upper_ref[...]
    acc_ref[...] = jnp.zeros_like(acc_ref)
    car_ref[...] = jnp.zeros_like(car_ref)
    parts = range(SB_PARTS)
    rows = SB_BLOCK // SB_PARTS
    cut = lambda a, p: a[p * rows:(p + 1) * rows]
    dotf = functools.partial(jnp.dot, preferred_element_type=F32)
    qpos = [i * SB_BLOCK + p * rows + lax.broadcasted_iota(I32, (rows, SB_BLOCK), 0) for p in parts]
    kidx = lax.broadcasted_iota(I32, (rows, SB_BLOCK), 1)

    def body(state):
        j, _ = state
        k0 = pl.multiple_of(j * SB_BLOCK, SB_BLOCK)
        kj = k_ref[pl.ds(k0, SB_BLOCK), :]
        vj = v_ref[pl.ds(k0, SB_BLOCK), :]
        z = [lax.dot_general(cut(q, p), kj, _NT, preferred_element_type=F32) for p in parts]
        strict = [(kidx + j * SB_BLOCK) < qpos[p] for p in parts]
        lk = [jnp.where(strict[p], jnp.minimum(-z[p], 0.0) - jnp.log1p(jnp.exp(-jnp.abs(z[p]))), 0.0)
              for p in parts]
        hi = [lk[p].astype(BF16) for p in parts]
        lo = [(lk[p] - hi[p].astype(F32)).astype(BF16) for p in parts]
        later = [dotf(hi[p], upper) + dotf(lo[p], upper) for p in parts]
        car = [car_ref[p] for p in parts]
        w = [jnp.where(strict[p], jnp.exp(z[p] + lk[p] + later[p] + car[p]), 0.0).astype(BF16) for p in parts]
        pv = [dotf(w[p], vj) for p in parts]
        top = None
        for p in parts:
            acc_ref[p] += pv[p]
            c = car[p] + jnp.sum(lk[p], axis=-1, keepdims=True)
            car_ref[p] = c
            top = jnp.max(c) if top is None else jnp.maximum(top, jnp.max(c))
        return j - 1, top

    def cond(state):
        j, top = state
        return jnp.logical_and(j >= 0, top > -SB_SKIP)

    lax.while_loop(cond, body, (i, jnp.float32(0.0)))
    for p in parts:
        o_ref[p * rows:(p + 1) * rows, :] = acc_ref[p].astype(o_ref.dtype)


def _sb_attention(z):
    t = z.shape[0]
    heads = BRANCH_W // SB_DIM
    upper = jnp.asarray(np.triu(np.ones((SB_BLOCK, SB_BLOCK), np.float32), 1).T, dtype=BF16)
    return pl.pallas_call(
        _sb_body,
        grid=(heads, t // SB_BLOCK),
        in_specs=[pl.BlockSpec((SB_BLOCK, SB_DIM), lambda h, i: (i, h)),
                  pl.BlockSpec((t, SB_DIM), lambda h, i: (0, heads + h)),
                  pl.BlockSpec((t, SB_DIM), lambda h, i: (0, 2 * heads + h)),
                  pl.BlockSpec((SB_BLOCK, SB_BLOCK), lambda h, i: (0, 0))],
        out_specs=pl.BlockSpec((SB_BLOCK, SB_DIM), lambda h, i: (i, h)),
        out_shape=jax.ShapeDtypeStruct((t, heads * SB_DIM), BF16),
        scratch_shapes=[pltpu.VMEM((SB_PARTS, SB_BLOCK // SB_PARTS, SB_DIM), F32),
                        pltpu.VMEM((SB_PARTS, SB_BLOCK // SB_PARTS, 1), F32)],
        compiler_params=_cparams(("arbitrary", "arbitrary")),
        name="stick_breaking",
    )(z, z, z, upper)


def _merge_body(b0_ref, b1_ref, b2_ref, b3_ref, g0_ref, g1_ref, g2_ref, g3_ref, wbr_ref, wout_ref, x_ref,
                lnw_ref, lnb_ref, xo_ref, xb_ref, xp_ref, acc_ref):
    n = pl.program_id(1)
    merged = None
    for g, (b_ref, g_ref) in enumerate(zip((b0_ref, b1_ref, b2_ref, b3_ref), (g0_ref, g1_ref, g2_ref, g3_ref))):
        y = jnp.dot(b_ref[...], wbr_ref[g], preferred_element_type=F32) * g_ref[...].astype(F32)
        merged = y if merged is None else merged + y
    part = jnp.dot(merged.astype(BF16), wout_ref[...], preferred_element_type=F32)

    @pl.when(n == 0)
    def _():
        acc_ref[...] = part

    @pl.when(n > 0)
    def _():
        acc_ref[...] += part

    @pl.when(n == pl.num_programs(1) - 1)
    def _():
        y = _layer_norm(DEEPNORM_ALPHA * x_ref[...] + acc_ref[...], lnw_ref[...], lnb_ref[...])
        xo_ref[...] = y
        xb_ref[...] = y.astype(BF16)
        _store_rows(xp_ref, _pack_pair(y[:, :D_MODEL // 2], y[:, D_MODEL // 2:]))


def _merge(branches, gates, w_br, w_out, x, ln_w, ln_b, tm=512):
    t = x.shape[0]
    tm = min(tm, t)
    nc = D_MODEL // MERGE_TN
    row = lambda i, n: (i, 0)
    const2 = lambda i, n: (0, 0)
    gate_spec = lambda g: pl.BlockSpec((tm, MERGE_TN), lambda i, n: (i, g * nc + n))
    return pl.pallas_call(
        _merge_body,
        grid=(t // tm, nc),
        in_specs=[pl.BlockSpec((tm, BRANCH_W), row)] * 4 + [gate_spec(g) for g in range(N_BRANCHES)] + [
            pl.BlockSpec((N_BRANCHES, BRANCH_W, MERGE_TN), lambda i, n: (0, 0, n)),
            pl.BlockSpec((MERGE_TN, D_MODEL), lambda i, n: (n, 0)),
            pl.BlockSpec((tm, D_MODEL), row),
            pl.BlockSpec((1, D_MODEL), const2),
            pl.BlockSpec((1, D_MODEL), const2)],
        out_specs=[pl.BlockSpec((tm, D_MODEL), row), pl.BlockSpec((tm, D_MODEL), row),
                   pl.BlockSpec((tm * ROW_TILES, LANES), row)],
        out_shape=[jax.ShapeDtypeStruct((t, D_MODEL), F32), jax.ShapeDtypeStruct((t, D_MODEL), BF16),
                   jax.ShapeDtypeStruct((t * ROW_TILES, LANES), U32)],
        scratch_shapes=[pltpu.VMEM((tm, D_MODEL), F32)],
        compiler_params=_cparams(("parallel", "arbitrary"), 56),
        name="merge_ln",
    )(*branches, gates, gates, gates, gates, w_br, w_out, x, ln_w, ln_b)


def _first_index(hit, idx, size, axis):
    return jnp.min(jnp.where(hit, idx, size), axis=axis, keepdims=True)


def _router_body(x_ref, wr_ref, bias_ref, before_ref, idx_ref, w_ref, rank_ref, cnt_ref, run_ref, *, tm):
    @pl.when(pl.program_id(0) == 0)
    def _():
        run_ref[...] = jnp.zeros_like(run_ref)

    per = N_EXPERTS // N_GROUPS
    w_hi, w_lo = _split(wr_ref[...])
    x_hi, x_lo = _split(x_ref[...])
    ntf = functools.partial(lax.dot_general, dimension_numbers=_NT, preferred_element_type=F32)
    logits = ntf(w_hi, x_hi) + ntf(w_hi, x_lo) + ntf(w_lo, x_hi)
    scores = _sigmoid(logits)
    biased = scores + bias_ref[:, 0:1]
    g3 = biased.reshape(N_GROUPS, per, tm)
    pos = lax.broadcasted_iota(I32, (N_GROUPS, per, tm), 1)
    m1 = jnp.max(g3, axis=1, keepdims=True)
    f1 = _first_index(g3 == m1, pos, per, 1)
    m2 = jnp.max(jnp.where(pos == f1, -jnp.inf, g3), axis=1, keepdims=True)
    gscore = (m1 + m2).reshape(N_GROUPS, tm)

    gpos = lax.broadcasted_iota(I32, (N_GROUPS, tm), 0)
    chosen = jnp.zeros((N_GROUPS, tm), F32)
    cur = gscore
    for _ in range(TOPK_GROUPS):
        m = jnp.max(cur, axis=0, keepdims=True)
        pick = gpos == _first_index(cur == m, gpos, N_GROUPS, 0)
        chosen = jnp.where(pick, 1.0, chosen)
        cur = jnp.where(pick, -jnp.inf, cur)
    ok = jnp.broadcast_to(chosen.reshape(N_GROUPS, 1, tm), (N_GROUPS, per, tm)).reshape(N_EXPERTS, tm)

    epos = lax.broadcasted_iota(I32, (N_EXPERTS, tm), 0)
    cur = jnp.where(ok > 0.5, biased, -jnp.inf)
    picks, idx_rows, w_rows = [], [], []
    member = jnp.zeros((N_EXPERTS, tm), F32)
    for _ in range(TOP_K):
        m = jnp.max(cur, axis=0, keepdims=True)
        f = _first_index(cur == m, epos, N_EXPERTS, 0)
        pick = epos == f
        picks.append(pick)
        idx_rows.append(f)
        w_rows.append(jnp.sum(jnp.where(pick, scores, 0.0), axis=0, keepdims=True))
        member = jnp.where(pick, 1.0, member)
        cur = jnp.where(pick, -jnp.inf, cur)
    w_sel = jnp.concatenate(w_rows, axis=0)
    w_ref[...] = ROUTED_SCALE * w_sel / jnp.sum(w_sel, axis=0, keepdims=True)
    idx_ref[...] = jnp.concatenate(idx_rows, axis=0)

    seen = jnp.dot(member.astype(BF16), before_ref[...], preferred_element_type=F32) + run_ref[:, 0:1]
    rank_rows = [jnp.sum(jnp.where(pk, seen, 0.0), axis=0, keepdims=True) for pk in picks]
    rank_ref[...] = jnp.concatenate(rank_rows, axis=0).astype(I32)
    run_ref[...] = run_ref[...] + jnp.sum(member, axis=1, keepdims=True)
    cnt_ref[...] = run_ref[...]


def _router(x, w_router_t, bias, tm=512):
    t = x.shape[0]
    tm = min(tm, t)
    before = jnp.asarray(np.triu(np.ones((tm, tm), np.float32), 1), dtype=BF16)
    slot = lambda i: (0, i)
    const = lambda i: (0, 0)
    return pl.pallas_call(
        functools.partial(_router_body, tm=tm),
        grid=(t // tm,),
        in_specs=[pl.BlockSpec((tm, D_MODEL), lambda i: (i, 0)),
                  pl.BlockSpec((N_EXPERTS, D_MODEL), const),
                  pl.BlockSpec((N_EXPERTS, LANES), const),
                  pl.BlockSpec((tm, tm), const)],
        out_specs=[pl.BlockSpec((TOP_K, tm), slot), pl.BlockSpec((TOP_K, tm), slot),
                   pl.BlockSpec((TOP_K, tm), slot), pl.BlockSpec((N_EXPERTS, LANES), const)],
        out_shape=[jax.ShapeDtypeStruct((TOP_K, t), I32), jax.ShapeDtypeStruct((TOP_K, t), F32),
                   jax.ShapeDtypeStruct((TOP_K, t), I32), jax.ShapeDtypeStruct((N_EXPERTS, LANES), F32)],
        scratch_shapes=[pltpu.VMEM((N_EXPERTS, LANES), F32)],
        compiler_params=_cparams(("arbitrary",)),
        name="router",
    )(x, w_router_t, bias, before)


def _dest_body(start_ref, idx_ref, rank_ref, dest_ref):
    idx = idx_ref[...]
    base = jnp.zeros(idx.shape, I32)
    for e in range(N_EXPERTS):
        base = jnp.where(idx == e, start_ref[e], base)
    dest_ref[...] = base + rank_ref[...]


def _dest_rows(pad_start, idx, rank, tm=2048):
    t = idx.shape[1]
    tm = min(tm, t)
    slot = lambda i, s: (0, i)
    return pl.pallas_call(
        _dest_body,
        grid_spec=pltpu.PrefetchScalarGridSpec(
            num_scalar_prefetch=1, grid=(t // tm,),
            in_specs=[pl.BlockSpec((TOP_K, tm), slot), pl.BlockSpec((TOP_K, tm), slot)],
            out_specs=pl.BlockSpec((TOP_K, tm), slot)),
        out_shape=jax.ShapeDtypeStruct((TOP_K, t), I32),
        compiler_params=_cparams(("parallel",)),
        name="dest_rows",
    )(pad_start, idx, rank)


def _dispatch_body(zb_ref, dest_ref, x_ref, xs_hbm, zero_ref, sem, zsem, *, tm, n_fill):
    @pl.when(pl.program_id(0) == 0)
    def _():
        zero_ref[...] = jnp.zeros_like(zero_ref)

        def fill(i, carry):
            @pl.when(zb_ref[i] >= 0)
            def _():
                r0 = pl.multiple_of(zb_ref[i] * (ROW_BLOCK * ROW_TILES), ROW_BLOCK * ROW_TILES)
                pltpu.make_async_copy(zero_ref, xs_hbm.at[pl.ds(r0, ROW_BLOCK * ROW_TILES)], zsem).start()
            return carry

        def drain(i, carry):
            @pl.when(zb_ref[i] >= 0)
            def _():
                pltpu.make_async_copy(zero_ref, xs_hbm.at[pl.ds(0, ROW_BLOCK * ROW_TILES)], zsem).wait()
            return carry

        lax.fori_loop(0, n_fill, fill, 0)
        lax.fori_loop(0, n_fill, drain, 0)

    def row(t, carry):
        for k in range(TOP_K):
            pltpu.make_async_copy(_row(x_ref, t), _row(xs_hbm, dest_ref[k, t]), sem).start(
                priority=k % 2)
        return carry

    lax.fori_loop(0, tm, row, 0)
    for _ in range(TOP_K):
        pltpu.make_async_copy(x_ref, xs_hbm.at[pl.ds(0, tm * ROW_TILES)], sem).wait()


def _dispatch(zero_blocks, dest, x_packed, n_rows, tm=256):
    t = x_packed.shape[0] // ROW_TILES
    tm = min(tm, t)
    return pl.pallas_call(
        functools.partial(_dispatch_body, tm=tm, n_fill=zero_blocks.shape[0]),
        grid_spec=pltpu.PrefetchScalarGridSpec(
            num_scalar_prefetch=1, grid=(t // tm,),
            in_specs=[pl.BlockSpec((TOP_K, tm), lambda i, zb: (0, i), memory_space=pltpu.SMEM),
                      pl.BlockSpec((tm * ROW_TILES, LANES), lambda i, zb: (i, 0))],
            out_specs=pl.BlockSpec(memory_space=pl.ANY),
            scratch_shapes=[pltpu.VMEM((ROW_BLOCK * ROW_TILES, LANES), U32), pltpu.SemaphoreType.DMA(()),
                            pltpu.SemaphoreType.DMA(())]),
        out_shape=jax.ShapeDtypeStruct((n_rows * ROW_TILES, LANES), U32),
        compiler_params=_cparams(("arbitrary",)),
        name="dispatch",
    )(zero_blocks, dest, x_packed)


def _experts_body(be_ref, nu_ref, nxt_ref, slot_ref, xs_ref, wg_hbm, wu_hbm, wd_hbm, ys_ref,
                  wgf_ref, wuf_ref, wdf_ref, wgb_ref, wub_ref, wdb_ref, sem, *, layer):
    def weight_copies(e, s):
        return (pltpu.make_async_copy(wg_hbm.at[layer, e], wgf_ref.at[s], sem.at[s]),
                pltpu.make_async_copy(wu_hbm.at[layer, e], wuf_ref.at[s], sem.at[s]),
                pltpu.make_async_copy(wd_hbm.at[layer, e], wdf_ref.at[s], sem.at[s]))

    @pl.when(pl.program_id(0) == 0)
    def _():
        for c in weight_copies(be_ref[0], slot_ref[0]):
            c.start()

    def block(b, xs_blk, ys_blk):
        used = b < nu_ref[0]
        new_expert = jnp.logical_or(b == 0, be_ref[b] != be_ref[jnp.maximum(b - 1, 0)])

        @pl.when(jnp.logical_and(used, new_expert))
        def _():
            s = slot_ref[b]
            for c in weight_copies(be_ref[b], s):
                c.wait()

            @pl.when(nxt_ref[b] >= 0)
            def _():
                for c in weight_copies(nxt_ref[b], 1 - s):
                    c.start()

            wgb_ref[...] = wgf_ref[s].astype(BF16)
            wub_ref[...] = wuf_ref[s].astype(BF16)
            wdb_ref[...] = wdf_ref[s].astype(BF16)

        @pl.when(used)
        def _():
            half = D_MODEL // 2
            lo, hi = _unpack_pair(_load_rows(xs_blk))
            lo = lo.astype(BF16)
            hi = hi.astype(BF16)
            gate = (jnp.dot(lo, wgb_ref[:half, :], preferred_element_type=F32)
                    + jnp.dot(hi, wgb_ref[half:, :], preferred_element_type=F32))
            up = (jnp.dot(lo, wub_ref[:half, :], preferred_element_type=F32)
                  + jnp.dot(hi, wub_ref[half:, :], preferred_element_type=F32))
            h = (_silu(gate) * up).astype(BF16)
            y = jnp.dot(h, wdb_ref[...], preferred_element_type=F32)
            _store_rows(ys_blk, _pack_pair(y[:, :half], y[:, half:]))

        @pl.when(jnp.logical_not(used))
        def _():
            ys_blk[...] = jnp.zeros_like(ys_blk)

    rows = ROW_BLOCK * ROW_TILES
    for sub in range(EXPERT_STEP_BLOCKS):
        block(pl.program_id(0) * EXPERT_STEP_BLOCKS + sub,
              xs_ref.at[pl.ds(sub * rows, rows)], ys_ref.at[pl.ds(sub * rows, rows)])


def _experts(block_expert, n_used, next_expert, slot, xs, wg, wu, wd, layer):
    n_rows = xs.shape[0] // ROW_TILES
    n_steps = n_rows // (ROW_BLOCK * EXPERT_STEP_BLOCKS)
    step_rows = EXPERT_STEP_BLOCKS * ROW_BLOCK * ROW_TILES
    blk = lambda i, be, nu, nx, sl: (jnp.minimum(i, (nu[0] - 1) // EXPERT_STEP_BLOCKS), 0)
    out_blk = lambda i, be, nu, nx, sl: (i, 0)
    hbm = pl.BlockSpec(memory_space=pl.ANY)
    return pl.pallas_call(
        functools.partial(_experts_body, layer=layer),
        grid_spec=pltpu.PrefetchScalarGridSpec(
            num_scalar_prefetch=4, grid=(n_steps,),
            in_specs=[pl.BlockSpec((step_rows, LANES), blk), hbm, hbm, hbm],
            out_specs=pl.BlockSpec((step_rows, LANES), out_blk),
            scratch_shapes=[pltpu.VMEM((2, D_MODEL, EXPERT_W), F32), pltpu.VMEM((2, D_MODEL, EXPERT_W), F32),
                            pltpu.VMEM((2, EXPERT_W, D_MODEL), F32),
                            pltpu.VMEM((D_MODEL, EXPERT_W), BF16), pltpu.VMEM((D_MODEL, EXPERT_W), BF16),
                            pltpu.VMEM((EXPERT_W, D_MODEL), BF16), pltpu.SemaphoreType.DMA((2,))]),
        out_shape=jax.ShapeDtypeStruct((n_rows * ROW_TILES, LANES), U32),
        compiler_params=_cparams(("arbitrary",), 56),
        name="experts",
    )(block_expert, n_used, next_expert, slot, xs, wg, wu, wd)


def _combine_body(dest_ref, dnext_ref, w_ref, x_ref, xb_ref, sg_ref, su_ref, sd_ref, lnw_ref, lnb_ref, ys_hbm,
                  xo_ref, xb_out_ref, buf_ref, lo_ref, hi_ref, sem, *, tm, n_steps):
    i = pl.program_id(0)
    cur = i % 2

    def request(d_ref, s):
        def row(t, carry):
            for k in range(TOP_K):
                pltpu.make_async_copy(_row(ys_hbm, d_ref[k, t]), _row(buf_ref.at[s, k], t), sem.at[s]).start(
                    priority=k % 2)
            return carry

        lax.fori_loop(0, tm, row, 0)

    @pl.when(i == 0)
    def _():
        request(dest_ref, 0)

    @pl.when(i + 1 < n_steps)
    def _():
        request(dnext_ref, 1 - cur)

    for k in range(TOP_K):
        pltpu.make_async_copy(ys_hbm.at[pl.ds(0, tm * ROW_TILES)], buf_ref.at[cur, k], sem.at[cur]).wait()

    xb = xb_ref[...]
    h = _silu(jnp.dot(xb, sg_ref[...], preferred_element_type=F32)) * jnp.dot(xb, su_ref[...],
                                                                           preferred_element_type=F32)
    shared = jnp.dot(h.astype(BF16), sd_ref[...], preferred_element_type=F32)

    lo_acc = hi_acc = None
    for k in range(TOP_K):
        lo, hi = _unpack_pair(buf_ref[cur, k])
        wk = w_ref[:, k:k + 1]
        lo_acc = wk * lo if lo_acc is None else lo_acc + wk * lo
        hi_acc = wk * hi if hi_acc is None else hi_acc + wk * hi
    lo_ref[...] = lo_acc
    hi_ref[...] = hi_acc
    routed = jnp.concatenate([_load_rows(lo_ref), _load_rows(hi_ref)], axis=1)
    y = _layer_norm(DEEPNORM_ALPHA * x_ref[...] + routed + shared, lnw_ref[...], lnb_ref[...])
    xo_ref[...] = y
    xb_out_ref[...] = y.astype(BF16)


def _combine(dest, w_sel, x, xb, sg, su, sd, ln_w, ln_b, ys, tm=256):
    t = x.shape[0]
    tm = min(tm, t)
    n_steps = t // tm
    row = lambda i: (i, 0)
    slot = lambda i: (0, i)
    slot_next = lambda i: (0, jnp.minimum(i + 1, n_steps - 1))
    const = lambda i: (0, 0)
    return pl.pallas_call(
        functools.partial(_combine_body, tm=tm, n_steps=n_steps),
        grid=(n_steps,),
        in_specs=[pl.BlockSpec((TOP_K, tm), slot, memory_space=pltpu.SMEM),
                  pl.BlockSpec((TOP_K, tm), slot_next, memory_space=pltpu.SMEM),
                  pl.BlockSpec((tm * ROW_TILES, TOP_K), row),
                  pl.BlockSpec((tm, D_MODEL), row),
                  pl.BlockSpec((tm, D_MODEL), row),
                  pl.BlockSpec((D_MODEL, EXPERT_W), const),
                  pl.BlockSpec((D_MODEL, EXPERT_W), const),
                  pl.BlockSpec((EXPERT_W, D_MODEL), const),
                  pl.BlockSpec((1, D_MODEL), const),
                  pl.BlockSpec((1, D_MODEL), const),
                  pl.BlockSpec(memory_space=pl.ANY)],
        out_specs=[pl.BlockSpec((tm, D_MODEL), row), pl.BlockSpec((tm, D_MODEL), row)],
        out_shape=[jax.ShapeDtypeStruct((t, D_MODEL), F32), jax.ShapeDtypeStruct((t, D_MODEL), BF16)],
        scratch_shapes=[pltpu.VMEM((2, TOP_K, tm * ROW_TILES, LANES), U32),
                        pltpu.VMEM((tm * ROW_TILES, LANES), F32), pltpu.VMEM((tm * ROW_TILES, LANES), F32),
                        pltpu.SemaphoreType.DMA((2,))],
        compiler_params=_cparams(("arbitrary",), 56),
        name="combine_ln",
    )(dest, dest, w_sel, x, xb, sg, su, sd, ln_w, ln_b, ys)


def _pad_cols(a, width):
    return jnp.pad(a, [(0, 0)] * (a.ndim - 1) + [(0, width - a.shape[-1])])


def _pad_rows(a, height):
    return jnp.pad(a, ((0, height - a.shape[0]), (0, 0)))


def _pad_heads(a, heads, dk):
    lead = a.shape[:-1]
    padded = jnp.pad(a.reshape(*lead, heads, dk), [(0, 0)] * (len(lead) + 1) + [(0, HEAD_PAD - dk)])
    return padded.reshape(*lead, heads * HEAD_PAD)


def _relayout_in_weight(w_in):
    w_rw = w_in[..., RW_OFF:SB_OFF]
    w_sb = w_in[..., SB_OFF:GLA_OFF]
    w_gla = w_in[..., GLA_OFF:GATE_OFF]
    zcol = lambda n: jnp.zeros(w_in.shape[:-1] + (n,), w_in.dtype)
    return jnp.concatenate([
        w_in[..., HG_OFF:RW_OFF],
        w_rw[..., :1536], _pad_cols(w_rw[..., 1536:1568], LANES), _pad_cols(w_rw[..., 1568:1600], LANES),
        _pad_cols(w_rw[..., 1600:1696], LANES), zcol(RW_Z - 3 * BRANCH_W - 3 * LANES),
        w_sb[..., :BRANCH_W] * (SB_DIM ** -0.5), w_sb[..., BRANCH_W:],
        _pad_heads(w_gla[..., :256], 4, GLA_DK), _pad_heads(w_gla[..., 256:512], 4, GLA_DK),
        w_gla[..., 512:1024], w_gla[..., 1040:1552], _pad_cols(w_gla[..., 1024:1040], LANES),
        zcol(GLA_Z - 4 * BRANCH_W - LANES),
        w_in[..., GATE_OFF:]], axis=-1).astype(BF16)


def _token_mixing(xb, w_all, layer, lower_bound, hg_norm_w, rw_mu, rw_w0, rw_w2, rw_a0, rw_a2, rw_g2, rw_kk,
                  rw_ka, rw_rk, rw_ln_w, rw_ln_b, gla_g2, gla_gb, gla_norm_w):
    row = lambda a: a.reshape(1, -1).astype(F32)

    z_hg = _matmul(xb, w_all, layer, ZHG_OFF, 4 * BRANCH_W, F32, tn=2 * PROJ_TN)
    o_hg = _gated_mixer(z_hg, row(lower_bound), row(hg_norm_w), jnp.zeros((8, BRANCH_W), F32), "hgrn2")

    z_rw = _matmul(xb, w_all, layer, ZRW_OFF, RW_Z, F32, tn=2 * PROJ_TN)
    mu = jnp.concatenate([rw_mu[:1536], jnp.pad(rw_mu[1536:1568], (0, 96)), jnp.pad(rw_mu[1568:1600], (0, 96)),
                          jnp.pad(rw_mu[1600:1696], (0, 32 + RW_Z - 3 * BRANCH_W - 3 * LANES))]).reshape(1, RW_Z)
    vecs = jnp.stack([rw_w0, rw_a0, rw_kk, rw_ka, rw_rk, rw_ln_w, rw_ln_b, jnp.zeros_like(rw_w0)]).astype(F32)
    o_rw = _rwkv_mixer(z_rw, mu, vecs, _pad_rows(rw_w2, LANES), _pad_rows(rw_a2, LANES), _pad_rows(rw_g2, LANES))

    o_sb = _sb_attention(_matmul(xb, w_all, layer, ZSB_OFF, 3 * BRANCH_W, BF16))

    z_gla = _matmul(xb, w_all, layer, ZGLA_OFF, GLA_Z, F32)
    g2p = _pad_rows(_pad_heads(gla_g2, 4, GLA_DK), LANES)
    gbp = _pad_heads(gla_gb.reshape(1, -1), 4, GLA_DK)
    o_gla = _gated_mixer(z_gla, gbp, row(gla_norm_w), g2p, "gla")

    gates = _matmul(xb, w_all, layer, ZGATE_OFF, N_BRANCHES * D_MODEL, BF16, act="sigmoid", tn=1024)
    return (o_hg, o_rw, o_sb, o_gla), gates


def _moe(x, xb, xp, w_router, router_bias, we_gate, we_up, we_down, layer, ws_gate, ws_up, ws_down, ln_w, ln_b):
    t = x.shape[0]
    bias = jnp.broadcast_to(router_bias.astype(F32).reshape(N_EXPERTS, 1), (N_EXPERTS, LANES))
    idx, w_sel, rank, counts = _router(x, w_router.T.astype(F32), bias)

    cnt = counts[:, 0].astype(I32)
    padded = (cnt + ROW_BLOCK - 1) // ROW_BLOCK * ROW_BLOCK
    pad_end = jnp.cumsum(padded)
    pad_start = (pad_end - padded).astype(I32)
    n_blocks = -(-(t * TOP_K // ROW_BLOCK + N_EXPERTS) // EXPERT_STEP_BLOCKS) * EXPERT_STEP_BLOCKS
    n_used = (pad_end[-1:] // ROW_BLOCK).astype(I32)
    first_row = jnp.arange(n_blocks, dtype=I32) * ROW_BLOCK
    block_expert = jnp.minimum(jnp.sum(pad_end[None, :] <= first_row[:, None], axis=1), N_EXPERTS - 1).astype(I32)

    last_block = jnp.where(padded > 0, pad_end // ROW_BLOCK - 1, -1)
    tail = n_used[0] + jnp.arange(N_EXPERTS, dtype=I32)
    zero_blocks = jnp.concatenate([last_block, jnp.where(tail < n_blocks, tail, -1)]).astype(I32)

    has_rows = cnt > 0
    eid = jnp.arange(N_EXPERTS, dtype=I32)
    later = jnp.where(has_rows[None, :] & (eid[None, :] > eid[:, None]), eid[None, :], N_EXPERTS)
    next_used = jnp.min(later, axis=1)
    next_used = jnp.where(next_used < N_EXPERTS, next_used, -1).astype(I32)
    ordinal = jnp.cumsum(has_rows.astype(I32)) - 1
    of_block = block_expert[:, None] == eid[None, :]
    next_expert = jnp.sum(jnp.where(of_block, next_used[None, :], 0), axis=1).astype(I32)
    slot = (jnp.sum(jnp.where(of_block, ordinal[None, :], 0), axis=1) % 2).astype(I32)

    dest = _dest_rows(pad_start, idx, rank)
    xs = _dispatch(zero_blocks, dest, xp, n_blocks * ROW_BLOCK)
    ys = _experts(block_expert, n_used, next_expert, slot, xs, we_gate, we_up, we_down, layer)
    w_rows = jnp.repeat(w_sel.T, ROW_TILES, axis=0)
    return _combine(dest, w_rows, x, xb, ws_gate.astype(BF16), ws_up.astype(BF16), ws_down.astype(BF16),
                    ln_w.reshape(1, -1), ln_b.reshape(1, -1), ys)


def kernel(x, w_in, hg_lb_logits, hg_norm_w, rw_mu, rw_w0, rw_w2, rw_a0, rw_a2, rw_g2, rw_kk, rw_ka, rw_rk,
           rw_ln_w, rw_ln_b, gla_g2, gla_gb, gla_norm_w, w_br, w_out, ln1_w, ln1_b, w_router, router_bias,
           we_gate, we_up, we_down, ws_gate, ws_up, ws_down, ln2_w, ln2_b):
    bsz, t, d = x.shape
    cum = jnp.cumsum(jax.nn.softmax(hg_lb_logits.astype(F32), axis=0), axis=0)
    lower_bounds = cum - cum[0:1]
    w_all = _relayout_in_weight(w_in)
    outs = []
    for bi in range(bsz):
        xf = x[bi].astype(F32)
        xb = xf.astype(BF16)
        for l in range(DEPTH):
            branches, gates = _token_mixing(
                xb, w_all, l, lower_bounds[l], hg_norm_w[l], rw_mu[l], rw_w0[l], rw_w2[l], rw_a0[l], rw_a2[l],
                rw_g2[l], rw_kk[l], rw_ka[l], rw_rk[l], rw_ln_w[l], rw_ln_b[l], gla_g2[l], gla_gb[l],
                gla_norm_w[l])
            xf, xb, xp = _merge(branches, gates, w_br[l].astype(BF16), w_out[l].astype(BF16), xf,
                                ln1_w[l].reshape(1, -1), ln1_b[l].reshape(1, -1))
            xf, xb = _moe(xf, xb, xp, w_router[l], router_bias[l], we_gate, we_up, we_down, l,
                          ws_gate[l], ws_up[l], ws_down[l], ln2_w[l], ln2_b[l])
        outs.append(xf)
    return jnp.stack(outs).astype(x.dtype)
```

```python
import functools

import jax
import jax.numpy as jnp
import numpy as np
from jax import lax
from jax.experimental import pallas as pl
from jax.experimental.pallas import tpu as pltpu

F32 = jnp.float32
BF16 = jnp.bfloat16
I32 = jnp.int32
U32 = jnp.uint32

D_MODEL = 2048
DEPTH = 2
BRANCH_W = 512
N_BRANCHES = 4
CHUNK = 64
SUB = 16
N_SUB = CHUNK // SUB
LANES = 128
HEAD_PAD = 128
RW_HEADS = 8
RW_DIM = 64
RW_PAIRS = RW_HEADS // 2
GATED_SPAN = 4
RW_SPAN = 4
GLA_DK = 64
GLA_TAU = 16.0
SB_DIM = 128
SB_BLOCK = 256
SB_PARTS = 2
SB_SKIP = 120.0
N_EXPERTS = 64
TOP_K = 8
N_GROUPS = 8
TOPK_GROUPS = 4
EXPERT_W = 512
ROUTED_SCALE = 2.5
ROW_BLOCK = 256
EXPERT_STEP_BLOCKS = 2
DEEPNORM_ALPHA = (2 * DEPTH) ** 0.25
LN_EPS = 1e-5
RW_GN_EPS = 64e-5
LOG2E = 1.4426950408889634

HG_OFF = 0
RW_OFF = 2048
SB_OFF = 3744
GLA_OFF = 5280
GATE_OFF = 6832
PROJ_TN = 512
RW_Z = 2048
GLA_Z = 2560
ZHG_OFF = 0
ZRW_OFF = ZHG_OFF + 4 * BRANCH_W
ZSB_OFF = ZRW_OFF + RW_Z
ZGLA_OFF = ZSB_OFF + 3 * BRANCH_W
ZGATE_OFF = ZGLA_OFF + GLA_Z
Z_TOTAL = ZGATE_OFF + N_BRANCHES * D_MODEL

_NT = (((1,), (1,)), ((), ()))
_TN = (((0,), (0,)), ((), ()))


def _cparams(sem, vmem_mb=48):
    return pltpu.CompilerParams(dimension_semantics=sem, vmem_limit_bytes=vmem_mb << 20)


def _sigmoid(x):
    return 1.0 / (1.0 + jnp.exp(-x))


def _log_sigmoid(x):
    return jnp.minimum(x, 0.0) - jnp.log1p(jnp.exp(-jnp.abs(x)))


def _silu(x):
    return x * _sigmoid(x)


def _layer_norm(y, w, b):
    mu = jnp.mean(y, axis=-1, keepdims=True)
    d = y - mu
    var = jnp.mean(d * d, axis=-1, keepdims=True)
    return d * lax.rsqrt(var + LN_EPS) * w + b


def _pack_pair(lo, hi):
    lo_b = lax.bitcast_convert_type(lo.astype(BF16).astype(F32), U32) >> 16
    hi_b = lax.bitcast_convert_type(hi.astype(BF16).astype(F32), U32) & jnp.uint32(0xFFFF0000)
    return lo_b | hi_b


def _unpack_pair(u):
    lo = lax.bitcast_convert_type(u << 16, F32)
    hi = lax.bitcast_convert_type(u & jnp.uint32(0xFFFF0000), F32)
    return lo, hi


ROW_TILES = D_MODEL // 2 // LANES


def _store_rows(ref, packed):
    r = packed.shape[0]
    for j in range(ROW_TILES):
        ref[pl.ds(j, r, stride=ROW_TILES), :] = packed[:, j * LANES:(j + 1) * LANES]


def _load_rows(ref):
    r = ref.shape[0] // ROW_TILES
    return jnp.concatenate([ref[pl.ds(j, r, stride=ROW_TILES), :] for j in range(ROW_TILES)], axis=1)


def _row(ref, i):
    return ref.at[pl.ds(pl.multiple_of(i * ROW_TILES, ROW_TILES), ROW_TILES)]


def _split(x):
    hi = x.astype(BF16)
    return hi, (x - hi.astype(F32)).astype(BF16)


def _dot2(x, ones_bf16):
    hi, lo = _split(x)
    return (jnp.dot(hi, ones_bf16, preferred_element_type=F32)
            + jnp.dot(lo, ones_bf16, preferred_element_type=F32))


def _ldot2(ones_bf16, x):
    hi, lo = _split(x)
    return (jnp.dot(ones_bf16, hi, preferred_element_type=F32)
            + jnp.dot(ones_bf16, lo, preferred_element_type=F32))


def _dot3(a, b):
    a_hi, a_lo = _split(a)
    b_hi, b_lo = _split(b)
    return (jnp.dot(a_hi, b_hi, preferred_element_type=F32) + jnp.dot(a_hi, b_lo, preferred_element_type=F32)
            + jnp.dot(a_lo, b_hi, preferred_element_type=F32))


def _mm_body(x_ref, w_ref, o_ref, *, act):
    acc = jnp.dot(x_ref[...], w_ref[...], preferred_element_type=F32)
    if act == "sigmoid":
        acc = _sigmoid(acc)
    o_ref[...] = acc.astype(o_ref.dtype)


def _matmul(x, w, layer, col0, n, out_dtype, act=None, tm=1024, tn=PROJ_TN):
    m, k = x.shape
    tm = min(tm, m)
    off = col0 // tn
    return pl.pallas_call(
        functools.partial(_mm_body, act=act),
        grid=(m // tm, n // tn),
        in_specs=[pl.BlockSpec((tm, k), lambda i, j: (i, 0)),
                  pl.BlockSpec((None, k, tn), lambda i, j: (layer, 0, off + j))],
        out_specs=pl.BlockSpec((tm, tn), lambda i, j: (i, j)),
        out_shape=jax.ShapeDtypeStruct((m, n), out_dtype),
        compiler_params=_cparams(("parallel", "parallel")),
        name="proj",
    )(x, w)


def _gated_span(q, k, v, g, st, tril):
    hs = range(len(q))
    lanes = [(ci, h) for ci in range(GATED_SPAN) for h in hs]
    ls = range(len(lanes))
    subs = range(N_SUB)
    dotf = functools.partial(jnp.dot, preferred_element_type=F32)
    blk = lambda x, i: x[i * SUB:(i + 1) * SUB]
    cut = lambda xs: [xs[h][ci * CHUNK:(ci + 1) * CHUNK] for ci, h in lanes]
    b_f = [_ldot2(tril, g[h] * LOG2E) for h in hs]
    qe_f = [(q[h] * jnp.exp2(b_f[h])).astype(BF16) for h in hs]
    vb_f = [v[h].astype(BF16) for h in hs]
    b, qc, kc, vb, qe = cut(b_f), cut(q), cut(k), cut(vb_f), cut(qe_f)
    blast = [b[i][CHUNK - 1:CHUNK, :] for i in ls]
    ends = [[b[i][(j + 1) * SUB - 1:(j + 1) * SUB, :] for j in subs] for i in ls]
    khat = [[blk(kc[i], j) * jnp.exp2(ends[i][j] - blk(b[i], j)) for j in subs] for i in ls]
    o_parts = [[None] * N_SUB for _ in ls]

    for j in range(N_SUB - 1):
        lo = (j + 1) * SUB
        qs = [(qc[i][lo:] * jnp.exp2(b[i][lo:] - ends[i][j])).astype(BF16) for i in ls]
        a = [lax.dot_general(qs[i], khat[i][j].astype(BF16), _NT, preferred_element_type=F32) for i in ls]
        pv = [dotf(a[i].astype(BF16), blk(vb[i], j)) for i in ls]
        for i in ls:
            for t in range(j + 1, N_SUB):
                piece = pv[i][(t - j - 1) * SUB:(t - j) * SUB]
                o_parts[i][t] = piece if o_parts[i][t] is None else o_parts[i][t] + piece

    lane = lax.broadcasted_iota(I32, (SUB, LANES), 1)
    trow = lax.broadcasted_iota(I32, (SUB, 1), 0)
    half = SUB // 2
    for t in subs:
        d_top = [jnp.zeros((half, LANES), F32) for _ in ls]
        d_bot = [jnp.zeros((half, LANES), F32) for _ in ls]
        for s in range(SUB):
            for i in ls:
                bi, qi, ki = blk(b[i], t), blk(qc[i], t), blk(kc[i], t)
                if s < half:
                    col = jnp.sum(qi * jnp.exp2(bi - bi[s:s + 1, :]) * ki[s:s + 1, :], axis=-1, keepdims=True)
                    d_top[i] = jnp.where(lane[:half] == s, col[:half], d_top[i])
                    d_bot[i] = jnp.where(lane[:half] == s, col[half:], d_bot[i])
                else:
                    col = jnp.sum(qi[half:] * jnp.exp2(bi[half:] - bi[s:s + 1, :]) * ki[s:s + 1, :],
                                  axis=-1, keepdims=True)
                    d_bot[i] = jnp.where(lane[:half] == s, col, d_bot[i])
        d = [jnp.concatenate([d_top[i], d_bot[i]], axis=0) for i in ls]
        pv = [dotf(jnp.where(lane <= trow, d[i], 0.0)[:, :SUB].astype(BF16), blk(vb[i], t)) for i in ls]
        for i in ls:
            o_parts[i][t] = pv[i] if o_parts[i][t] is None else o_parts[i][t] + pv[i]

    o_intra = [jnp.concatenate(o_parts[i], axis=0) for i in ls]
    kd = [jnp.concatenate([khat[i][j] * jnp.exp2(blast[i] - ends[i][j]) for j in subs], axis=0).astype(BF16)
          for i in ls]
    kv = [lax.dot_general(vb[i], kd[i], _TN, preferred_element_type=F32) for i in ls]
    dec = [jnp.exp2(blast[i]) for i in ls]

    state = list(st)
    o_ch = []
    for ci in range(GATED_SPAN):
        idx = [ci * len(q) + h for h in hs]
        o_ch.append([lax.dot_general(qe[idx[h]], state[h].astype(BF16), _NT, preferred_element_type=F32)
                     + o_intra[idx[h]] for h in hs])
        state = [state[h] * dec[idx[h]] + kv[idx[h]] for h in hs]
    o = [jnp.concatenate([o_ch[ci][h] for ci in range(GATED_SPAN)], axis=0) for h in hs]
    return o, state


def _gated_body(z_ref, aux_ref, nw_ref, g2_ref, tril_ref, o_ref, st_ref, *, mode, rows):
    @pl.when(pl.program_id(0) == 0)
    def _():
        st_ref[...] = jnp.zeros_like(st_ref)

    tril = tril_ref[...]
    span = GATED_SPAN * CHUNK

    def step(c, carry):
        r0 = pl.multiple_of(c * span, span)
        rs = pl.ds(r0, span)
        hs = range(4)
        sls = [slice(h * HEAD_PAD, (h + 1) * HEAD_PAD) for h in hs]
        zq = [z_ref[rs, h * HEAD_PAD:(h + 1) * HEAD_PAD] for h in hs]
        zk = [z_ref[rs, BRANCH_W + h * HEAD_PAD:BRANCH_W + (h + 1) * HEAD_PAD] for h in hs]
        v = [z_ref[rs, 2 * BRANCH_W + h * HEAD_PAD:2 * BRANCH_W + (h + 1) * HEAD_PAD] for h in hs]
        if mode == "hgrn2":
            q = [_silu(zq[h]) for h in hs]
            k = [(1.0 - aux_ref[0:1, sls[h]]) * _sigmoid(-zk[h]) for h in hs]
            g = [jnp.log1p(-k[h]) for h in hs]
        else:
            q = [zq[h] * (GLA_DK ** -0.5) for h in hs]
            k = zk
            la = _dot3(z_ref[rs, 4 * BRANCH_W:4 * BRANCH_W + LANES], g2_ref[...]) + aux_ref[...]
            g = [_log_sigmoid(la[:, sls[h]]) * (1.0 / GLA_TAU) for h in hs]
        o, st_new = _gated_span(q, k, v, g, [st_ref[h] for h in hs], tril)
        for h in hs:
            st_ref[h] = st_new[h]
            gate = z_ref[rs, 3 * BRANCH_W + h * HEAD_PAD:3 * BRANCH_W + (h + 1) * HEAD_PAD]
            on = o[h] * lax.rsqrt(jnp.mean(o[h] * o[h], axis=-1, keepdims=True) + LN_EPS)
            o_ref[rs, sls[h]] = (on * nw_ref[0:1, sls[h]] * _silu(gate)).astype(o_ref.dtype)
        return carry

    lax.fori_loop(0, rows // span, step, 0)


def _gated_mixer(z, aux, norm_w, g2, mode, rows=256):
    t, wz = z.shape
    rows = min(rows, t)
    span = GATED_SPAN * CHUNK
    pos = np.arange(span)
    tril = jnp.asarray(((pos[:, None] >= pos[None, :]) & (pos[:, None] // CHUNK == pos[None, :] // CHUNK))
                       .astype(np.float32), dtype=BF16)
    return pl.pallas_call(
        functools.partial(_gated_body, mode=mode, rows=rows),
        grid=(t // rows,),
        in_specs=[pl.BlockSpec((rows, wz), lambda i: (i, 0)),
                  pl.BlockSpec((1, BRANCH_W), lambda i: (0, 0)),
                  pl.BlockSpec((1, BRANCH_W), lambda i: (0, 0)),
                  pl.BlockSpec(g2.shape, lambda i: (0, 0)),
                  pl.BlockSpec((span, span), lambda i: (0, 0))],
        out_specs=pl.BlockSpec((rows, BRANCH_W), lambda i: (i, 0)),
        out_shape=jax.ShapeDtypeStruct((t, BRANCH_W), BF16),
        scratch_shapes=[pltpu.VMEM((4, HEAD_PAD, HEAD_PAD), F32)],
        compiler_params=_cparams(("arbitrary",)),
        name="gated_" + mode,
    )(z, aux, norm_w, g2, tril)


def _rwkv_body(z_ref, mu_ref, vec_ref, w2_ref, a2_ref, g2_ref, tril_ref, ones2_ref, o_ref,
               prev_ref, st_ref, *, rows):
    @pl.when(pl.program_id(0) == 0)
    def _():
        prev_ref[...] = jnp.zeros_like(prev_ref)
        st_ref[...] = jnp.zeros_like(st_ref)

    tril = tril_ref[...]
    ones2 = ones2_ref[...]
    row128 = lax.broadcasted_iota(I32, (2 * CHUNK, 2 * CHUNK), 0)
    col128 = lax.broadcasted_iota(I32, (2 * CHUNK, 2 * CHUNK), 1)
    rt = jnp.where(row128 >= CHUNK, row128 - CHUNK, row128)
    ct = jnp.where(col128 >= CHUNK, col128 - CHUNK, col128)
    strict = rt > ct
    incl = rt >= ct
    eye = row128 == col128
    head0 = lax.broadcasted_iota(I32, (1, LANES), 1) < RW_DIM
    span = RW_SPAN * CHUNK
    first_row = lax.broadcasted_iota(I32, (span, 1), 0) == 0

    def stack(x):
        return jnp.concatenate([jnp.where(head0, x, 0.0), jnp.where(head0, 0.0, x)], axis=0)

    def step(c, carry):
        r0 = pl.multiple_of(c * span, span)
        rs = pl.ds(r0, span)
        z = z_ref[rs, :]
        zprev = jnp.where(first_row, prev_ref[...], pltpu.roll(z, 1, axis=0))
        prev_ref[...] = z[span - 1:span, :]
        zs = z + (zprev - z) * mu_ref[...]
        lw = zs[:, 3 * BRANCH_W:3 * BRANCH_W + LANES]
        la = zs[:, 3 * BRANCH_W + LANES:3 * BRANCH_W + 2 * LANES]
        lg = zs[:, 3 * BRANCH_W + 2 * LANES:3 * BRANCH_W + 3 * LANES]
        wl = -(vec_ref[0:1, :] + _dot3(jnp.tanh(lw), w2_ref[...]))
        w_raw = -(jnp.maximum(wl, 0.0) + jnp.log1p(jnp.exp(-jnp.abs(wl)))) - 0.5
        logw_all = -jnp.exp(w_raw)
        iclr_all = _sigmoid(vec_ref[1:2, :] + _dot3(la, a2_ref[...]))
        gate_all = _dot3(_sigmoid(lg), g2_ref[...])
        cum_all = _ldot2(tril, logw_all)

        pairs = range(RW_PAIRS)
        sls = [slice(p * LANES, (p + 1) * LANES) for p in pairs]
        dotf = functools.partial(jnp.dot, preferred_element_type=F32)
        c2 = 2 * CHUNK
        r_f = [zs[:, p * LANES:(p + 1) * LANES] for p in pairs]
        k_f = [zs[:, BRANCH_W + p * LANES:BRANCH_W + (p + 1) * LANES] for p in pairs]
        v_f = [zs[:, 2 * BRANCH_W + p * LANES:2 * BRANCH_W + (p + 1) * LANES] for p in pairs]
        cum_f = [cum_all[:, s] for s in sls]
        iclr_f = [iclr_all[:, s] for s in sls]
        kkr = [k_f[p] * vec_ref[2:3, sls[p]] for p in pairs]
        ssq = [_dot2(kkr[p] * kkr[p], ones2) for p in pairs]
        kk_f = [kkr[p] / jnp.maximum(jnp.sqrt(ssq[p]), 1e-12) for p in pairs]
        kmod_f = [k_f[p] * (1.0 + (iclr_f[p] - 1.0) * vec_ref[3:4, sls[p]]) for p in pairs]
        bb_f = [kk_f[p] * iclr_f[p] for p in pairs]
        eneg_f = [jnp.exp(-cum_f[p]) for p in pairs]
        a_f = [-kk_f[p] * jnp.exp(cum_f[p] - logw_all[:, sls[p]]) for p in pairs]
        rd_f = [r_f[p] * jnp.exp(cum_f[p]) for p in pairs]
        bn_f = [bb_f[p] * eneg_f[p] for p in pairs]
        kn_f = [kmod_f[p] * eneg_f[p] for p in pairs]

        lanes = [(ci, p) for ci in range(RW_SPAN) for p in pairs]
        ls = range(len(lanes))
        cut = lambda xs: [xs[p][ci * CHUNK:(ci + 1) * CHUNK] for ci, p in lanes]
        cum, bb, kmod = cut(cum_f), cut(bb_f), cut(kmod_f)
        clast = [cum[i][CHUNK - 1:CHUNK, :] for i in ls]
        e_end = [jnp.exp(clast[i] - cum[i]) for i in ls]
        a_s = [stack(x) for x in cut(a_f)]
        r_s = [stack(x) for x in cut(rd_f)]
        b_s = [stack(x) for x in cut(bn_f)]
        k_s = [stack(x) for x in cut(kn_f)]
        bh_s = [stack(bb[i] * e_end[i]).astype(BF16) for i in ls]
        kh_s = [stack(kmod[i] * e_end[i]).astype(BF16) for i in ls]
        v_sb = [stack(x).astype(BF16) for x in cut(v_f)]

        left = [jnp.concatenate([a_s[i], r_s[i]], axis=0).astype(BF16) for i in ls]
        right = [jnp.concatenate([b_s[i], k_s[i]], axis=0).astype(BF16) for i in ls]
        gram = [lax.dot_general(left[i], right[i], _NT, preferred_element_type=F32) for i in ls]
        n_ab = [jnp.where(strict, gram[i][:c2, :c2], 0.0) for i in ls]
        a_ak = [jnp.where(strict, gram[i][:c2, c2:], 0.0).astype(BF16) for i in ls]
        g_bk = [jnp.concatenate([jnp.where(incl, gram[i][c2:, :c2], 0.0),
                                 jnp.where(incl, gram[i][c2:, c2:], 0.0)], axis=1).astype(BF16) for i in ls]

        x = [jnp.where(eye, 1.0, 0.0) + n_ab[i] for i in ls]
        pw = n_ab
        for _ in range(5):
            pwb = [pw[i].astype(BF16) for i in ls]
            pw = [dotf(pwb[i], pwb[i]) for i in ls]
            x = [x[i] + dotf(pw[i].astype(BF16), x[i].astype(BF16)) for i in ls]

        av = [dotf(a_ak[i], v_sb[i]) for i in ls]
        pqb = [dotf(x[i].astype(BF16), jnp.concatenate([a_s[i], av[i]], axis=1).astype(BF16)).astype(BF16)
               for i in ls]
        zero_blk = jnp.zeros((c2, LANES), BF16)
        top = [dotf(g_bk[i], jnp.concatenate([pqb[i], jnp.concatenate([zero_blk, v_sb[i]], axis=1)], axis=0))
               for i in ls]
        bot = [lax.dot_general(bh_s[i], pqb[i], _TN, preferred_element_type=F32) for i in ls]
        kv = [lax.dot_general(kh_s[i], v_sb[i], _TN, preferred_element_type=F32) for i in ls]
        r2 = [(r_s[i] + top[i][:, :LANES]).astype(BF16) for i in ls]
        tm = [(bot[i][:, :LANES] + jnp.where(eye, jnp.exp(clast[i]), 0.0)).astype(BF16) for i in ls]

        state = [st_ref[p] for p in pairs]
        o_ch = []
        for ci in range(RW_SPAN):
            idx = [ci * RW_PAIRS + p for p in pairs]
            s_old = [state[p].astype(BF16) for p in pairs]
            o_st = [dotf(r2[idx[p]], s_old[p]) + top[idx[p]][:, LANES:] for p in pairs]
            state = [dotf(tm[idx[p]], s_old[p]) + bot[idx[p]][:, LANES:] + kv[idx[p]] for p in pairs]
            o_ch.append([o_st[p][:CHUNK] + o_st[p][CHUNK:] for p in pairs])
        for p in pairs:
            st_ref[p] = state[p]
        o = [jnp.concatenate([o_ch[ci][p] for ci in range(RW_SPAN)], axis=0) for p in pairs]

        inv_n = 1.0 / RW_DIM
        mean = [_dot2(o[p], ones2) * inv_n for p in pairs]
        dlt = [o[p] - mean[p] for p in pairs]
        var = [_dot2(dlt[p] * dlt[p], ones2) * inv_n for p in pairs]
        bonus = [_dot2(r_f[p] * kmod_f[p] * vec_ref[4:5, sls[p]], ones2) * v_f[p] for p in pairs]
        for p in pairs:
            on = dlt[p] * lax.rsqrt(var[p] + RW_GN_EPS) * vec_ref[5:6, sls[p]] + vec_ref[6:7, sls[p]]
            o_ref[rs, sls[p]] = ((on + bonus[p]) * gate_all[:, sls[p]]).astype(o_ref.dtype)
        return carry

    lax.fori_loop(0, rows // span, step, 0)


def _rwkv_mixer(z, mu, vecs, w2, a2, g2, rows=256):
    t = z.shape[0]
    rows = min(rows, t)
    span = RW_SPAN * CHUNK
    pos = np.arange(span)
    tril = jnp.asarray(((pos[:, None] >= pos[None, :]) & (pos[:, None] // CHUNK == pos[None, :] // CHUNK))
                       .astype(np.float32), dtype=BF16)
    hid = np.arange(LANES) // RW_DIM
    ones2 = jnp.asarray((hid[:, None] == hid[None, :]).astype(np.float32), dtype=BF16)
    const = lambda i: (0, 0)
    return pl.pallas_call(
        functools.partial(_rwkv_body, rows=rows),
        grid=(t // rows,),
        in_specs=[pl.BlockSpec((rows, RW_Z), lambda i: (i, 0)),
                  pl.BlockSpec((1, RW_Z), const),
                  pl.BlockSpec((8, BRANCH_W), const),
                  pl.BlockSpec((LANES, BRANCH_W), const),
                  pl.BlockSpec((LANES, BRANCH_W), const),
                  pl.BlockSpec((LANES, BRANCH_W), const),
                  pl.BlockSpec((span, span), const),
                  pl.BlockSpec((LANES, LANES), const)],
        out_specs=pl.BlockSpec((rows, BRANCH_W), lambda i: (i, 0)),
        out_shape=jax.ShapeDtypeStruct((t, BRANCH_W), BF16),
        scratch_shapes=[pltpu.VMEM((1, RW_Z), F32), pltpu.VMEM((RW_PAIRS, LANES, LANES), F32)],
        compiler_params=_cparams(("arbitrary",)),
        name="rwkv7",
    )(z, mu, vecs, w2, a2, g2, tril, ones2)


def _sb_body(q_ref, k_ref, v_ref, upper_ref, o_ref, acc_ref, car_ref):
    i = pl.program_id(1)
    q = q_ref[...]
    upper = upper_ref[...]
    acc_ref[...] = jnp.zeros_like(acc_ref)
    car_ref[...] = jnp.zeros_like(car_ref)
    parts = range(SB_PARTS)
    rows = SB_BLOCK // SB_PARTS
    cut = lambda a, p: a[p * rows:(p + 1) * rows]
    dotf = functools.partial(jnp.dot, preferred_element_type=F32)
    qpos = [i * SB_BLOCK + p * rows + lax.broadcasted_iota(I32, (rows, SB_BLOCK), 0) for p in parts]
    kidx = lax.broadcasted_iota(I32, (rows, SB_BLOCK), 1)

    def body(state):
        j, _ = state
        k0 = pl.multiple_of(j * SB_BLOCK, SB_BLOCK)
        kj = k_ref[pl.ds(k0, SB_BLOCK), :]
        vj = v_ref[pl.ds(k0, SB_BLOCK), :]
        z = [lax.dot_general(cut(q, p), kj, _NT, preferred_element_type=F32) for p in parts]
        strict = [(kidx + j * SB_BLOCK) < qpos[p] for p in parts]
        lk = [jnp.where(strict[p], jnp.minimum(-z[p], 0.0) - jnp.log1p(jnp.exp(-jnp.abs(z[p]))), 0.0)
              for p in parts]
        hi = [lk[p].astype(BF16) for p in parts]
        lo = [(lk[p] - hi[p].astype(F32)).astype(BF16) for p in parts]
        later = [dotf(hi[p], upper) + dotf(lo[p], upper) for p in parts]
        car = [car_ref[p] for p in parts]
        w = [jnp.where(strict[p], jnp.exp(z[p] + lk[p] + later[p] + car[p]), 0.0).astype(BF16) for p in parts]
        pv = [dotf(w[p], vj) for p in parts]
        top = None
        for p in parts:
            acc_ref[p] += pv[p]
            c = car[p] + jnp.sum(lk[p], axis=-1, keepdims=True)
            car_ref[p] = c
            top = jnp.max(c) if top is None else jnp.maximum(top, jnp.max(c))
        return j - 1, top

    def cond(state):
        j, top = state
        return jnp.logical_and(j >= 0, top > -SB_SKIP)

    lax.while_loop(cond, body, (i, jnp.float32(0.0)))
    for p in parts:
        o_ref[p * rows:(p + 1) * rows, :] = acc_ref[p].astype(o_ref.dtype)


def _sb_attention(z):
    t = z.shape[0]
    heads = BRANCH_W // SB_DIM
    upper = jnp.asarray(np.triu(np.ones((SB_BLOCK, SB_BLOCK), np.float32), 1).T, dtype=BF16)
    return pl.pallas_call(
        _sb_body,
        grid=(heads, t // SB_BLOCK),
        in_specs=[pl.BlockSpec((SB_BLOCK, SB_DIM), lambda h, i: (i, h)),
                  pl.BlockSpec((t, SB_DIM), lambda h, i: (0, heads + h)),
                  pl.BlockSpec((t, SB_DIM), lambda h, i: (0, 2 * heads + h)),
                  pl.BlockSpec((SB_BLOCK, SB_BLOCK), lambda h, i: (0, 0))],
        out_specs=pl.BlockSpec((SB_BLOCK, SB_DIM), lambda h, i: (i, h)),
        out_shape=jax.ShapeDtypeStruct((t, heads * SB_DIM), BF16),
        scratch_shapes=[pltpu.VMEM((SB_PARTS, SB_BLOCK // SB_PARTS, SB_DIM), F32),
                        pltpu.VMEM((SB_PARTS, SB_BLOCK // SB_PARTS, 1), F32)],
        compiler_params=_cparams(("arbitrary", "arbitrary")),
        name="stick_breaking",
    )(z, z, z, upper)


def _merge_body(b0_ref, b1_ref, b2_ref, b3_ref, gate_ref, wbr_ref, wout_ref, x_ref, lnw_ref, lnb_ref,
                xo_ref, xb_ref, xp_ref):
    merged = None
    for g, b_ref in enumerate((b0_ref, b1_ref, b2_ref, b3_ref)):
        y = jnp.dot(b_ref[...], wbr_ref[g], preferred_element_type=F32)
        y = y * gate_ref[:, g * D_MODEL:(g + 1) * D_MODEL].astype(F32)
        merged = y if merged is None else merged + y
    mix = jnp.dot(merged.astype(BF16), wout_ref[...], preferred_element_type=F32)
    y = _layer_norm(DEEPNORM_ALPHA * x_ref[...] + mix, lnw_ref[...], lnb_ref[...])
    xo_ref[...] = y
    xb_ref[...] = y.astype(BF16)
    _store_rows(xp_ref, _pack_pair(y[:, :D_MODEL // 2], y[:, D_MODEL // 2:]))


def _merge(branches, gates, w_br, w_out, x, ln_w, ln_b, tm=256):
    t = x.shape[0]
    tm = min(tm, t)
    row = lambda i: (i, 0)
    const2 = lambda i: (0, 0)
    return pl.pallas_call(
        _merge_body,
        grid=(t // tm,),
        in_specs=[pl.BlockSpec((tm, BRANCH_W), row)] * 4 + [
            pl.BlockSpec((tm, N_BRANCHES * D_MODEL), row),
            pl.BlockSpec((N_BRANCHES, BRANCH_W, D_MODEL), lambda i: (0, 0, 0)),
            pl.BlockSpec((D_MODEL, D_MODEL), const2),
            pl.BlockSpec((tm, D_MODEL), row),
            pl.BlockSpec((1, D_MODEL), const2),
            pl.BlockSpec((1, D_MODEL), const2)],
        out_specs=[pl.BlockSpec((tm, D_MODEL), row), pl.BlockSpec((tm, D_MODEL), row),
                   pl.BlockSpec((tm * ROW_TILES, LANES), row)],
        out_shape=[jax.ShapeDtypeStruct((t, D_MODEL), F32), jax.ShapeDtypeStruct((t, D_MODEL), BF16),
                   jax.ShapeDtypeStruct((t * ROW_TILES, LANES), U32)],
        compiler_params=_cparams(("parallel",), 56),
        name="merge_ln",
    )(*branches, gates, w_br, w_out, x, ln_w, ln_b)


def _first_index(hit, idx, size, axis):
    return jnp.min(jnp.where(hit, idx, size), axis=axis, keepdims=True)


def _router_body(x_ref, wr_ref, bias_ref, before_ref, idx_ref, w_ref, rank_ref, cnt_ref, run_ref, *, tm):
    @pl.when(pl.program_id(0) == 0)
    def _():
        run_ref[...] = jnp.zeros_like(run_ref)

    per = N_EXPERTS // N_GROUPS
    w_hi, w_lo = _split(wr_ref[...])
    x_hi, x_lo = _split(x_ref[...])
    ntf = functools.partial(lax.dot_general, dimension_numbers=_NT, preferred_element_type=F32)
    logits = ntf(w_hi, x_hi) + ntf(w_hi, x_lo) + ntf(w_lo, x_hi)
    scores = _sigmoid(logits)
    biased = scores + bias_ref[:, 0:1]
    g3 = biased.reshape(N_GROUPS, per, tm)
    pos = lax.broadcasted_iota(I32, (N_GROUPS, per, tm), 1)
    m1 = jnp.max(g3, axis=1, keepdims=True)
    f1 = _first_index(g3 == m1, pos, per, 1)
    m2 = jnp.max(jnp.where(pos == f1, -jnp.inf, g3), axis=1, keepdims=True)
    gscore = (m1 + m2).reshape(N_GROUPS, tm)

    gpos = lax.broadcasted_iota(I32, (N_GROUPS, tm), 0)
    chosen = jnp.zeros((N_GROUPS, tm), F32)
    cur = gscore
    for _ in range(TOPK_GROUPS):
        m = jnp.max(cur, axis=0, keepdims=True)
        pick = gpos == _first_index(cur == m, gpos, N_GROUPS, 0)
        chosen = jnp.where(pick, 1.0, chosen)
        cur = jnp.where(pick, -jnp.inf, cur)
    ok = jnp.broadcast_to(chosen.reshape(N_GROUPS, 1, tm), (N_GROUPS, per, tm)).reshape(N_EXPERTS, tm)

    epos = lax.broadcasted_iota(I32, (N_EXPERTS, tm), 0)
    cur = jnp.where(ok > 0.5, biased, -jnp.inf)
    picks, idx_rows, w_rows = [], [], []
    member = jnp.zeros((N_EXPERTS, tm), F32)
    for _ in range(TOP_K):
        m = jnp.max(cur, axis=0, keepdims=True)
        f = _first_index(cur == m, epos, N_EXPERTS, 0)
        pick = epos == f
        picks.append(pick)
        idx_rows.append(f)
        w_rows.append(jnp.sum(jnp.where(pick, scores, 0.0), axis=0, keepdims=True))
        member = jnp.where(pick, 1.0, member)
        cur = jnp.where(pick, -jnp.inf, cur)
    w_sel = jnp.concatenate(w_rows, axis=0)
    w_ref[...] = ROUTED_SCALE * w_sel / jnp.sum(w_sel, axis=0, keepdims=True)
    idx_ref[...] = jnp.concatenate(idx_rows, axis=0)

    seen = jnp.dot(member.astype(BF16), before_ref[...], preferred_element_type=F32) + run_ref[:, 0:1]
    rank_rows = [jnp.sum(jnp.where(pk, seen, 0.0), axis=0, keepdims=True) for pk in picks]
    rank_ref[...] = jnp.concatenate(rank_rows, axis=0).astype(I32)
    run_ref[...] = run_ref[...] + jnp.sum(member, axis=1, keepdims=True)
    cnt_ref[...] = run_ref[...]


def _router(x, w_router_t, bias, tm=512):
    t = x.shape[0]
    tm = min(tm, t)
    before = jnp.asarray(np.triu(np.ones((tm, tm), np.float32), 1), dtype=BF16)
    slot = lambda i: (0, i)
    const = lambda i: (0, 0)
    return pl.pallas_call(
        functools.partial(_router_body, tm=tm),
        grid=(t // tm,),
        in_specs=[pl.BlockSpec((tm, D_MODEL), lambda i: (i, 0)),
                  pl.BlockSpec((N_EXPERTS, D_MODEL), const),
                  pl.BlockSpec((N_EXPERTS, LANES), const),
                  pl.BlockSpec((tm, tm), const)],
        out_specs=[pl.BlockSpec((TOP_K, tm), slot), pl.BlockSpec((TOP_K, tm), slot),
                   pl.BlockSpec((TOP_K, tm), slot), pl.BlockSpec((N_EXPERTS, LANES), const)],
        out_shape=[jax.ShapeDtypeStruct((TOP_K, t), I32), jax.ShapeDtypeStruct((TOP_K, t), F32),
                   jax.ShapeDtypeStruct((TOP_K, t), I32), jax.ShapeDtypeStruct((N_EXPERTS, LANES), F32)],
        scratch_shapes=[pltpu.VMEM((N_EXPERTS, LANES), F32)],
        compiler_params=_cparams(("arbitrary",)),
        name="router",
    )(x, w_router_t, bias, before)


def _dest_body(start_ref, idx_ref, rank_ref, dest_ref):
    idx = idx_ref[...]
    base = jnp.zeros(idx.shape, I32)
    for e in range(N_EXPERTS):
        base = jnp.where(idx == e, start_ref[e], base)
    dest_ref[...] = base + rank_ref[...]


def _dest_rows(pad_start, idx, rank, tm=2048):
    t = idx.shape[1]
    tm = min(tm, t)
    slot = lambda i, s: (0, i)
    return pl.pallas_call(
        _dest_body,
        grid_spec=pltpu.PrefetchScalarGridSpec(
            num_scalar_prefetch=1, grid=(t // tm,),
            in_specs=[pl.BlockSpec((TOP_K, tm), slot), pl.BlockSpec((TOP_K, tm), slot)],
            out_specs=pl.BlockSpec((TOP_K, tm), slot)),
        out_shape=jax.ShapeDtypeStruct((TOP_K, t), I32),
        compiler_params=_cparams(("parallel",)),
        name="dest_rows",
    )(pad_start, idx, rank)


def _dispatch_body(zb_ref, dest_ref, x_ref, xs_hbm, zero_ref, sem, zsem, *, tm, n_fill):
    @pl.when(pl.program_id(0) == 0)
    def _():
        zero_ref[...] = jnp.zeros_like(zero_ref)

        def fill(i, carry):
            @pl.when(zb_ref[i] >= 0)
            def _():
                r0 = pl.multiple_of(zb_ref[i] * (ROW_BLOCK * ROW_TILES), ROW_BLOCK * ROW_TILES)
                pltpu.make_async_copy(zero_ref, xs_hbm.at[pl.ds(r0, ROW_BLOCK * ROW_TILES)], zsem).start()
            return carry

        def drain(i, carry):
            @pl.when(zb_ref[i] >= 0)
            def _():
                pltpu.make_async_copy(zero_ref, xs_hbm.at[pl.ds(0, ROW_BLOCK * ROW_TILES)], zsem).wait()
            return carry

        lax.fori_loop(0, n_fill, fill, 0)
        lax.fori_loop(0, n_fill, drain, 0)

    def row(t, carry):
        for k in range(TOP_K):
            pltpu.make_async_copy(_row(x_ref, t), _row(xs_hbm, dest_ref[k, t]), sem).start(
                priority=k % 2)
        return carry

    lax.fori_loop(0, tm, row, 0)
    for _ in range(TOP_K):
        pltpu.make_async_copy(x_ref, xs_hbm.at[pl.ds(0, tm * ROW_TILES)], sem).wait()


def _dispatch(zero_blocks, dest, x_packed, n_rows, tm=256):
    t = x_packed.shape[0] // ROW_TILES
    tm = min(tm, t)
    return pl.pallas_call(
        functools.partial(_dispatch_body, tm=tm, n_fill=zero_blocks.shape[0]),
        grid_spec=pltpu.PrefetchScalarGridSpec(
            num_scalar_prefetch=1, grid=(t // tm,),
            in_specs=[pl.BlockSpec((TOP_K, tm), lambda i, zb: (0, i), memory_space=pltpu.SMEM),
                      pl.BlockSpec((tm * ROW_TILES, LANES), lambda i, zb: (i, 0))],
            out_specs=pl.BlockSpec(memory_space=pl.ANY),
            scratch_shapes=[pltpu.VMEM((ROW_BLOCK * ROW_TILES, LANES), U32), pltpu.SemaphoreType.DMA(()),
                            pltpu.SemaphoreType.DMA(())]),
        out_shape=jax.ShapeDtypeStruct((n_rows * ROW_TILES, LANES), U32),
        compiler_params=_cparams(("arbitrary",)),
        name="dispatch",
    )(zero_blocks, dest, x_packed)


def _experts_body(be_ref, nu_ref, nxt_ref, slot_ref, xs_ref, wg_hbm, wu_hbm, wd_hbm, ys_ref,
                  wgf_ref, wuf_ref, wdf_ref, wgb_ref, wub_ref, wdb_ref, sem, *, layer):
    def weight_copies(e, s):
        return (pltpu.make_async_copy(wg_hbm.at[layer, e], wgf_ref.at[s], sem.at[s]),
                pltpu.make_async_copy(wu_hbm.at[layer, e], wuf_ref.at[s], sem.at[s]),
                pltpu.make_async_copy(wd_hbm.at[layer, e], wdf_ref.at[s], sem.at[s]))

    @pl.when(pl.program_id(0) == 0)
    def _():
        for c in weight_copies(be_ref[0], slot_ref[0]):
            c.start()

    def block(b, xs_blk, ys_blk):
        used = b < nu_ref[0]
        new_expert = jnp.logical_or(b == 0, be_ref[b] != be_ref[jnp.maximum(b - 1, 0)])

        @pl.when(jnp.logical_and(used, new_expert))
        def _():
            s = slot_ref[b]
            for c in weight_copies(be_ref[b], s):
                c.wait()

            @pl.when(nxt_ref[b] >= 0)
            def _():
                for c in weight_copies(nxt_ref[b], 1 - s):
                    c.start()

            wgb_ref[...] = wgf_ref[s].astype(BF16)
            wub_ref[...] = wuf_ref[s].astype(BF16)
            wdb_ref[...] = wdf_ref[s].astype(BF16)

        @pl.when(used)
        def _():
            half = D_MODEL // 2
            lo, hi = _unpack_pair(_load_rows(xs_blk))
            lo = lo.astype(BF16)
            hi = hi.astype(BF16)
            gate = (jnp.dot(lo, wgb_ref[:half, :], preferred_element_type=F32)
                    + jnp.dot(hi, wgb_ref[half:, :], preferred_element_type=F32))
            up = (jnp.dot(lo, wub_ref[:half, :], preferred_element_type=F32)
                  + jnp.dot(hi, wub_ref[half:, :], preferred_element_type=F32))
            h = (_silu(gate) * up).astype(BF16)
            y = jnp.dot(h, wdb_ref[...], preferred_element_type=F32)
            _store_rows(ys_blk, _pack_pair(y[:, :half], y[:, half:]))

        @pl.when(jnp.logical_not(used))
        def _():
            ys_blk[...] = jnp.zeros_like(ys_blk)

    rows = ROW_BLOCK * ROW_TILES
    for sub in range(EXPERT_STEP_BLOCKS):
        block(pl.program_id(0) * EXPERT_STEP_BLOCKS + sub,
              xs_ref.at[pl.ds(sub * rows, rows)], ys_ref.at[pl.ds(sub * rows, rows)])


def _experts(block_expert, n_used, next_expert, slot, xs, wg, wu, wd, layer):
    n_rows = xs.shape[0] // ROW_TILES
    n_steps = n_rows // (ROW_BLOCK * EXPERT_STEP_BLOCKS)
    step_rows = EXPERT_STEP_BLOCKS * ROW_BLOCK * ROW_TILES
    blk = lambda i, be, nu, nx, sl: (jnp.minimum(i, (nu[0] - 1) // EXPERT_STEP_BLOCKS), 0)
    out_blk = lambda i, be, nu, nx, sl: (i, 0)
    hbm = pl.BlockSpec(memory_space=pl.ANY)
    return pl.pallas_call(
        functools.partial(_experts_body, layer=layer),
        grid_spec=pltpu.PrefetchScalarGridSpec(
            num_scalar_prefetch=4, grid=(n_steps,),
            in_specs=[pl.BlockSpec((step_rows, LANES), blk), hbm, hbm, hbm],
            out_specs=pl.BlockSpec((step_rows, LANES), out_blk),
            scratch_shapes=[pltpu.VMEM((2, D_MODEL, EXPERT_W), F32), pltpu.VMEM((2, D_MODEL, EXPERT_W), F32),
                            pltpu.VMEM((2, EXPERT_W, D_MODEL), F32),
                            pltpu.VMEM((D_MODEL, EXPERT_W), BF16), pltpu.VMEM((D_MODEL, EXPERT_W), BF16),
                            pltpu.VMEM((EXPERT_W, D_MODEL), BF16), pltpu.SemaphoreType.DMA((2,))]),
        out_shape=jax.ShapeDtypeStruct((n_rows * ROW_TILES, LANES), U32),
        compiler_params=_cparams(("arbitrary",), 56),
        name="experts",
    )(block_expert, n_used, next_expert, slot, xs, wg, wu, wd)


def _combine_body(dest_ref, dnext_ref, w_ref, x_ref, xb_ref, sg_ref, su_ref, sd_ref, lnw_ref, lnb_ref, ys_hbm,
                  xo_ref, xb_out_ref, buf_ref, lo_ref, hi_ref, sem, *, tm, n_steps):
    i = pl.program_id(0)
    cur = i % 2

    def request(d_ref, s):
        def row(t, carry):
            for k in range(TOP_K):
                pltpu.make_async_copy(_row(ys_hbm, d_ref[k, t]), _row(buf_ref.at[s, k], t), sem.at[s]).start(
                    priority=k % 2)
            return carry

        lax.fori_loop(0, tm, row, 0)

    @pl.when(i == 0)
    def _():
        request(dest_ref, 0)

    @pl.when(i + 1 < n_steps)
    def _():
        request(dnext_ref, 1 - cur)

    for k in range(TOP_K):
        pltpu.make_async_copy(ys_hbm.at[pl.ds(0, tm * ROW_TILES)], buf_ref.at[cur, k], sem.at[cur]).wait()

    xb = xb_ref[...]
    h = _silu(jnp.dot(xb, sg_ref[...], preferred_element_type=F32)) * jnp.dot(xb, su_ref[...],
                                                                           preferred_element_type=F32)
    shared = jnp.dot(h.astype(BF16), sd_ref[...], preferred_element_type=F32)

    lo_acc = hi_acc = None
    for k in range(TOP_K):
        lo, hi = _unpack_pair(buf_ref[cur, k])
        wk = w_ref[:, k:k + 1]
        lo_acc = wk * lo if lo_acc is None else lo_acc + wk * lo
        hi_acc = wk * hi if hi_acc is None else hi_acc + wk * hi
    lo_ref[...] = lo_acc
    hi_ref[...] = hi_acc
    routed = jnp.concatenate([_load_rows(lo_ref), _load_rows(hi_ref)], axis=1)
    y = _layer_norm(DEEPNORM_ALPHA * x_ref[...] + routed + shared, lnw_ref[...], lnb_ref[...])
    xo_ref[...] = y
    xb_out_ref[...] = y.astype(BF16)


def _combine(dest, w_sel, x, xb, sg, su, sd, ln_w, ln_b, ys, tm=256):
    t = x.shape[0]
    tm = min(tm, t)
    n_steps = t // tm
    row = lambda i: (i, 0)
    slot = lambda i: (0, i)
    slot_next = lambda i: (0, jnp.minimum(i + 1, n_steps - 1))
    const = lambda i: (0, 0)
    return pl.pallas_call(
        functools.partial(_combine_body, tm=tm, n_steps=n_steps),
        grid=(n_steps,),
        in_specs=[pl.BlockSpec((TOP_K, tm), slot, memory_space=pltpu.SMEM),
                  pl.BlockSpec((TOP_K, tm), slot_next, memory_space=pltpu.SMEM),
                  pl.BlockSpec((tm * ROW_TILES, TOP_K), row),
                  pl.BlockSpec((tm, D_MODEL), row),
                  pl.BlockSpec((tm, D_MODEL), row),
                  pl.BlockSpec((D_MODEL, EXPERT_W), const),
                  pl.BlockSpec((D_MODEL, EXPERT_W), const),
                  pl.BlockSpec((EXPERT_W, D_MODEL), const),
                  pl.BlockSpec((1, D_MODEL), const),
                  pl.BlockSpec((1, D_MODEL), const),
                  pl.BlockSpec(memory_space=pl.ANY)],
        out_specs=[pl.BlockSpec((tm, D_MODEL), row), pl.BlockSpec((tm, D_MODEL), row)],
        out_shape=[jax.ShapeDtypeStruct((t, D_MODEL), F32), jax.ShapeDtypeStruct((t, D_MODEL), BF16)],
        scratch_shapes=[pltpu.VMEM((2, TOP_K, tm * ROW_TILES, LANES), U32),
                        pltpu.VMEM((tm * ROW_TILES, LANES), F32), pltpu.VMEM((tm * ROW_TILES, LANES), F32),
                        pltpu.SemaphoreType.DMA((2,))],
        compiler_params=_cparams(("arbitrary",), 56),
        name="combine_ln",
    )(dest, dest, w_sel, x, xb, sg, su, sd, ln_w, ln_b, ys)


def _pad_cols(a, width):
    return jnp.pad(a, [(0, 0)] * (a.ndim - 1) + [(0, width - a.shape[-1])])


def _pad_rows(a, height):
    return jnp.pad(a, ((0, height - a.shape[0]), (0, 0)))


def _pad_heads(a, heads, dk):
    lead = a.shape[:-1]
    padded = jnp.pad(a.reshape(*lead, heads, dk), [(0, 0)] * (len(lead) + 1) + [(0, HEAD_PAD - dk)])
    return padded.reshape(*lead, heads * HEAD_PAD)


def _relayout_body(w_ref, o_ref):
    o_ref[...] = jnp.zeros_like(o_ref)

    def put(dst, src, width, scale=None):
        v = w_ref[:, src:src + width]
        if scale is not None:
            v = v * scale
        o_ref[:, dst:dst + width] = v.astype(o_ref.dtype)

    put(ZHG_OFF, HG_OFF, 4 * BRANCH_W)
    put(ZRW_OFF, RW_OFF, 3 * BRANCH_W)
    put(ZRW_OFF + 3 * BRANCH_W, RW_OFF + 1536, 32)
    put(ZRW_OFF + 3 * BRANCH_W + LANES, RW_OFF + 1568, 32)
    put(ZRW_OFF + 3 * BRANCH_W + 2 * LANES, RW_OFF + 1600, 96)
    put(ZSB_OFF, SB_OFF, BRANCH_W, SB_DIM ** -0.5)
    put(ZSB_OFF + BRANCH_W, SB_OFF + BRANCH_W, 2 * BRANCH_W)
    for h in range(4):
        put(ZGLA_OFF + h * HEAD_PAD, GLA_OFF + h * GLA_DK, GLA_DK)
        put(ZGLA_OFF + BRANCH_W + h * HEAD_PAD, GLA_OFF + 256 + h * GLA_DK, GLA_DK)
    put(ZGLA_OFF + 2 * BRANCH_W, GLA_OFF + 512, BRANCH_W)
    put(ZGLA_OFF + 3 * BRANCH_W, GLA_OFF + 1040, BRANCH_W)
    put(ZGLA_OFF + 4 * BRANCH_W, GLA_OFF + 1024, 16)
    put(ZGATE_OFF, GATE_OFF, N_BRANCHES * D_MODEL)


def _relayout_in_weight(w_in, tk=128):
    n_layers, d, width = w_in.shape
    return pl.pallas_call(
        _relayout_body,
        grid=(n_layers, d // tk),
        in_specs=[pl.BlockSpec((None, tk, width), lambda l, i: (l, i, 0))],
        out_specs=pl.BlockSpec((None, tk, Z_TOTAL), lambda l, i: (l, i, 0)),
        out_shape=jax.ShapeDtypeStruct((n_layers, d, Z_TOTAL), BF16),
        compiler_params=_cparams(("parallel", "parallel")),
        name="relayout_w_in",
    )(w_in)


def _token_mixing(xb, w_all, layer, lower_bound, hg_norm_w, rw_mu, rw_w0, rw_w2, rw_a0, rw_a2, rw_g2, rw_kk,
                  rw_ka, rw_rk, rw_ln_w, rw_ln_b, gla_g2, gla_gb, gla_norm_w):
    row = lambda a: a.reshape(1, -1).astype(F32)

    z_hg = _matmul(xb, w_all, layer, ZHG_OFF, 4 * BRANCH_W, F32, tn=2 * PROJ_TN)
    o_hg = _gated_mixer(z_hg, row(lower_bound), row(hg_norm_w), jnp.zeros((8, BRANCH_W), F32), "hgrn2")

    z_rw = _matmul(xb, w_all, layer, ZRW_OFF, RW_Z, F32, tn=2 * PROJ_TN)
    mu = jnp.concatenate([rw_mu[:1536], jnp.pad(rw_mu[1536:1568], (0, 96)), jnp.pad(rw_mu[1568:1600], (0, 96)),
                          jnp.pad(rw_mu[1600:1696], (0, 32 + RW_Z - 3 * BRANCH_W - 3 * LANES))]).reshape(1, RW_Z)
    vecs = jnp.stack([rw_w0, rw_a0, rw_kk, rw_ka, rw_rk, rw_ln_w, rw_ln_b, jnp.zeros_like(rw_w0)]).astype(F32)
    o_rw = _rwkv_mixer(z_rw, mu, vecs, _pad_rows(rw_w2, LANES), _pad_rows(rw_a2, LANES), _pad_rows(rw_g2, LANES))

    o_sb = _sb_attention(_matmul(xb, w_all, layer, ZSB_OFF, 3 * BRANCH_W, BF16))

    z_gla = _matmul(xb, w_all, layer, ZGLA_OFF, GLA_Z, F32)
    g2p = _pad_rows(_pad_heads(gla_g2, 4, GLA_DK), LANES)
    gbp = _pad_heads(gla_gb.reshape(1, -1), 4, GLA_DK)
    o_gla = _gated_mixer(z_gla, gbp, row(gla_norm_w), g2p, "gla")

    gates = _matmul(xb, w_all, layer, ZGATE_OFF, N_BRANCHES * D_MODEL, BF16, act="sigmoid", tn=1024)
    return (o_hg, o_rw, o_sb, o_gla), gates


def _moe(x, xb, xp, w_router, router_bias, we_gate, we_up, we_down, layer, ws_gate, ws_up, ws_down, ln_w, ln_b):
    t = x.shape[0]
    bias = jnp.broadcast_to(router_bias.astype(F32).reshape(N_EXPERTS, 1), (N_EXPERTS, LANES))
    idx, w_sel, rank, counts = _router(x, w_router.T.astype(F32), bias)

    cnt = counts[:, 0].astype(I32)
    padded = (cnt + ROW_BLOCK - 1) // ROW_BLOCK * ROW_BLOCK
    pad_end = jnp.cumsum(padded)
    pad_start = (pad_end - padded).astype(I32)
    n_blocks = -(-(t * TOP_K // ROW_BLOCK + N_EXPERTS) // EXPERT_STEP_BLOCKS) * EXPERT_STEP_BLOCKS
    n_used = (pad_end[-1:] // ROW_BLOCK).astype(I32)
    first_row = jnp.arange(n_blocks, dtype=I32) * ROW_BLOCK
    block_expert = jnp.minimum(jnp.sum(pad_end[None, :] <= first_row[:, None], axis=1), N_EXPERTS - 1).astype(I32)

    last_block = jnp.where(padded > 0, pad_end // ROW_BLOCK - 1, -1)
    tail = n_used[0] + jnp.arange(N_EXPERTS, dtype=I32)
    zero_blocks = jnp.concatenate([last_block, jnp.where(tail < n_blocks, tail, -1)]).astype(I32)

    has_rows = cnt > 0
    eid = jnp.arange(N_EXPERTS, dtype=I32)
    later = jnp.where(has_rows[None, :] & (eid[None, :] > eid[:, None]), eid[None, :], N_EXPERTS)
    next_used = jnp.min(later, axis=1)
    next_used = jnp.where(next_used < N_EXPERTS, next_used, -1).astype(I32)
    ordinal = jnp.cumsum(has_rows.astype(I32)) - 1
    of_block = block_expert[:, None] == eid[None, :]
    next_expert = jnp.sum(jnp.where(of_block, next_used[None, :], 0), axis=1).astype(I32)
    slot = (jnp.sum(jnp.where(of_block, ordinal[None, :], 0), axis=1) % 2).astype(I32)

    dest = _dest_rows(pad_start, idx, rank)
    xs = _dispatch(zero_blocks, dest, xp, n_blocks * ROW_BLOCK)
    ys = _experts(block_expert, n_used, next_expert, slot, xs, we_gate, we_up, we_down, layer)
    w_rows = jnp.repeat(w_sel.T, ROW_TILES, axis=0)
    return _combine(dest, w_rows, x, xb, ws_gate.astype(BF16), ws_up.astype(BF16), ws_down.astype(BF16),
                    ln_w.reshape(1, -1), ln_b.reshape(1, -1), ys)


def kernel(x, w_in, hg_lb_logits, hg_norm_w, rw_mu, rw_w0, rw_w2, rw_a0, rw_a2, rw_g2, rw_kk, rw_ka, rw_rk,
           rw_ln_w, rw_ln_b, gla_g2, gla_gb, gla_norm_w, w_br, w_out, ln1_w, ln1_b, w_router, router_bias,
           we_gate, we_up, we_down, ws_gate, ws_up, ws_down, ln2_w, ln2_b):
    bsz, t, d = x.shape
    cum = jnp.cumsum(jax.nn.softmax(hg_lb_logits.astype(F32), axis=0), axis=0)
    lower_bounds = cum - cum[0:1]
    w_all = _relayout_in_weight(w_in)
    outs = []
    for bi in range(bsz):
        xf = x[bi].astype(F32)
        xb = xf.astype(BF16)
        for l in range(DEPTH):
            branches, gates = _token_mixing(
                xb, w_all, l, lower_bounds[l], hg_norm_w[l], rw_mu[l], rw_w0[l], rw_w2[l], rw_a0[l], rw_a2[l],
                rw_g2[l], rw_kk[l], rw_ka[l], rw_rk[l], rw_ln_w[l], rw_ln_b[l], gla_g2[l], gla_gb[l],
                gla_norm_w[l])
            xf, xb, xp = _merge(branches, gates, w_br[l].astype(BF16), w_out[l].astype(BF16), xf,
                                ln1_w[l].reshape(1, -1), ln1_b[l].reshape(1, -1))
            xf, xb = _moe(xf, xb, xp, w_router[l], router_bias[l], we_gate, we_up, we_down, l,
                          ws_gate[l], ws_up[l], ws_down[l], ln2_w[l], ln2_b[l])
        outs.append(xf)
    return jnp.stack(outs).astype(x.dtype)
```

```python
import functools

import jax
import jax.numpy as jnp
import numpy as np
from jax import lax
from jax.experimental import pallas as pl
from jax.experimental.pallas import tpu as pltpu

F32 = jnp.float32
BF16 = jnp.bfloat16
I32 = jnp.int32
U32 = jnp.uint32

D_MODEL = 2048
DEPTH = 2
BRANCH_W = 512
N_BRANCHES = 4
CHUNK = 64
SUB = 16
N_SUB = CHUNK // SUB
LANES = 128
HEAD_PAD = 128
RW_HEADS = 8
RW_DIM = 64
RW_PAIRS = RW_HEADS // 2
GATED_SPAN = 4
RW_SPAN = 4
GLA_DK = 64
GLA_TAU = 16.0
SB_DIM = 128
SB_BLOCK = 256
SB_PARTS = 2
SB_SKIP = 120.0
N_EXPERTS = 64
TOP_K = 8
N_GROUPS = 8
TOPK_GROUPS = 4
EXPERT_W = 512
ROUTED_SCALE = 2.5
ROW_BLOCK = 256
EXPERT_STEP_BLOCKS = 2
DEEPNORM_ALPHA = (2 * DEPTH) ** 0.25
LN_EPS = 1e-5
RW_GN_EPS = 64e-5
LOG2E = 1.4426950408889634

HG_OFF = 0
RW_OFF = 2048
SB_OFF = 3744
GLA_OFF = 5280
GATE_OFF = 6832
PROJ_TN = 512
RW_Z = 2048
GLA_Z = 2560
ZHG_OFF = 0
ZRW_OFF = ZHG_OFF + 4 * BRANCH_W
ZSB_OFF = ZRW_OFF + RW_Z
ZGLA_OFF = ZSB_OFF + 3 * BRANCH_W
ZGATE_OFF = ZGLA_OFF + GLA_Z
Z_TOTAL = ZGATE_OFF + N_BRANCHES * D_MODEL

_NT = (((1,), (1,)), ((), ()))
_TN = (((0,), (0,)), ((), ()))


def _cparams(sem, vmem_mb=48):
    return pltpu.CompilerParams(dimension_semantics=sem, vmem_limit_bytes=vmem_mb << 20)


def _sigmoid(x):
    return 1.0 / (1.0 + jnp.exp(-x))


def _log_sigmoid(x):
    return jnp.minimum(x, 0.0) - jnp.log1p(jnp.exp(-jnp.abs(x)))


def _silu(x):
    return x * _sigmoid(x)


def _layer_norm(y, w, b):
    mu = jnp.mean(y, axis=-1, keepdims=True)
    d = y - mu
    var = jnp.mean(d * d, axis=-1, keepdims=True)
    return d * lax.rsqrt(var + LN_EPS) * w + b


def _pack_pair(lo, hi):
    lo_b = lax.bitcast_convert_type(lo.astype(BF16).astype(F32), U32) >> 16
    hi_b = lax.bitcast_convert_type(hi.astype(BF16).astype(F32), U32) & jnp.uint32(0xFFFF0000)
    return lo_b | hi_b


def _unpack_pair(u):
    lo = lax.bitcast_convert_type(u << 16, F32)
    hi = lax.bitcast_convert_type(u & jnp.uint32(0xFFFF0000), F32)
    return lo, hi


ROW_TILES = D_MODEL // 2 // LANES


def _store_rows(ref, packed):
    r = packed.shape[0]
    for j in range(ROW_TILES):
        ref[pl.ds(j, r, stride=ROW_TILES), :] = packed[:, j * LANES:(j + 1) * LANES]


def _load_rows(ref):
    r = ref.shape[0] // ROW_TILES
    return jnp.concatenate([ref[pl.ds(j, r, stride=ROW_TILES), :] for j in range(ROW_TILES)], axis=1)


def _row(ref, i):
    return ref.at[pl.ds(pl.multiple_of(i * ROW_TILES, ROW_TILES), ROW_TILES)]


def _split(x):
    hi = x.astype(BF16)
    return hi, (x - hi.astype(F32)).astype(BF16)


def _dot2(x, ones_bf16):
    hi, lo = _split(x)
    return (jnp.dot(hi, ones_bf16, preferred_element_type=F32)
            + jnp.dot(lo, ones_bf16, preferred_element_type=F32))


def _ldot2(ones_bf16, x):
    hi, lo = _split(x)
    return (jnp.dot(ones_bf16, hi, preferred_element_type=F32)
            + jnp.dot(ones_bf16, lo, preferred_element_type=F32))


def _dot3(a, b):
    a_hi, a_lo = _split(a)
    b_hi, b_lo = _split(b)
    return (jnp.dot(a_hi, b_hi, preferred_element_type=F32) + jnp.dot(a_hi, b_lo, preferred_element_type=F32)
            + jnp.dot(a_lo, b_hi, preferred_element_type=F32))


def _mm_body(x_ref, w_ref, o_ref, *, act):
    acc = lax.dot_general(x_ref[...], w_ref[...], _NT, preferred_element_type=F32)
    if act == "sigmoid":
        acc = _sigmoid(acc)
    o_ref[...] = acc.astype(o_ref.dtype)


def _matmul(x, w, layer, col0, n, out_dtype, act=None, tm=1024, tn=PROJ_TN):
    m, k = x.shape
    tm = min(tm, m)
    off = col0 // tn
    return pl.pallas_call(
        functools.partial(_mm_body, act=act),
        grid=(m // tm, n // tn),
        in_specs=[pl.BlockSpec((tm, k), lambda i, j: (i, 0)),
                  pl.BlockSpec((None, tn, k), lambda i, j: (layer, off + j, 0))],
        out_specs=pl.BlockSpec((tm, tn), lambda i, j: (i, j)),
        out_shape=jax.ShapeDtypeStruct((m, n), out_dtype),
        compiler_params=_cparams(("parallel", "parallel")),
        name="proj",
    )(x, w)


def _gated_span(q, k, v, g, st, tril):
    hs = range(len(q))
    lanes = [(ci, h) for ci in range(GATED_SPAN) for h in hs]
    ls = range(len(lanes))
    subs = range(N_SUB)
    dotf = functools.partial(jnp.dot, preferred_element_type=F32)
    blk = lambda x, i: x[i * SUB:(i + 1) * SUB]
    cut = lambda xs: [xs[h][ci * CHUNK:(ci + 1) * CHUNK] for ci, h in lanes]
    b_f = [_ldot2(tril, g[h] * LOG2E) for h in hs]
    qe_f = [(q[h] * jnp.exp2(b_f[h])).astype(BF16) for h in hs]
    vb_f = [v[h].astype(BF16) for h in hs]
    b, qc, kc, vb, qe = cut(b_f), cut(q), cut(k), cut(vb_f), cut(qe_f)
    blast = [b[i][CHUNK - 1:CHUNK, :] for i in ls]
    ends = [[b[i][(j + 1) * SUB - 1:(j + 1) * SUB, :] for j in subs] for i in ls]
    khat = [[blk(kc[i], j) * jnp.exp2(ends[i][j] - blk(b[i], j)) for j in subs] for i in ls]
    o_parts = [[None] * N_SUB for _ in ls]

    for j in range(N_SUB - 1):
        lo = (j + 1) * SUB
        qs = [(qc[i][lo:] * jnp.exp2(b[i][lo:] - ends[i][j])).astype(BF16) for i in ls]
        a = [lax.dot_general(qs[i], khat[i][j].astype(BF16), _NT, preferred_element_type=F32) for i in ls]
        pv = [dotf(a[i].astype(BF16), blk(vb[i], j)) for i in ls]
        for i in ls:
            for t in range(j + 1, N_SUB):
                piece = pv[i][(t - j - 1) * SUB:(t - j) * SUB]
                o_parts[i][t] = piece if o_parts[i][t] is None else o_parts[i][t] + piece

    lane = lax.broadcasted_iota(I32, (SUB, LANES), 1)
    trow = lax.broadcasted_iota(I32, (SUB, 1), 0)
    half = SUB // 2
    for t in subs:
        d_top = [jnp.zeros((half, LANES), F32) for _ in ls]
        d_bot = [jnp.zeros((half, LANES), F32) for _ in ls]
        for s in range(SUB):
            for i in ls:
                bi, qi, ki = blk(b[i], t), blk(qc[i], t), blk(kc[i], t)
                if s < half:
                    col = jnp.sum(qi * jnp.exp2(bi - bi[s:s + 1, :]) * ki[s:s + 1, :], axis=-1, keepdims=True)
                    d_top[i] = jnp.where(lane[:half] == s, col[:half], d_top[i])
                    d_bot[i] = jnp.where(lane[:half] == s, col[half:], d_bot[i])
                else:
                    col = jnp.sum(qi[half:] * jnp.exp2(bi[half:] - bi[s:s + 1, :]) * ki[s:s + 1, :],
                                  axis=-1, keepdims=True)
                    d_bot[i] = jnp.where(lane[:half] == s, col, d_bot[i])
        d = [jnp.concatenate([d_top[i], d_bot[i]], axis=0) for i in ls]
        pv = [dotf(jnp.where(lane <= trow, d[i], 0.0)[:, :SUB].astype(BF16), blk(vb[i], t)) for i in ls]
        for i in ls:
            o_parts[i][t] = pv[i] if o_parts[i][t] is None else o_parts[i][t] + pv[i]

    o_intra = [jnp.concatenate(o_parts[i], axis=0) for i in ls]
    kd = [jnp.concatenate([khat[i][j] * jnp.exp2(blast[i] - ends[i][j]) for j in subs], axis=0).astype(BF16)
          for i in ls]
    kv = [lax.dot_general(vb[i], kd[i], _TN, preferred_element_type=F32) for i in ls]
    dec = [jnp.exp2(blast[i]) for i in ls]

    state = list(st)
    o_ch = []
    for ci in range(GATED_SPAN):
        idx = [ci * len(q) + h for h in hs]
        o_ch.append([lax.dot_general(qe[idx[h]], state[h].astype(BF16), _NT, preferred_element_type=F32)
                     + o_intra[idx[h]] for h in hs])
        state = [state[h] * dec[idx[h]] + kv[idx[h]] for h in hs]
    o = [jnp.concatenate([o_ch[ci][h] for ci in range(GATED_SPAN)], axis=0) for h in hs]
    return o, state


def _gated_body(z_ref, aux_ref, nw_ref, g2_ref, tril_ref, o_ref, st_ref, *, mode, rows):
    @pl.when(pl.program_id(0) == 0)
    def _():
        st_ref[...] = jnp.zeros_like(st_ref)

    tril = tril_ref[...]
    span = GATED_SPAN * CHUNK

    def step(c, carry):
        r0 = pl.multiple_of(c * span, span)
        rs = pl.ds(r0, span)
        hs = range(4)
        sls = [slice(h * HEAD_PAD, (h + 1) * HEAD_PAD) for h in hs]
        zq = [z_ref[rs, h * HEAD_PAD:(h + 1) * HEAD_PAD] for h in hs]
        zk = [z_ref[rs, BRANCH_W + h * HEAD_PAD:BRANCH_W + (h + 1) * HEAD_PAD] for h in hs]
        v = [z_ref[rs, 2 * BRANCH_W + h * HEAD_PAD:2 * BRANCH_W + (h + 1) * HEAD_PAD] for h in hs]
        if mode == "hgrn2":
            q = [_silu(zq[h]) for h in hs]
            k = [(1.0 - aux_ref[0:1, sls[h]]) * _sigmoid(-zk[h]) for h in hs]
            g = [jnp.log1p(-k[h]) for h in hs]
        else:
            q = [zq[h] * (GLA_DK ** -0.5) for h in hs]
            k = zk
            la = _dot3(z_ref[rs, 4 * BRANCH_W:4 * BRANCH_W + LANES], g2_ref[...]) + aux_ref[...]
            g = [_log_sigmoid(la[:, sls[h]]) * (1.0 / GLA_TAU) for h in hs]
        o, st_new = _gated_span(q, k, v, g, [st_ref[h] for h in hs], tril)
        for h in hs:
            st_ref[h] = st_new[h]
            gate = z_ref[rs, 3 * BRANCH_W + h * HEAD_PAD:3 * BRANCH_W + (h + 1) * HEAD_PAD]
            on = o[h] * lax.rsqrt(jnp.mean(o[h] * o[h], axis=-1, keepdims=True) + LN_EPS)
            o_ref[rs, sls[h]] = (on * nw_ref[0:1, sls[h]] * _silu(gate)).astype(o_ref.dtype)
        return carry

    lax.fori_loop(0, rows // span, step, 0)


def _gated_mixer(z, aux, norm_w, g2, mode, rows=256):
    t, wz = z.shape
    rows = min(rows, t)
    span = GATED_SPAN * CHUNK
    pos = np.arange(span)
    tril = jnp.asarray(((pos[:, None] >= pos[None, :]) & (pos[:, None] // CHUNK == pos[None, :] // CHUNK))
                       .astype(np.float32), dtype=BF16)
    return pl.pallas_call(
        functools.partial(_gated_body, mode=mode, rows=rows),
        grid=(t // rows,),
        in_specs=[pl.BlockSpec((rows, wz), lambda i: (i, 0)),
                  pl.BlockSpec((1, BRANCH_W), lambda i: (0, 0)),
                  pl.BlockSpec((1, BRANCH_W), lambda i: (0, 0)),
                  pl.BlockSpec(g2.shape, lambda i: (0, 0)),
                  pl.BlockSpec((span, span), lambda i: (0, 0))],
        out_specs=pl.BlockSpec((rows, BRANCH_W), lambda i: (i, 0)),
        out_shape=jax.ShapeDtypeStruct((t, BRANCH_W), BF16),
        scratch_shapes=[pltpu.VMEM((4, HEAD_PAD, HEAD_PAD), F32)],
        compiler_params=_cparams(("arbitrary",)),
        name="gated_" + mode,
    )(z, aux, norm_w, g2, tril)


def _rwkv_body(z_ref, mu_ref, vec_ref, w2_ref, a2_ref, g2_ref, tril_ref, ones2_ref, o_ref,
               prev_ref, st_ref, *, rows):
    @pl.when(pl.program_id(0) == 0)
    def _():
        prev_ref[...] = jnp.zeros_like(prev_ref)
        st_ref[...] = jnp.zeros_like(st_ref)

    tril = tril_ref[...]
    ones2 = ones2_ref[...]
    row128 = lax.broadcasted_iota(I32, (2 * CHUNK, 2 * CHUNK), 0)
    col128 = lax.broadcasted_iota(I32, (2 * CHUNK, 2 * CHUNK), 1)
    rt = jnp.where(row128 >= CHUNK, row128 - CHUNK, row128)
    ct = jnp.where(col128 >= CHUNK, col128 - CHUNK, col128)
    strict = rt > ct
    incl = rt >= ct
    eye = row128 == col128
    head0 = lax.broadcasted_iota(I32, (1, LANES), 1) < RW_DIM
    span = RW_SPAN * CHUNK
    first_row = lax.broadcasted_iota(I32, (span, 1), 0) == 0

    def stack(x):
        return jnp.concatenate([jnp.where(head0, x, 0.0), jnp.where(head0, 0.0, x)], axis=0)

    def step(c, carry):
        r0 = pl.multiple_of(c * span, span)
        rs = pl.ds(r0, span)
        z = z_ref[rs, :]
        zprev = jnp.where(first_row, prev_ref[...], pltpu.roll(z, 1, axis=0))
        prev_ref[...] = z[span - 1:span, :]
        zs = z + (zprev - z) * mu_ref[...]
        lw = zs[:, 3 * BRANCH_W:3 * BRANCH_W + LANES]
        la = zs[:, 3 * BRANCH_W + LANES:3 * BRANCH_W + 2 * LANES]
        lg = zs[:, 3 * BRANCH_W + 2 * LANES:3 * BRANCH_W + 3 * LANES]
        wl = -(vec_ref[0:1, :] + _dot3(jnp.tanh(lw), w2_ref[...]))
        w_raw = -(jnp.maximum(wl, 0.0) + jnp.log1p(jnp.exp(-jnp.abs(wl)))) - 0.5
        logw_all = -jnp.exp(w_raw)
        iclr_all = _sigmoid(vec_ref[1:2, :] + _dot3(la, a2_ref[...]))
        gate_all = _dot3(_sigmoid(lg), g2_ref[...])
        cum_all = _ldot2(tril, logw_all)

        pairs = range(RW_PAIRS)
        sls = [slice(p * LANES, (p + 1) * LANES) for p in pairs]
        dotf = functools.partial(jnp.dot, preferred_element_type=F32)
        c2 = 2 * CHUNK
        r_f = [zs[:, p * LANES:(p + 1) * LANES] for p in pairs]
        k_f = [zs[:, BRANCH_W + p * LANES:BRANCH_W + (p + 1) * LANES] for p in pairs]
        v_f = [zs[:, 2 * BRANCH_W + p * LANES:2 * BRANCH_W + (p + 1) * LANES] for p in pairs]
        cum_f = [cum_all[:, s] for s in sls]
        iclr_f = [iclr_all[:, s] for s in sls]
        kkr = [k_f[p] * vec_ref[2:3, sls[p]] for p in pairs]
        ssq = [_dot2(kkr[p] * kkr[p], ones2) for p in pairs]
        kk_f = [kkr[p] / jnp.maximum(jnp.sqrt(ssq[p]), 1e-12) for p in pairs]
        kmod_f = [k_f[p] * (1.0 + (iclr_f[p] - 1.0) * vec_ref[3:4, sls[p]]) for p in pairs]
        bb_f = [kk_f[p] * iclr_f[p] for p in pairs]
        eneg_f = [jnp.exp(-cum_f[p]) for p in pairs]
        a_f = [-kk_f[p] * jnp.exp(cum_f[p] - logw_all[:, sls[p]]) for p in pairs]
        rd_f = [r_f[p] * jnp.exp(cum_f[p]) for p in pairs]
        bn_f = [bb_f[p] * eneg_f[p] for p in pairs]
        kn_f = [kmod_f[p] * eneg_f[p] for p in pairs]

        lanes = [(ci, p) for ci in range(RW_SPAN) for p in pairs]
        ls = range(len(lanes))
        cut = lambda xs: [xs[p][ci * CHUNK:(ci + 1) * CHUNK] for ci, p in lanes]
        cum, bb, kmod = cut(cum_f), cut(bb_f), cut(kmod_f)
        clast = [cum[i][CHUNK - 1:CHUNK, :] for i in ls]
        e_end = [jnp.exp(clast[i] - cum[i]) for i in ls]
        a_s = [stack(x) for x in cut(a_f)]
        r_s = [stack(x) for x in cut(rd_f)]
        b_s = [stack(x) for x in cut(bn_f)]
        k_s = [stack(x) for x in cut(kn_f)]
        bh_s = [stack(bb[i] * e_end[i]).astype(BF16) for i in ls]
        kh_s = [stack(kmod[i] * e_end[i]).astype(BF16) for i in ls]
        v_sb = [stack(x).astype(BF16) for x in cut(v_f)]

        left = [jnp.concatenate([a_s[i], r_s[i]], axis=0).astype(BF16) for i in ls]
        right = [jnp.concatenate([b_s[i], k_s[i]], axis=0).astype(BF16) for i in ls]
        gram = [lax.dot_general(left[i], right[i], _NT, preferred_element_type=F32) for i in ls]
        n_ab = [jnp.where(strict, gram[i][:c2, :c2], 0.0) for i in ls]
        a_ak = [jnp.where(strict, gram[i][:c2, c2:], 0.0).astype(BF16) for i in ls]
        g_bk = [jnp.concatenate([jnp.where(incl, gram[i][c2:, :c2], 0.0),
                                 jnp.where(incl, gram[i][c2:, c2:], 0.0)], axis=1).astype(BF16) for i in ls]

        x = [jnp.where(eye, 1.0, 0.0) + n_ab[i] for i in ls]
        pw = n_ab
        for _ in range(5):
            pwb = [pw[i].astype(BF16) for i in ls]
            pw = [dotf(pwb[i], pwb[i]) for i in ls]
            x = [x[i] + dotf(pw[i].astype(BF16), x[i].astype(BF16)) for i in ls]

        av = [dotf(a_ak[i], v_sb[i]) for i in ls]
        pqb = [dotf(x[i].astype(BF16), jnp.concatenate([a_s[i], av[i]], axis=1).astype(BF16)).astype(BF16)
               for i in ls]
        zero_blk = jnp.zeros((c2, LANES), BF16)
        top = [dotf(g_bk[i], jnp.concatenate([pqb[i], jnp.concatenate([zero_blk, v_sb[i]], axis=1)], axis=0))
               for i in ls]
        bot = [lax.dot_general(bh_s[i], pqb[i], _TN, preferred_element_type=F32) for i in ls]
        kv = [lax.dot_general(kh_s[i], v_sb[i], _TN, preferred_element_type=F32) for i in ls]
        r2 = [(r_s[i] + top[i][:, :LANES]).astype(BF16) for i in ls]
        tm = [(bot[i][:, :LANES] + jnp.where(eye, jnp.exp(clast[i]), 0.0)).astype(BF16) for i in ls]

        state = [st_ref[p] for p in pairs]
        o_ch = []
        for ci in range(RW_SPAN):
            idx = [ci * RW_PAIRS + p for p in pairs]
            s_old = [state[p].astype(BF16) for p in pairs]
            o_st = [dotf(r2[idx[p]], s_old[p]) + top[idx[p]][:, LANES:] for p in pairs]
            state = [dotf(tm[idx[p]], s_old[p]) + bot[idx[p]][:, LANES:] + kv[idx[p]] for p in pairs]
            o_ch.append([o_st[p][:CHUNK] + o_st[p][CHUNK:] for p in pairs])
        for p in pairs:
            st_ref[p] = state[p]
        o = [jnp.concatenate([o_ch[ci][p] for ci in range(RW_SPAN)], axis=0) for p in pairs]

        inv_n = 1.0 / RW_DIM
        mean = [_dot2(o[p], ones2) * inv_n for p in pairs]
        dlt = [o[p] - mean[p] for p in pairs]
        var = [_dot2(dlt[p] * dlt[p], ones2) * inv_n for p in pairs]
        bonus = [_dot2(r_f[p] * kmod_f[p] * vec_ref[4:5, sls[p]], ones2) * v_f[p] for p in pairs]
        for p in pairs:
            on = dlt[p] * lax.rsqrt(var[p] + RW_GN_EPS) * vec_ref[5:6, sls[p]] + vec_ref[6:7, sls[p]]
            o_ref[rs, sls[p]] = ((on + bonus[p]) * gate_all[:, sls[p]]).astype(o_ref.dtype)
        return carry

    lax.fori_loop(0, rows // span, step, 0)


def _rwkv_mixer(z, mu, vecs, w2, a2, g2, rows=256):
    t = z.shape[0]
    rows = min(rows, t)
    span = RW_SPAN * CHUNK
    pos = np.arange(span)
    tril = jnp.asarray(((pos[:, None] >= pos[None, :]) & (pos[:, None] // CHUNK == pos[None, :] // CHUNK))
                       .astype(np.float32), dtype=BF16)
    hid = np.arange(LANES) // RW_DIM
    ones2 = jnp.asarray((hid[:, None] == hid[None, :]).astype(np.float32), dtype=BF16)
    const = lambda i: (0, 0)
    return pl.pallas_call(
        functools.partial(_rwkv_body, rows=rows),
        grid=(t // rows,),
        in_specs=[pl.BlockSpec((rows, RW_Z), lambda i: (i, 0)),
                  pl.BlockSpec((1, RW_Z), const),
                  pl.BlockSpec((8, BRANCH_W), const),
                  pl.BlockSpec((LANES, BRANCH_W), const),
                  pl.BlockSpec((LANES, BRANCH_W), const),
                  pl.BlockSpec((LANES, BRANCH_W), const),
                  pl.BlockSpec((span, span), const),
                  pl.BlockSpec((LANES, LANES), const)],
        out_specs=pl.BlockSpec((rows, BRANCH_W), lambda i: (i, 0)),
        out_shape=jax.ShapeDtypeStruct((t, BRANCH_W), BF16),
        scratch_shapes=[pltpu.VMEM((1, RW_Z), F32), pltpu.VMEM((RW_PAIRS, LANES, LANES), F32)],
        compiler_params=_cparams(("arbitrary",)),
        name="rwkv7",
    )(z, mu, vecs, w2, a2, g2, tril, ones2)


def _sb_body(q_ref, k_ref, v_ref, upper_ref, o_ref, acc_ref, car_ref):
    i = pl.program_id(1)
    q = q_ref[...]
    upper = upper_ref[...]
    acc_ref[...] = jnp.zeros_like(acc_ref)
    car_ref[...] = jnp.zeros_like(car_ref)
    parts = range(SB_PARTS)
    rows = SB_BLOCK // SB_PARTS
    cut = lambda a, p: a[p * rows:(p + 1) * rows]
    dotf = functools.partial(jnp.dot, preferred_element_type=F32)
    qpos = [i * SB_BLOCK + p * rows + lax.broadcasted_iota(I32, (rows, SB_BLOCK), 0) for p in parts]
    kidx = lax.broadcasted_iota(I32, (rows, SB_BLOCK), 1)

    def body(state):
        j, _ = state
        k0 = pl.multiple_of(j * SB_BLOCK, SB_BLOCK)
        kj = k_ref[pl.ds(k0, SB_BLOCK), :]
        vj = v_ref[pl.ds(k0, SB_BLOCK), :]
        z = [lax.dot_general(cut(q, p), kj, _NT, preferred_element_type=F32) for p in parts]
        strict = [(kidx + j * SB_BLOCK) < qpos[p] for p in parts]
        lk = [jnp.where(strict[p], jnp.minimum(-z[p], 0.0) - jnp.log1p(jnp.exp(-jnp.abs(z[p]))), 0.0)
              for p in parts]
        hi = [lk[p].astype(BF16) for p in parts]
        lo = [(lk[p] - hi[p].astype(F32)).astype(BF16) for p in parts]
        later = [dotf(hi[p], upper) + dotf(lo[p], upper) for p in parts]
        car = [car_ref[p] for p in parts]
        w = [jnp.where(strict[p], jnp.exp(z[p] + lk[p] + later[p] + car[p]), 0.0).astype(BF16) for p in parts]
        pv = [dotf(w[p], vj) for p in parts]
        top = None
        for p in parts:
            acc_ref[p] += pv[p]
            c = car[p] + jnp.sum(lk[p], axis=-1, keepdims=True)
            car_ref[p] = c
            top = jnp.max(c) if top is None else jnp.maximum(top, jnp.max(c))
        return j - 1, top

    def cond(state):
        j, top = state
        return jnp.logical_and(j >= 0, top > -SB_SKIP)

    lax.while_loop(cond, body, (i, jnp.float32(0.0)))
    for p in parts:
        o_ref[p * rows:(p + 1) * rows, :] = acc_ref[p].astype(o_ref.dtype)


def _sb_attention(z):
    t = z.shape[0]
    heads = BRANCH_W // SB_DIM
    upper = jnp.asarray(np.triu(np.ones((SB_BLOCK, SB_BLOCK), np.float32), 1).T, dtype=BF16)
    return pl.pallas_call(
        _sb_body,
        grid=(heads, t // SB_BLOCK),
        in_specs=[pl.BlockSpec((SB_BLOCK, SB_DIM), lambda h, i: (i, h)),
                  pl.BlockSpec((t, SB_DIM), lambda h, i: (0, heads + h)),
                  pl.BlockSpec((t, SB_DIM), lambda h, i: (0, 2 * heads + h)),
                  pl.BlockSpec((SB_BLOCK, SB_BLOCK), lambda h, i: (0, 0))],
        out_specs=pl.BlockSpec((SB_BLOCK, SB_DIM), lambda h, i: (i, h)),
        out_shape=jax.ShapeDtypeStruct((t, heads * SB_DIM), BF16),
        scratch_shapes=[pltpu.VMEM((SB_PARTS, SB_BLOCK // SB_PARTS, SB_DIM), F32),
                        pltpu.VMEM((SB_PARTS, SB_BLOCK // SB_PARTS, 1), F32)],
        compiler_params=_cparams(("arbitrary", "arbitrary")),
        name="stick_breaking",
    )(z, z, z, upper)


def _merge_body(b0_ref, b1_ref, b2_ref, b3_ref, gate_ref, wbr_ref, wout_ref, x_ref, lnw_ref, lnb_ref,
                xo_ref, xb_ref, xp_ref):
    merged = None
    for g, b_ref in enumerate((b0_ref, b1_ref, b2_ref, b3_ref)):
        y = jnp.dot(b_ref[...], wbr_ref[g], preferred_element_type=F32)
        y = y * gate_ref[:, g * D_MODEL:(g + 1) * D_MODEL].astype(F32)
        merged = y if merged is None else merged + y
    mix = jnp.dot(merged.astype(BF16), wout_ref[...], preferred_element_type=F32)
    y = _layer_norm(DEEPNORM_ALPHA * x_ref[...] + mix, lnw_ref[...], lnb_ref[...])
    xo_ref[...] = y
    xb_ref[...] = y.astype(BF16)
    _store_rows(xp_ref, _pack_pair(y[:, :D_MODEL // 2], y[:, D_MODEL // 2:]))


def _merge(branches, gates, w_br, w_out, x, ln_w, ln_b, tm=256):
    t = x.shape[0]
    tm = min(tm, t)
    row = lambda i: (i, 0)
    const2 = lambda i: (0, 0)
    return pl.pallas_call(
        _merge_body,
        grid=(t // tm,),
        in_specs=[pl.BlockSpec((tm, BRANCH_W), row)] * 4 + [
            pl.BlockSpec((tm, N_BRANCHES * D_MODEL), row),
            pl.BlockSpec((N_BRANCHES, BRANCH_W, D_MODEL), lambda i: (0, 0, 0)),
            pl.BlockSpec((D_MODEL, D_MODEL), const2),
            pl.BlockSpec((tm, D_MODEL), row),
            pl.BlockSpec((1, D_MODEL), const2),
            pl.BlockSpec((1, D_MODEL), const2)],
        out_specs=[pl.BlockSpec((tm, D_MODEL), row), pl.BlockSpec((tm, D_MODEL), row),
                   pl.BlockSpec((tm * ROW_TILES, LANES), row)],
        out_shape=[jax.ShapeDtypeStruct((t, D_MODEL), F32), jax.ShapeDtypeStruct((t, D_MODEL), BF16),
                   jax.ShapeDtypeStruct((t * ROW_TILES, LANES), U32)],
        compiler_params=_cparams(("parallel",), 56),
        name="merge_ln",
    )(*branches, gates, w_br, w_out, x, ln_w, ln_b)


def _first_index(hit, idx, size, axis):
    return jnp.min(jnp.where(hit, idx, size), axis=axis, keepdims=True)


def _router_body(x_ref, wr_ref, bias_ref, before_ref, idx_ref, w_ref, rank_ref, cnt_ref, run_ref, *, tm):
    @pl.when(pl.program_id(0) == 0)
    def _():
        run_ref[...] = jnp.zeros_like(run_ref)

    per = N_EXPERTS // N_GROUPS
    w_hi, w_lo = _split(wr_ref[...])
    x_hi, x_lo = _split(x_ref[...])
    ntf = functools.partial(lax.dot_general, dimension_numbers=_NT, preferred_element_type=F32)
    logits = ntf(w_hi, x_hi) + ntf(w_hi, x_lo) + ntf(w_lo, x_hi)
    scores = _sigmoid(logits)
    biased = scores + bias_ref[:, 0:1]
    g3 = biased.reshape(N_GROUPS, per, tm)
    pos = lax.broadcasted_iota(I32, (N_GROUPS, per, tm), 1)
    m1 = jnp.max(g3, axis=1, keepdims=True)
    f1 = _first_index(g3 == m1, pos, per, 1)
    m2 = jnp.max(jnp.where(pos == f1, -jnp.inf, g3), axis=1, keepdims=True)
    gscore = (m1 + m2).reshape(N_GROUPS, tm)

    gpos = lax.broadcasted_iota(I32, (N_GROUPS, tm), 0)
    chosen = jnp.zeros((N_GROUPS, tm), F32)
    cur = gscore
    for _ in range(TOPK_GROUPS):
        m = jnp.max(cur, axis=0, keepdims=True)
        pick = gpos == _first_index(cur == m, gpos, N_GROUPS, 0)
        chosen = jnp.where(pick, 1.0, chosen)
        cur = jnp.where(pick, -jnp.inf, cur)
    ok = jnp.broadcast_to(chosen.reshape(N_GROUPS, 1, tm), (N_GROUPS, per, tm)).reshape(N_EXPERTS, tm)

    epos = lax.broadcasted_iota(I32, (N_EXPERTS, tm), 0)
    cur = jnp.where(ok > 0.5, biased, -jnp.inf)
    picks, idx_rows, w_rows = [], [], []
    member = jnp.zeros((N_EXPERTS, tm), F32)
    for _ in range(TOP_K):
        m = jnp.max(cur, axis=0, keepdims=True)
        f = _first_index(cur == m, epos, N_EXPERTS, 0)
        pick = epos == f
        picks.append(pick)
        idx_rows.append(f)
        w_rows.append(jnp.sum(jnp.where(pick, scores, 0.0), axis=0, keepdims=True))
        member = jnp.where(pick, 1.0, member)
        cur = jnp.where(pick, -jnp.inf, cur)
    w_sel = jnp.concatenate(w_rows, axis=0)
    w_ref[...] = ROUTED_SCALE * w_sel / jnp.sum(w_sel, axis=0, keepdims=True)
    idx_ref[...] = jnp.concatenate(idx_rows, axis=0)

    seen = jnp.dot(member.astype(BF16), before_ref[...], preferred_element_type=F32) + run_ref[:, 0:1]
    rank_rows = [jnp.sum(jnp.where(pk, seen, 0.0), axis=0, keepdims=True) for pk in picks]
    rank_ref[...] = jnp.concatenate(rank_rows, axis=0).astype(I32)
    run_ref[...] = run_ref[...] + jnp.sum(member, axis=1, keepdims=True)
    cnt_ref[...] = run_ref[...]


def _router(x, w_router_t, bias, tm=512):
    t = x.shape[0]
    tm = min(tm, t)
    before = jnp.asarray(np.triu(np.ones((tm, tm), np.float32), 1), dtype=BF16)
    slot = lambda i: (0, i)
    const = lambda i: (0, 0)
    return pl.pallas_call(
        functools.partial(_router_body, tm=tm),
        grid=(t // tm,),
        in_specs=[pl.BlockSpec((tm, D_MODEL), lambda i: (i, 0)),
                  pl.BlockSpec((N_EXPERTS, D_MODEL), const),
                  pl.BlockSpec((N_EXPERTS, LANES), const),
                  pl.BlockSpec((tm, tm), const)],
        out_specs=[pl.BlockSpec((TOP_K, tm), slot), pl.BlockSpec((TOP_K, tm), slot),
                   pl.BlockSpec((TOP_K, tm), slot), pl.BlockSpec((N_EXPERTS, LANES), const)],
        out_shape=[jax.ShapeDtypeStruct((TOP_K, t), I32), jax.ShapeDtypeStruct((TOP_K, t), F32),
                   jax.ShapeDtypeStruct((TOP_K, t), I32), jax.ShapeDtypeStruct((N_EXPERTS, LANES), F32)],
        scratch_shapes=[pltpu.VMEM((N_EXPERTS, LANES), F32)],
        compiler_params=_cparams(("arbitrary",)),
        name="router",
    )(x, w_router_t, bias, before)


def _dest_body(start_ref, idx_ref, rank_ref, dest_ref):
    idx = idx_ref[...]
    base = jnp.zeros(idx.shape, I32)
    for e in range(N_EXPERTS):
        base = jnp.where(idx == e, start_ref[e], base)
    dest_ref[...] = base + rank_ref[...]


def _dest_rows(pad_start, idx, rank, tm=2048):
    t = idx.shape[1]
    tm = min(tm, t)
    slot = lambda i, s: (0, i)
    return pl.pallas_call(
        _dest_body,
        grid_spec=pltpu.PrefetchScalarGridSpec(
            num_scalar_prefetch=1, grid=(t // tm,),
            in_specs=[pl.BlockSpec((TOP_K, tm), slot), pl.BlockSpec((TOP_K, tm), slot)],
            out_specs=pl.BlockSpec((TOP_K, tm), slot)),
        out_shape=jax.ShapeDtypeStruct((TOP_K, t), I32),
        compiler_params=_cparams(("parallel",)),
        name="dest_rows",
    )(pad_start, idx, rank)


def _dispatch_body(zb_ref, dest_ref, x_ref, xs_hbm, zero_ref, sem, zsem, *, tm, n_fill):
    @pl.when(pl.program_id(0) == 0)
    def _():
        zero_ref[...] = jnp.zeros_like(zero_ref)

        def fill(i, carry):
            @pl.when(zb_ref[i] >= 0)
            def _():
                r0 = pl.multiple_of(zb_ref[i] * (ROW_BLOCK * ROW_TILES), ROW_BLOCK * ROW_TILES)
                pltpu.make_async_copy(zero_ref, xs_hbm.at[pl.ds(r0, ROW_BLOCK * ROW_TILES)], zsem).start()
            return carry

        def drain(i, carry):
            @pl.when(zb_ref[i] >= 0)
            def _():
                pltpu.make_async_copy(zero_ref, xs_hbm.at[pl.ds(0, ROW_BLOCK * ROW_TILES)], zsem).wait()
            return carry

        lax.fori_loop(0, n_fill, fill, 0)
        lax.fori_loop(0, n_fill, drain, 0)

    def row(t, carry):
        for k in range(TOP_K):
            pltpu.make_async_copy(_row(x_ref, t), _row(xs_hbm, dest_ref[k, t]), sem).start(
                priority=k % 2)
        return carry

    lax.fori_loop(0, tm, row, 0)
    for _ in range(TOP_K):
        pltpu.make_async_copy(x_ref, xs_hbm.at[pl.ds(0, tm * ROW_TILES)], sem).wait()


def _dispatch(zero_blocks, dest, x_packed, n_rows, tm=256):
    t = x_packed.shape[0] // ROW_TILES
    tm = min(tm, t)
    return pl.pallas_call(
        functools.partial(_dispatch_body, tm=tm, n_fill=zero_blocks.shape[0]),
        grid_spec=pltpu.PrefetchScalarGridSpec(
            num_scalar_prefetch=1, grid=(t // tm,),
            in_specs=[pl.BlockSpec((TOP_K, tm), lambda i, zb: (0, i), memory_space=pltpu.SMEM),
                      pl.BlockSpec((tm * ROW_TILES, LANES), lambda i, zb: (i, 0))],
            out_specs=pl.BlockSpec(memory_space=pl.ANY),
            scratch_shapes=[pltpu.VMEM((ROW_BLOCK * ROW_TILES, LANES), U32), pltpu.SemaphoreType.DMA(()),
                            pltpu.SemaphoreType.DMA(())]),
        out_shape=jax.ShapeDtypeStruct((n_rows * ROW_TILES, LANES), U32),
        compiler_params=_cparams(("arbitrary",)),
        name="dispatch",
    )(zero_blocks, dest, x_packed)


def _experts_body(be_ref, nu_ref, nxt_ref, slot_ref, xs_ref, wg_hbm, wu_hbm, wd_hbm, ys_ref,
                  wgf_ref, wuf_ref, wdf_ref, wgb_ref, wub_ref, wdb_ref, sem, *, layer):
    def weight_copies(e, s):
        return (pltpu.make_async_copy(wg_hbm.at[layer, e], wgf_ref.at[s], sem.at[s]),
                pltpu.make_async_copy(wu_hbm.at[layer, e], wuf_ref.at[s], sem.at[s]),
                pltpu.make_async_copy(wd_hbm.at[layer, e], wdf_ref.at[s], sem.at[s]))

    @pl.when(pl.program_id(0) == 0)
    def _():
        for c in weight_copies(be_ref[0], slot_ref[0]):
            c.start()

    def block(b, xs_blk, ys_blk):
        used = b < nu_ref[0]
        new_expert = jnp.logical_or(b == 0, be_ref[b] != be_ref[jnp.maximum(b - 1, 0)])

        @pl.when(jnp.logical_and(used, new_expert))
        def _():
            s = slot_ref[b]
            for c in weight_copies(be_ref[b], s):
                c.wait()

            @pl.when(nxt_ref[b] >= 0)
            def _():
                for c in weight_copies(nxt_ref[b], 1 - s):
                    c.start()

            wgb_ref[...] = wgf_ref[s].astype(BF16)
            wub_ref[...] = wuf_ref[s].astype(BF16)
            wdb_ref[...] = wdf_ref[s].astype(BF16)

        @pl.when(used)
        def _():
            half = D_MODEL // 2
            lo, hi = _unpack_pair(_load_rows(xs_blk))
            lo = lo.astype(BF16)
            hi = hi.astype(BF16)
            gate = (jnp.dot(lo, wgb_ref[:half, :], preferred_element_type=F32)
                    + jnp.dot(hi, wgb_ref[half:, :], preferred_element_type=F32))
            up = (jnp.dot(lo, wub_ref[:half, :], preferred_element_type=F32)
                  + jnp.dot(hi, wub_ref[half:, :], preferred_element_type=F32))
            h = (_silu(gate) * up).astype(BF16)
            y = jnp.dot(h, wdb_ref[...], preferred_element_type=F32)
            _store_rows(ys_blk, _pack_pair(y[:, :half], y[:, half:]))

        @pl.when(jnp.logical_not(used))
        def _():
            ys_blk[...] = jnp.zeros_like(ys_blk)

    rows = ROW_BLOCK * ROW_TILES
    for sub in range(EXPERT_STEP_BLOCKS):
        block(pl.program_id(0) * EXPERT_STEP_BLOCKS + sub,
              xs_ref.at[pl.ds(sub * rows, rows)], ys_ref.at[pl.ds(sub * rows, rows)])


def _experts(block_expert, n_used, next_expert, slot, xs, wg, wu, wd, layer):
    n_rows = xs.shape[0] // ROW_TILES
    n_steps = n_rows // (ROW_BLOCK * EXPERT_STEP_BLOCKS)
    step_rows = EXPERT_STEP_BLOCKS * ROW_BLOCK * ROW_TILES
    blk = lambda i, be, nu, nx, sl: (jnp.minimum(i, (nu[0] - 1) // EXPERT_STEP_BLOCKS), 0)
    out_blk = lambda i, be, nu, nx, sl: (i, 0)
    hbm = pl.BlockSpec(memory_space=pl.ANY)
    return pl.pallas_call(
        functools.partial(_experts_body, layer=layer),
        grid_spec=pltpu.PrefetchScalarGridSpec(
            num_scalar_prefetch=4, grid=(n_steps,),
            in_specs=[pl.BlockSpec((step_rows, LANES), blk), hbm, hbm, hbm],
            out_specs=pl.BlockSpec((step_rows, LANES), out_blk),
            scratch_shapes=[pltpu.VMEM((2, D_MODEL, EXPERT_W), F32), pltpu.VMEM((2, D_MODEL, EXPERT_W), F32),
                            pltpu.VMEM((2, EXPERT_W, D_MODEL), F32),
                            pltpu.VMEM((D_MODEL, EXPERT_W), BF16), pltpu.VMEM((D_MODEL, EXPERT_W), BF16),
                            pltpu.VMEM((EXPERT_W, D_MODEL), BF16), pltpu.SemaphoreType.DMA((2,))]),
        out_shape=jax.ShapeDtypeStruct((n_rows * ROW_TILES, LANES), U32),
        compiler_params=_cparams(("arbitrary",), 56),
        name="experts",
    )(block_expert, n_used, next_expert, slot, xs, wg, wu, wd)


def _combine_body(dest_ref, dnext_ref, w_ref, x_ref, xb_ref, sg_ref, su_ref, sd_ref, lnw_ref, lnb_ref, ys_hbm,
                  xo_ref, xb_out_ref, buf_ref, lo_ref, hi_ref, sem, *, tm, n_steps):
    i = pl.program_id(0)
    cur = i % 2

    def request(d_ref, s):
        def row(t, carry):
            for k in range(TOP_K):
                pltpu.make_async_copy(_row(ys_hbm, d_ref[k, t]), _row(buf_ref.at[s, k], t), sem.at[s]).start(
                    priority=k % 2)
            return carry

        lax.fori_loop(0, tm, row, 0)

    @pl.when(i == 0)
    def _():
        request(dest_ref, 0)

    @pl.when(i + 1 < n_steps)
    def _():
        request(dnext_ref, 1 - cur)

    for k in range(TOP_K):
        pltpu.make_async_copy(ys_hbm.at[pl.ds(0, tm * ROW_TILES)], buf_ref.at[cur, k], sem.at[cur]).wait()

    xb = xb_ref[...]
    h = _silu(jnp.dot(xb, sg_ref[...], preferred_element_type=F32)) * jnp.dot(xb, su_ref[...],
                                                                           preferred_element_type=F32)
    shared = jnp.dot(h.astype(BF16), sd_ref[...], preferred_element_type=F32)

    lo_acc = hi_acc = None
    for k in range(TOP_K):
        lo, hi = _unpack_pair(buf_ref[cur, k])
        wk = w_ref[:, k:k + 1]
        lo_acc = wk * lo if lo_acc is None else lo_acc + wk * lo
        hi_acc = wk * hi if hi_acc is None else hi_acc + wk * hi
    lo_ref[...] = lo_acc
    hi_ref[...] = hi_acc
    routed = jnp.concatenate([_load_rows(lo_ref), _load_rows(hi_ref)], axis=1)
    y = _layer_norm(DEEPNORM_ALPHA * x_ref[...] + routed + shared, lnw_ref[...], lnb_ref[...])
    xo_ref[...] = y
    xb_out_ref[...] = y.astype(BF16)


def _combine(dest, w_sel, x, xb, sg, su, sd, ln_w, ln_b, ys, tm=256):
    t = x.shape[0]
    tm = min(tm, t)
    n_steps = t // tm
    row = lambda i: (i, 0)
    slot = lambda i: (0, i)
    slot_next = lambda i: (0, jnp.minimum(i + 1, n_steps - 1))
    const = lambda i: (0, 0)
    return pl.pallas_call(
        functools.partial(_combine_body, tm=tm, n_steps=n_steps),
        grid=(n_steps,),
        in_specs=[pl.BlockSpec((TOP_K, tm), slot, memory_space=pltpu.SMEM),
                  pl.BlockSpec((TOP_K, tm), slot_next, memory_space=pltpu.SMEM),
                  pl.BlockSpec((tm * ROW_TILES, TOP_K), row),
                  pl.BlockSpec((tm, D_MODEL), row),
                  pl.BlockSpec((tm, D_MODEL), row),
                  pl.BlockSpec((D_MODEL, EXPERT_W), const),
                  pl.BlockSpec((D_MODEL, EXPERT_W), const),
                  pl.BlockSpec((EXPERT_W, D_MODEL), const),
                  pl.BlockSpec((1, D_MODEL), const),
                  pl.BlockSpec((1, D_MODEL), const),
                  pl.BlockSpec(memory_space=pl.ANY)],
        out_specs=[pl.BlockSpec((tm, D_MODEL), row), pl.BlockSpec((tm, D_MODEL), row)],
        out_shape=[jax.ShapeDtypeStruct((t, D_MODEL), F32), jax.ShapeDtypeStruct((t, D_MODEL), BF16)],
        scratch_shapes=[pltpu.VMEM((2, TOP_K, tm * ROW_TILES, LANES), U32),
                        pltpu.VMEM((tm * ROW_TILES, LANES), F32), pltpu.VMEM((tm * ROW_TILES, LANES), F32),
                        pltpu.SemaphoreType.DMA((2,))],
        compiler_params=_cparams(("arbitrary",), 56),
        name="combine_ln",
    )(dest, dest, w_sel, x, xb, sg, su, sd, ln_w, ln_b, ys)


def _pad_cols(a, width):
    return jnp.pad(a, [(0, 0)] * (a.ndim - 1) + [(0, width - a.shape[-1])])


def _pad_rows(a, height):
    return jnp.pad(a, ((0, height - a.shape[0]), (0, 0)))


def _pad_heads(a, heads, dk):
    lead = a.shape[:-1]
    padded = jnp.pad(a.reshape(*lead, heads, dk), [(0, 0)] * (len(lead) + 1) + [(0, HEAD_PAD - dk)])
    return padded.reshape(*lead, heads * HEAD_PAD)


def _relayout_body(w_ref, o_ref):
    o_ref[...] = jnp.zeros_like(o_ref)

    def put(dst, src, width, scale=None):
        v = w_ref[src:src + width, :]
        if scale is not None:
            v = v * scale
        o_ref[dst:dst + width, :] = v.astype(o_ref.dtype)

    put(ZHG_OFF, HG_OFF, 4 * BRANCH_W)
    put(ZRW_OFF, RW_OFF, 3 * BRANCH_W)
    put(ZRW_OFF + 3 * BRANCH_W, RW_OFF + 1536, 32)
    put(ZRW_OFF + 3 * BRANCH_W + LANES, RW_OFF + 1568, 32)
    put(ZRW_OFF + 3 * BRANCH_W + 2 * LANES, RW_OFF + 1600, 96)
    put(ZSB_OFF, SB_OFF, BRANCH_W, SB_DIM ** -0.5)
    put(ZSB_OFF + BRANCH_W, SB_OFF + BRANCH_W, 2 * BRANCH_W)
    for h in range(4):
        put(ZGLA_OFF + h * HEAD_PAD, GLA_OFF + h * GLA_DK, GLA_DK)
        put(ZGLA_OFF + BRANCH_W + h * HEAD_PAD, GLA_OFF + 256 + h * GLA_DK, GLA_DK)
    put(ZGLA_OFF + 2 * BRANCH_W, GLA_OFF + 512, BRANCH_W)
    put(ZGLA_OFF + 3 * BRANCH_W, GLA_OFF + 1040, BRANCH_W)
    put(ZGLA_OFF + 4 * BRANCH_W, GLA_OFF + 1024, 16)
    put(ZGATE_OFF, GATE_OFF, N_BRANCHES * D_MODEL)


def _relayout_in_weight(w_in, tk=128):
    w_in = jnp.swapaxes(w_in, 1, 2)
    n_layers, width, d = w_in.shape
    return pl.pallas_call(
        _relayout_body,
        grid=(n_layers, d // tk),
        in_specs=[pl.BlockSpec((None, width, tk), lambda l, i: (l, 0, i))],
        out_specs=pl.BlockSpec((None, Z_TOTAL, tk), lambda l, i: (l, 0, i)),
        out_shape=jax.ShapeDtypeStruct((n_layers, Z_TOTAL, d), BF16),
        compiler_params=_cparams(("parallel", "parallel")),
        name="relayout_w_in",
    )(w_in)


def _token_mixing(xb, w_all, layer, lower_bound, hg_norm_w, rw_mu, rw_w0, rw_w2, rw_a0, rw_a2, rw_g2, rw_kk,
                  rw_ka, rw_rk, rw_ln_w, rw_ln_b, gla_g2, gla_gb, gla_norm_w):
    row = lambda a: a.reshape(1, -1).astype(F32)

    z_hg = _matmul(xb, w_all, layer, ZHG_OFF, 4 * BRANCH_W, F32, tn=2 * PROJ_TN)
    o_hg = _gated_mixer(z_hg, row(lower_bound), row(hg_norm_w), jnp.zeros((8, BRANCH_W), F32), "hgrn2")

    z_rw = _matmul(xb, w_all, layer, ZRW_OFF, RW_Z, F32, tn=2 * PROJ_TN)
    mu = jnp.concatenate([rw_mu[:1536], jnp.pad(rw_mu[1536:1568], (0, 96)), jnp.pad(rw_mu[1568:1600], (0, 96)),
                          jnp.pad(rw_mu[1600:1696], (0, 32 + RW_Z - 3 * BRANCH_W - 3 * LANES))]).reshape(1, RW_Z)
    vecs = jnp.stack([rw_w0, rw_a0, rw_kk, rw_ka, rw_rk, rw_ln_w, rw_ln_b, jnp.zeros_like(rw_w0)]).astype(F32)
    o_rw = _rwkv_mixer(z_rw, mu, vecs, _pad_rows(rw_w2, LANES), _pad_rows(rw_a2, LANES), _pad_rows(rw_g2, LANES))

    o_sb = _sb_attention(_matmul(xb, w_all, layer, ZSB_OFF, 3 * BRANCH_W, BF16))

    z_gla = _matmul(xb, w_all, layer, ZGLA_OFF, GLA_Z, F32)
    g2p = _pad_rows(_pad_heads(gla_g2, 4, GLA_DK), LANES)
    gbp = _pad_heads(gla_gb.reshape(1, -1), 4, GLA_DK)
    o_gla = _gated_mixer(z_gla, gbp, row(gla_norm_w), g2p, "gla")

    gates = _matmul(xb, w_all, layer, ZGATE_OFF, N_BRANCHES * D_MODEL, BF16, act="sigmoid", tm=2048, tn=1024)
    return (o_hg, o_rw, o_sb, o_gla), gates


def _moe(x, xb, xp, w_router, router_bias, we_gate, we_up, we_down, layer, ws_gate, ws_up, ws_down, ln_w, ln_b):
    t = x.shape[0]
    bias = jnp.broadcast_to(router_bias.astype(F32).reshape(N_EXPERTS, 1), (N_EXPERTS, LANES))
    idx, w_sel, rank, counts = _router(x, w_router.T.astype(F32), bias)

    cnt = counts[:, 0].astype(I32)
    padded = (cnt + ROW_BLOCK - 1) // ROW_BLOCK * ROW_BLOCK
    pad_end = jnp.cumsum(padded)
    pad_start = (pad_end - padded).astype(I32)
    n_blocks = -(-(t * TOP_K // ROW_BLOCK + N_EXPERTS) // EXPERT_STEP_BLOCKS) * EXPERT_STEP_BLOCKS
    n_used = (pad_end[-1:] // ROW_BLOCK).astype(I32)
    first_row = jnp.arange(n_blocks, dtype=I32) * ROW_BLOCK
    block_expert = jnp.minimum(jnp.sum(pad_end[None, :] <= first_row[:, None], axis=1), N_EXPERTS - 1).astype(I32)

    last_block = jnp.where(padded > 0, pad_end // ROW_BLOCK - 1, -1)
    tail = n_used[0] + jnp.arange(N_EXPERTS, dtype=I32)
    zero_blocks = jnp.concatenate([last_block, jnp.where(tail < n_blocks, tail, -1)]).astype(I32)

    has_rows = cnt > 0
    eid = jnp.arange(N_EXPERTS, dtype=I32)
    later = jnp.where(has_rows[None, :] & (eid[None, :] > eid[:, None]), eid[None, :], N_EXPERTS)
    next_used = jnp.min(later, axis=1)
    next_used = jnp.where(next_used < N_EXPERTS, next_used, -1).astype(I32)
    ordinal = jnp.cumsum(has_rows.astype(I32)) - 1
    of_block = block_expert[:, None] == eid[None, :]
    next_expert = jnp.sum(jnp.where(of_block, next_used[None, :], 0), axis=1).astype(I32)
    slot = (jnp.sum(jnp.where(of_block, ordinal[None, :], 0), axis=1) % 2).astype(I32)

    dest = _dest_rows(pad_start, idx, rank)
    xs = _dispatch(zero_blocks, dest, xp, n_blocks * ROW_BLOCK)
    ys = _experts(block_expert, n_used, next_expert, slot, xs, we_gate, we_up, we_down, layer)
    w_rows = jnp.repeat(w_sel.T, ROW_TILES, axis=0)
    return _combine(dest, w_rows, x, xb, ws_gate.astype(BF16), ws_up.astype(BF16), ws_down.astype(BF16),
                    ln_w.reshape(1, -1), ln_b.reshape(1, -1), ys)


def kernel(x, w_in, hg_lb_logits, hg_norm_w, rw_mu, rw_w0, rw_w2, rw_a0, rw_a2, rw_g2, rw_kk, rw_ka, rw_rk,
           rw_ln_w, rw_ln_b, gla_g2, gla_gb, gla_norm_w, w_br, w_out, ln1_w, ln1_b, w_router, router_bias,
           we_gate, we_up, we_down, ws_gate, ws_up, ws_down, ln2_w, ln2_b):
    bsz, t, d = x.shape
    cum = jnp.cumsum(jax.nn.softmax(hg_lb_logits.astype(F32), axis=0), axis=0)
    lower_bounds = cum - cum[0:1]
    w_all = _relayout_in_weight(w_in)
    outs = []
    for bi in range(bsz):
        xf = x[bi].astype(F32)
        xb = xf.astype(BF16)
        for l in range(DEPTH):
            branches, gates = _token_mixing(
                xb, w_all, l, lower_bounds[l], hg_norm_w[l], rw_mu[l], rw_w0[l], rw_w2[l], rw_a0[l], rw_a2[l],
                rw_g2[l], rw_kk[l], rw_ka[l], rw_rk[l], rw_ln_w[l], rw_ln_b[l], gla_g2[l], gla_gb[l],
                gla_norm_w[l])
            xf, xb, xp = _merge(branches, gates, w_br[l].astype(BF16), w_out[l].astype(BF16), xf,
                                ln1_w[l].reshape(1, -1), ln1_b[l].reshape(1, -1))
            xf, xb = _moe(xf, xb, xp, w_router[l], router_bias[l], we_gate, we_up, we_down, l,
                          ws_gate[l], ws_up[l], ws_down[l], ln2_w[l], ln2_b[l])
        outs.append(xf)
    return jnp.stack(outs).astype(x.dtype)
```

```python
import functools

import jax
import jax.numpy as jnp
import numpy as np
from jax import lax
from jax.experimental import pallas as pl
from jax.experimental.pallas import tpu as pltpu

F32 = jnp.float32
BF16 = jnp.bfloat16
I32 = jnp.int32
U32 = jnp.uint32

D_MODEL = 2048
DEPTH = 2
BRANCH_W = 512
N_BRANCHES = 4
CHUNK = 64
SUB = 16
N_SUB = CHUNK // SUB
LANES = 128
HEAD_PAD = 128
RW_HEADS = 8
RW_DIM = 64
RW_PAIRS = RW_HEADS // 2
GATED_SPAN = 4
RW_SPAN = 4
GLA_DK = 64
GLA_TAU = 16.0
SB_DIM = 128
SB_BLOCK = 256
SB_PARTS = 2
SB_SKIP = 120.0
N_EXPERTS = 64
TOP_K = 8
N_GROUPS = 8
TOPK_GROUPS = 4
EXPERT_W = 512
ROUTED_SCALE = 2.5
ROW_BLOCK = 256
EXPERT_STEP_BLOCKS = 2
DEEPNORM_ALPHA = (2 * DEPTH) ** 0.25
LN_EPS = 1e-5
RW_GN_EPS = 64e-5
LOG2E = 1.4426950408889634

HG_OFF = 0
RW_OFF = 2048
SB_OFF = 3744
GLA_OFF = 5280
GATE_OFF = 6832
PROJ_TN = 512
RW_Z = 2048
GLA_Z = 2560
ZHG_OFF = 0
ZRW_OFF = ZHG_OFF + 4 * BRANCH_W
ZSB_OFF = ZRW_OFF + RW_Z
ZGLA_OFF = ZSB_OFF + 3 * BRANCH_W
ZGATE_OFF = ZGLA_OFF + GLA_Z
Z_TOTAL = ZGATE_OFF + N_BRANCHES * D_MODEL

_NT = (((1,), (1,)), ((), ()))
_TN = (((0,), (0,)), ((), ()))


def _cparams(sem, vmem_mb=48):
    return pltpu.CompilerParams(dimension_semantics=sem, vmem_limit_bytes=vmem_mb << 20)


def _sigmoid(x):
    return 1.0 / (1.0 + jnp.exp(-x))


def _log_sigmoid(x):
    return jnp.minimum(x, 0.0) - jnp.log1p(jnp.exp(-jnp.abs(x)))


def _silu(x):
    return x * _sigmoid(x)


def _layer_norm(y, w, b):
    mu = jnp.mean(y, axis=-1, keepdims=True)
    d = y - mu
    var = jnp.mean(d * d, axis=-1, keepdims=True)
    return d * lax.rsqrt(var + LN_EPS) * w + b


def _pack_pair(lo, hi):
    lo_b = lax.bitcast_convert_type(lo.astype(BF16).astype(F32), U32) >> 16
    hi_b = lax.bitcast_convert_type(hi.astype(BF16).astype(F32), U32) & jnp.uint32(0xFFFF0000)
    return lo_b | hi_b


def _unpack_pair(u):
    lo = lax.bitcast_convert_type(u << 16, F32)
    hi = lax.bitcast_convert_type(u & jnp.uint32(0xFFFF0000), F32)
    return lo, hi


ROW_TILES = D_MODEL // 2 // LANES


def _store_rows(ref, packed):
    r = packed.shape[0]
    for j in range(ROW_TILES):
        ref[pl.ds(j, r, stride=ROW_TILES), :] = packed[:, j * LANES:(j + 1) * LANES]


def _load_rows(ref):
    r = ref.shape[0] // ROW_TILES
    return jnp.concatenate([ref[pl.ds(j, r, stride=ROW_TILES), :] for j in range(ROW_TILES)], axis=1)


def _row(ref, i):
    return ref.at[pl.ds(pl.multiple_of(i * ROW_TILES, ROW_TILES), ROW_TILES)]


def _split(x):
    hi = x.astype(BF16)
    return hi, (x - hi.astype(F32)).astype(BF16)


def _dot2(x, ones_bf16):
    hi, lo = _split(x)
    return (jnp.dot(hi, ones_bf16, preferred_element_type=F32)
            + jnp.dot(lo, ones_bf16, preferred_element_type=F32))


def _ldot2(ones_bf16, x):
    hi, lo = _split(x)
    return (jnp.dot(ones_bf16, hi, preferred_element_type=F32)
            + jnp.dot(ones_bf16, lo, preferred_element_type=F32))


def _dot3(a, b):
    a_hi, a_lo = _split(a)
    b_hi, b_lo = _split(b)
    return (jnp.dot(a_hi, b_hi, preferred_element_type=F32) + jnp.dot(a_hi, b_lo, preferred_element_type=F32)
            + jnp.dot(a_lo, b_hi, preferred_element_type=F32))


def _mm_body(x_ref, w_ref, o_ref, *, act):
    acc = lax.dot_general(x_ref[...], w_ref[...], _NT, preferred_element_type=F32)
    if act == "sigmoid":
        acc = _sigmoid(acc)
    o_ref[...] = acc.astype(o_ref.dtype)


def _matmul(x, w, layer, col0, n, out_dtype, act=None, tm=1024, tn=PROJ_TN):
    m, k = x.shape
    tm = min(tm, m)
    off = col0 // tn
    return pl.pallas_call(
        functools.partial(_mm_body, act=act),
        grid=(m // tm, n // tn),
        in_specs=[pl.BlockSpec((tm, k), lambda i, j: (i, 0)),
                  pl.BlockSpec((None, tn, k), lambda i, j: (layer, off + j, 0))],
        out_specs=pl.BlockSpec((tm, tn), lambda i, j: (i, j)),
        out_shape=jax.ShapeDtypeStruct((m, n), out_dtype),
        compiler_params=_cparams(("parallel", "parallel")),
        name="proj",
    )(x, w)


def _gated_span(q, k, v, g, st, tril):
    hs = range(len(q))
    lanes = [(ci, h) for ci in range(GATED_SPAN) for h in hs]
    ls = range(len(lanes))
    subs = range(N_SUB)
    dotf = functools.partial(jnp.dot, preferred_element_type=F32)
    blk = lambda x, i: x[i * SUB:(i + 1) * SUB]
    cut = lambda xs: [xs[h][ci * CHUNK:(ci + 1) * CHUNK] for ci, h in lanes]
    b_f = [_ldot2(tril, g[h] * LOG2E) for h in hs]
    qe_f = [(q[h] * jnp.exp2(b_f[h])).astype(BF16) for h in hs]
    vb_f = [v[h].astype(BF16) for h in hs]
    b, qc, kc, vb, qe = cut(b_f), cut(q), cut(k), cut(vb_f), cut(qe_f)
    blast = [b[i][CHUNK - 1:CHUNK, :] for i in ls]
    ends = [[b[i][(j + 1) * SUB - 1:(j + 1) * SUB, :] for j in subs] for i in ls]
    khat = [[blk(kc[i], j) * jnp.exp2(ends[i][j] - blk(b[i], j)) for j in subs] for i in ls]
    o_parts = [[None] * N_SUB for _ in ls]

    for j in range(N_SUB - 1):
        lo = (j + 1) * SUB
        qs = [(qc[i][lo:] * jnp.exp2(b[i][lo:] - ends[i][j])).astype(BF16) for i in ls]
        a = [lax.dot_general(qs[i], khat[i][j].astype(BF16), _NT, preferred_element_type=F32) for i in ls]
        pv = [dotf(a[i].astype(BF16), blk(vb[i], j)) for i in ls]
        for i in ls:
            for t in range(j + 1, N_SUB):
                piece = pv[i][(t - j - 1) * SUB:(t - j) * SUB]
                o_parts[i][t] = piece if o_parts[i][t] is None else o_parts[i][t] + piece

    lane = lax.broadcasted_iota(I32, (SUB, LANES), 1)
    trow = lax.broadcasted_iota(I32, (SUB, 1), 0)
    half = SUB // 2
    for t in subs:
        d_top = [jnp.zeros((half, LANES), F32) for _ in ls]
        d_bot = [jnp.zeros((half, LANES), F32) for _ in ls]
        for s in range(SUB):
            for i in ls:
                bi, qi, ki = blk(b[i], t), blk(qc[i], t), blk(kc[i], t)
                if s < half:
                    col = jnp.sum(qi * jnp.exp2(bi - bi[s:s + 1, :]) * ki[s:s + 1, :], axis=-1, keepdims=True)
                    d_top[i] = jnp.where(lane[:half] == s, col[:half], d_top[i])
                    d_bot[i] = jnp.where(lane[:half] == s, col[half:], d_bot[i])
                else:
                    col = jnp.sum(qi[half:] * jnp.exp2(bi[half:] - bi[s:s + 1, :]) * ki[s:s + 1, :],
                                  axis=-1, keepdims=True)
                    d_bot[i] = jnp.where(lane[:half] == s, col, d_bot[i])
        d = [jnp.concatenate([d_top[i], d_bot[i]], axis=0) for i in ls]
        pv = [dotf(jnp.where(lane <= trow, d[i], 0.0)[:, :SUB].astype(BF16), blk(vb[i], t)) for i in ls]
        for i in ls:
            o_parts[i][t] = pv[i] if o_parts[i][t] is None else o_parts[i][t] + pv[i]

    o_intra = [jnp.concatenate(o_parts[i], axis=0) for i in ls]
    kd = [jnp.concatenate([khat[i][j] * jnp.exp2(blast[i] - ends[i][j]) for j in subs], axis=0).astype(BF16)
          for i in ls]
    kv = [lax.dot_general(vb[i], kd[i], _TN, preferred_element_type=F32) for i in ls]
    dec = [jnp.exp2(blast[i]) for i in ls]

    state = list(st)
    o_ch = []
    for ci in range(GATED_SPAN):
        idx = [ci * len(q) + h for h in hs]
        o_ch.append([lax.dot_general(qe[idx[h]], state[h].astype(BF16), _NT, preferred_element_type=F32)
                     + o_intra[idx[h]] for h in hs])
        state = [state[h] * dec[idx[h]] + kv[idx[h]] for h in hs]
    o = [jnp.concatenate([o_ch[ci][h] for ci in range(GATED_SPAN)], axis=0) for h in hs]
    return o, state


def _gated_body(z_ref, aux_ref, nw_ref, g2_ref, tril_ref, o_ref, st_ref, *, mode, rows):
    @pl.when(pl.program_id(0) == 0)
    def _():
        st_ref[...] = jnp.zeros_like(st_ref)

    tril = tril_ref[...]
    span = GATED_SPAN * CHUNK

    def step(c, carry):
        r0 = pl.multiple_of(c * span, span)
        rs = pl.ds(r0, span)
        hs = range(4)
        sls = [slice(h * HEAD_PAD, (h + 1) * HEAD_PAD) for h in hs]
        zq = [z_ref[rs, h * HEAD_PAD:(h + 1) * HEAD_PAD] for h in hs]
        zk = [z_ref[rs, BRANCH_W + h * HEAD_PAD:BRANCH_W + (h + 1) * HEAD_PAD] for h in hs]
        v = [z_ref[rs, 2 * BRANCH_W + h * HEAD_PAD:2 * BRANCH_W + (h + 1) * HEAD_PAD] for h in hs]
        if mode == "hgrn2":
            q = [_silu(zq[h]) for h in hs]
            k = [(1.0 - aux_ref[0:1, sls[h]]) * _sigmoid(-zk[h]) for h in hs]
            g = [jnp.log1p(-k[h]) for h in hs]
        else:
            q = [zq[h] * (GLA_DK ** -0.5) for h in hs]
            k = zk
            la = _dot3(z_ref[rs, 4 * BRANCH_W:4 * BRANCH_W + LANES], g2_ref[...]) + aux_ref[...]
            g = [_log_sigmoid(la[:, sls[h]]) * (1.0 / GLA_TAU) for h in hs]
        o, st_new = _gated_span(q, k, v, g, [st_ref[h] for h in hs], tril)
        for h in hs:
            st_ref[h] = st_new[h]
            gate = z_ref[rs, 3 * BRANCH_W + h * HEAD_PAD:3 * BRANCH_W + (h + 1) * HEAD_PAD]
            on = o[h] * lax.rsqrt(jnp.mean(o[h] * o[h], axis=-1, keepdims=True) + LN_EPS)
            o_ref[rs, sls[h]] = (on * nw_ref[0:1, sls[h]] * _silu(gate)).astype(o_ref.dtype)
        return carry

    lax.fori_loop(0, rows // span, step, 0)


def _gated_mixer(z, aux, norm_w, g2, mode, rows=1024):
    t, wz = z.shape
    rows = min(rows, t)
    span = GATED_SPAN * CHUNK
    pos = np.arange(span)
    tril = jnp.asarray(((pos[:, None] >= pos[None, :]) & (pos[:, None] // CHUNK == pos[None, :] // CHUNK))
                       .astype(np.float32), dtype=BF16)
    return pl.pallas_call(
        functools.partial(_gated_body, mode=mode, rows=rows),
        grid=(t // rows,),
        in_specs=[pl.BlockSpec((rows, wz), lambda i: (i, 0)),
                  pl.BlockSpec((1, BRANCH_W), lambda i: (0, 0)),
                  pl.BlockSpec((1, BRANCH_W), lambda i: (0, 0)),
                  pl.BlockSpec(g2.shape, lambda i: (0, 0)),
                  pl.BlockSpec((span, span), lambda i: (0, 0))],
        out_specs=pl.BlockSpec((rows, BRANCH_W), lambda i: (i, 0)),
        out_shape=jax.ShapeDtypeStruct((t, BRANCH_W), BF16),
        scratch_shapes=[pltpu.VMEM((4, HEAD_PAD, HEAD_PAD), F32)],
        compiler_params=_cparams(("arbitrary",)),
        name="gated_" + mode,
    )(z, aux, norm_w, g2, tril)


def _rwkv_body(z_ref, mu_ref, vec_ref, w2_ref, a2_ref, g2_ref, tril_ref, ones2_ref, o_ref,
               prev_ref, st_ref, *, rows):
    @pl.when(pl.program_id(0) == 0)
    def _():
        prev_ref[...] = jnp.zeros_like(prev_ref)
        st_ref[...] = jnp.zeros_like(st_ref)

    tril = tril_ref[...]
    ones2 = ones2_ref[...]
    row128 = lax.broadcasted_iota(I32, (2 * CHUNK, 2 * CHUNK), 0)
    col128 = lax.broadcasted_iota(I32, (2 * CHUNK, 2 * CHUNK), 1)
    rt = jnp.where(row128 >= CHUNK, row128 - CHUNK, row128)
    ct = jnp.where(col128 >= CHUNK, col128 - CHUNK, col128)
    strict = rt > ct
    incl = rt >= ct
    eye = row128 == col128
    head0 = lax.broadcasted_iota(I32, (1, LANES), 1) < RW_DIM
    span = RW_SPAN * CHUNK
    first_row = lax.broadcasted_iota(I32, (span, 1), 0) == 0

    def stack(x):
        return jnp.concatenate([jnp.where(head0, x, 0.0), jnp.where(head0, 0.0, x)], axis=0)

    def step(c, carry):
        r0 = pl.multiple_of(c * span, span)
        rs = pl.ds(r0, span)
        z = z_ref[rs, :]
        zprev = jnp.where(first_row, prev_ref[...], pltpu.roll(z, 1, axis=0))
        prev_ref[...] = z[span - 1:span, :]
        zs = z + (zprev - z) * mu_ref[...]
        lw = zs[:, 3 * BRANCH_W:3 * BRANCH_W + LANES]
        la = zs[:, 3 * BRANCH_W + LANES:3 * BRANCH_W + 2 * LANES]
        lg = zs[:, 3 * BRANCH_W + 2 * LANES:3 * BRANCH_W + 3 * LANES]
        wl = -(vec_ref[0:1, :] + _dot3(jnp.tanh(lw), w2_ref[...]))
        w_raw = -(jnp.maximum(wl, 0.0) + jnp.log1p(jnp.exp(-jnp.abs(wl)))) - 0.5
        logw_all = -jnp.exp(w_raw)
        iclr_all = _sigmoid(vec_ref[1:2, :] + _dot3(la, a2_ref[...]))
        gate_all = _dot3(_sigmoid(lg), g2_ref[...])
        cum_all = _ldot2(tril, logw_all)

        pairs = range(RW_PAIRS)
        sls = [slice(p * LANES, (p + 1) * LANES) for p in pairs]
        dotf = functools.partial(jnp.dot, preferred_element_type=F32)
        c2 = 2 * CHUNK
        r_f = [zs[:, p * LANES:(p + 1) * LANES] for p in pairs]
        k_f = [zs[:, BRANCH_W + p * LANES:BRANCH_W + (p + 1) * LANES] for p in pairs]
        v_f = [zs[:, 2 * BRANCH_W + p * LANES:2 * BRANCH_W + (p + 1) * LANES] for p in pairs]
        cum_f = [cum_all[:, s] for s in sls]
        iclr_f = [iclr_all[:, s] for s in sls]
        kkr = [k_f[p] * vec_ref[2:3, sls[p]] for p in pairs]
        ssq = [_dot2(kkr[p] * kkr[p], ones2) for p in pairs]
        kk_f = [kkr[p] / jnp.maximum(jnp.sqrt(ssq[p]), 1e-12) for p in pairs]
        kmod_f = [k_f[p] * (1.0 + (iclr_f[p] - 1.0) * vec_ref[3:4, sls[p]]) for p in pairs]
        bb_f = [kk_f[p] * iclr_f[p] for p in pairs]
        eneg_f = [jnp.exp(-cum_f[p]) for p in pairs]
        a_f = [-kk_f[p] * jnp.exp(cum_f[p] - logw_all[:, sls[p]]) for p in pairs]
        rd_f = [r_f[p] * jnp.exp(cum_f[p]) for p in pairs]
        bn_f = [bb_f[p] * eneg_f[p] for p in pairs]
        kn_f = [kmod_f[p] * eneg_f[p] for p in pairs]

        lanes = [(ci, p) for ci in range(RW_SPAN) for p in pairs]
        ls = range(len(lanes))
        cut = lambda xs: [xs[p][ci * CHUNK:(ci + 1) * CHUNK] for ci, p in lanes]
        cum, bb, kmod = cut(cum_f), cut(bb_f), cut(kmod_f)
        clast = [cum[i][CHUNK - 1:CHUNK, :] for i in ls]
        e_end = [jnp.exp(clast[i] - cum[i]) for i in ls]
        a_s = [stack(x) for x in cut(a_f)]
        r_s = [stack(x) for x in cut(rd_f)]
        b_s = [stack(x) for x in cut(bn_f)]
        k_s = [stack(x) for x in cut(kn_f)]
        bh_s = [stack(bb[i] * e_end[i]).astype(BF16) for i in ls]
        kh_s = [stack(kmod[i] * e_end[i]).astype(BF16) for i in ls]
        v_sb = [stack(x).astype(BF16) for x in cut(v_f)]

        left = [jnp.concatenate([a_s[i], r_s[i]], axis=0).astype(BF16) for i in ls]
        right = [jnp.concatenate([b_s[i], k_s[i]], axis=0).astype(BF16) for i in ls]
        gram = [lax.dot_general(left[i], right[i], _NT, preferred_element_type=F32) for i in ls]
        n_ab = [jnp.where(strict, gram[i][:c2, :c2], 0.0) for i in ls]
        a_ak = [jnp.where(strict, gram[i][:c2, c2:], 0.0).astype(BF16) for i in ls]
        g_bk = [jnp.concatenate([jnp.where(incl, gram[i][c2:, :c2], 0.0),
                                 jnp.where(incl, gram[i][c2:, c2:], 0.0)], axis=1).astype(BF16) for i in ls]

        x = [jnp.where(eye, 1.0, 0.0) + n_ab[i] for i in ls]
        pw = n_ab
        for _ in range(5):
            pwb = [pw[i].astype(BF16) for i in ls]
            pw = [dotf(pwb[i], pwb[i]) for i in ls]
            x = [x[i] + dotf(pw[i].astype(BF16), x[i].astype(BF16)) for i in ls]

        av = [dotf(a_ak[i], v_sb[i]) for i in ls]
        pqb = [dotf(x[i].astype(BF16), jnp.concatenate([a_s[i], av[i]], axis=1).astype(BF16)).astype(BF16)
               for i in ls]
        zero_blk = jnp.zeros((c2, LANES), BF16)
        top = [dotf(g_bk[i], jnp.concatenate([pqb[i], jnp.concatenate([zero_blk, v_sb[i]], axis=1)], axis=0))
               for i in ls]
        bot = [lax.dot_general(bh_s[i], pqb[i], _TN, preferred_element_type=F32) for i in ls]
        kv = [lax.dot_general(kh_s[i], v_sb[i], _TN, preferred_element_type=F32) for i in ls]
        r2 = [(r_s[i] + top[i][:, :LANES]).astype(BF16) for i in ls]
        tm = [(bot[i][:, :LANES] + jnp.where(eye, jnp.exp(clast[i]), 0.0)).astype(BF16) for i in ls]

        state = [st_ref[p] for p in pairs]
        o_ch = []
        for ci in range(RW_SPAN):
            idx = [ci * RW_PAIRS + p for p in pairs]
            s_old = [state[p].astype(BF16) for p in pairs]
            o_st = [dotf(r2[idx[p]], s_old[p]) + top[idx[p]][:, LANES:] for p in pairs]
            state = [dotf(tm[idx[p]], s_old[p]) + bot[idx[p]][:, LANES:] + kv[idx[p]] for p in pairs]
            o_ch.append([o_st[p][:CHUNK] + o_st[p][CHUNK:] for p in pairs])
        for p in pairs:
            st_ref[p] = state[p]
        o = [jnp.concatenate([o_ch[ci][p] for ci in range(RW_SPAN)], axis=0) for p in pairs]

        inv_n = 1.0 / RW_DIM
        mean = [_dot2(o[p], ones2) * inv_n for p in pairs]
        dlt = [o[p] - mean[p] for p in pairs]
        var = [_dot2(dlt[p] * dlt[p], ones2) * inv_n for p in pairs]
        bonus = [_dot2(r_f[p] * kmod_f[p] * vec_ref[4:5, sls[p]], ones2) * v_f[p] for p in pairs]
        for p in pairs:
            on = dlt[p] * lax.rsqrt(var[p] + RW_GN_EPS) * vec_ref[5:6, sls[p]] + vec_ref[6:7, sls[p]]
            o_ref[rs, sls[p]] = ((on + bonus[p]) * gate_all[:, sls[p]]).astype(o_ref.dtype)
        return carry

    lax.fori_loop(0, rows // span, step, 0)


def _rwkv_mixer(z, mu, vecs, w2, a2, g2, rows=1024):
    t = z.shape[0]
    rows = min(rows, t)
    span = RW_SPAN * CHUNK
    pos = np.arange(span)
    tril = jnp.asarray(((pos[:, None] >= pos[None, :]) & (pos[:, None] // CHUNK == pos[None, :] // CHUNK))
                       .astype(np.float32), dtype=BF16)
    hid = np.arange(LANES) // RW_DIM
    ones2 = jnp.asarray((hid[:, None] == hid[None, :]).astype(np.float32), dtype=BF16)
    const = lambda i: (0, 0)
    return pl.pallas_call(
        functools.partial(_rwkv_body, rows=rows),
        grid=(t // rows,),
        in_specs=[pl.BlockSpec((rows, RW_Z), lambda i: (i, 0)),
                  pl.BlockSpec((1, RW_Z), const),
                  pl.BlockSpec((8, BRANCH_W), const),
                  pl.BlockSpec((LANES, BRANCH_W), const),
                  pl.BlockSpec((LANES, BRANCH_W), const),
                  pl.BlockSpec((LANES, BRANCH_W), const),
                  pl.BlockSpec((span, span), const),
                  pl.BlockSpec((LANES, LANES), const)],
        out_specs=pl.BlockSpec((rows, BRANCH_W), lambda i: (i, 0)),
        out_shape=jax.ShapeDtypeStruct((t, BRANCH_W), BF16),
        scratch_shapes=[pltpu.VMEM((1, RW_Z), F32), pltpu.VMEM((RW_PAIRS, LANES, LANES), F32)],
        compiler_params=_cparams(("arbitrary",)),
        name="rwkv7",
    )(z, mu, vecs, w2, a2, g2, tril, ones2)


def _sb_body(q_ref, k_ref, v_ref, upper_ref, o_ref, acc_ref, car_ref):
    i = pl.program_id(1)
    q = q_ref[...]
    upper = upper_ref[...]
    acc_ref[...] = jnp.zeros_like(acc_ref)
    car_ref[...] = jnp.zeros_like(car_ref)
    parts = range(SB_PARTS)
    rows = SB_BLOCK // SB_PARTS
    cut = lambda a, p: a[p * rows:(p + 1) * rows]
    dotf = functools.partial(jnp.dot, preferred_element_type=F32)
    qpos = [i * SB_BLOCK + p * rows + lax.broadcasted_iota(I32, (rows, SB_BLOCK), 0) for p in parts]
    kidx = lax.broadcasted_iota(I32, (rows, SB_BLOCK), 1)

    def body(state):
        j, _ = state
        k0 = pl.multiple_of(j * SB_BLOCK, SB_BLOCK)
        kj = k_ref[pl.ds(k0, SB_BLOCK), :]
        vj = v_ref[pl.ds(k0, SB_BLOCK), :]
        z = [lax.dot_general(cut(q, p), kj, _NT, preferred_element_type=F32) for p in parts]
        strict = [(kidx + j * SB_BLOCK) < qpos[p] for p in parts]
        lk = [jnp.where(strict[p], jnp.minimum(-z[p], 0.0) - jnp.log1p(jnp.exp(-jnp.abs(z[p]))), 0.0)
              for p in parts]
        hi = [lk[p].astype(BF16) for p in parts]
        lo = [(lk[p] - hi[p].astype(F32)).astype(BF16) for p in parts]
        later = [dotf(hi[p], upper) + dotf(lo[p], upper) for p in parts]
        car = [car_ref[p] for p in parts]
        w = [jnp.where(strict[p], jnp.exp(z[p] + lk[p] + later[p] + car[p]), 0.0).astype(BF16) for p in parts]
        pv = [dotf(w[p], vj) for p in parts]
        top = None
        for p in parts:
            acc_ref[p] += pv[p]
            c = car[p] + jnp.sum(lk[p], axis=-1, keepdims=True)
            car_ref[p] = c
            top = jnp.max(c) if top is None else jnp.maximum(top, jnp.max(c))
        return j - 1, top

    def cond(state):
        j, top = state
        return jnp.logical_and(j >= 0, top > -SB_SKIP)

    lax.while_loop(cond, body, (i, jnp.float32(0.0)))
    for p in parts:
        o_ref[p * rows:(p + 1) * rows, :] = acc_ref[p].astype(o_ref.dtype)


def _sb_attention(z):
    t = z.shape[0]
    heads = BRANCH_W // SB_DIM
    upper = jnp.asarray(np.triu(np.ones((SB_BLOCK, SB_BLOCK), np.float32), 1).T, dtype=BF16)
    return pl.pallas_call(
        _sb_body,
        grid=(heads, t // SB_BLOCK),
        in_specs=[pl.BlockSpec((SB_BLOCK, SB_DIM), lambda h, i: (i, h)),
                  pl.BlockSpec((t, SB_DIM), lambda h, i: (0, heads + h)),
                  pl.BlockSpec((t, SB_DIM), lambda h, i: (0, 2 * heads + h)),
                  pl.BlockSpec((SB_BLOCK, SB_BLOCK), lambda h, i: (0, 0))],
        out_specs=pl.BlockSpec((SB_BLOCK, SB_DIM), lambda h, i: (i, h)),
        out_shape=jax.ShapeDtypeStruct((t, heads * SB_DIM), BF16),
        scratch_shapes=[pltpu.VMEM((SB_PARTS, SB_BLOCK // SB_PARTS, SB_DIM), F32),
                        pltpu.VMEM((SB_PARTS, SB_BLOCK // SB_PARTS, 1), F32)],
        compiler_params=_cparams(("arbitrary", "arbitrary")),
        name="stick_breaking",
    )(z, z, z, upper)


def _merge_body(b0_ref, b1_ref, b2_ref, b3_ref, gate_ref, wbr_ref, wout_ref, x_ref, lnw_ref, lnb_ref,
                xo_ref, xb_ref, xp_ref):
    merged = None
    for g, b_ref in enumerate((b0_ref, b1_ref, b2_ref, b3_ref)):
        y = jnp.dot(b_ref[...], wbr_ref[g], preferred_element_type=F32)
        y = y * gate_ref[:, g * D_MODEL:(g + 1) * D_MODEL].astype(F32)
        merged = y if merged is None else merged + y
    mix = jnp.dot(merged.astype(BF16), wout_ref[...], preferred_element_type=F32)
    y = _layer_norm(DEEPNORM_ALPHA * x_ref[...] + mix, lnw_ref[...], lnb_ref[...])
    xo_ref[...] = y
    xb_ref[...] = y.astype(BF16)
    _store_rows(xp_ref, _pack_pair(y[:, :D_MODEL // 2], y[:, D_MODEL // 2:]))


def _merge(branches, gates, w_br, w_out, x, ln_w, ln_b, tm=256):
    t = x.shape[0]
    tm = min(tm, t)
    row = lambda i: (i, 0)
    const2 = lambda i: (0, 0)
    return pl.pallas_call(
        _merge_body,
        grid=(t // tm,),
        in_specs=[pl.BlockSpec((tm, BRANCH_W), row)] * 4 + [
            pl.BlockSpec((tm, N_BRANCHES * D_MODEL), row),
            pl.BlockSpec((N_BRANCHES, BRANCH_W, D_MODEL), lambda i: (0, 0, 0)),
            pl.BlockSpec((D_MODEL, D_MODEL), const2),
            pl.BlockSpec((tm, D_MODEL), row),
            pl.BlockSpec((1, D_MODEL), const2),
            pl.BlockSpec((1, D_MODEL), const2)],
        out_specs=[pl.BlockSpec((tm, D_MODEL), row), pl.BlockSpec((tm, D_MODEL), row),
                   pl.BlockSpec((tm * ROW_TILES, LANES), row)],
        out_shape=[jax.ShapeDtypeStruct((t, D_MODEL), F32), jax.ShapeDtypeStruct((t, D_MODEL), BF16),
                   jax.ShapeDtypeStruct((t * ROW_TILES, LANES), U32)],
        compiler_params=_cparams(("parallel",), 56),
        name="merge_ln",
    )(*branches, gates, w_br, w_out, x, ln_w, ln_b)


def _first_index(hit, idx, size, axis):
    return jnp.min(jnp.where(hit, idx, size), axis=axis, keepdims=True)


def _router_body(x_ref, wr_ref, bias_ref, before_ref, idx_ref, w_ref, rank_ref, cnt_ref, run_ref, *, tm):
    @pl.when(pl.program_id(0) == 0)
    def _():
        run_ref[...] = jnp.zeros_like(run_ref)

    per = N_EXPERTS // N_GROUPS
    w_hi, w_lo = _split(wr_ref[...])
    x_hi, x_lo = _split(x_ref[...])
    ntf = functools.partial(lax.dot_general, dimension_numbers=_NT, preferred_element_type=F32)
    logits = ntf(w_hi, x_hi) + ntf(w_hi, x_lo) + ntf(w_lo, x_hi)
    scores = _sigmoid(logits)
    biased = scores + bias_ref[:, 0:1]
    g3 = biased.reshape(N_GROUPS, per, tm)
    pos = lax.broadcasted_iota(I32, (N_GROUPS, per, tm), 1)
    m1 = jnp.max(g3, axis=1, keepdims=True)
    f1 = _first_index(g3 == m1, pos, per, 1)
    m2 = jnp.max(jnp.where(pos == f1, -jnp.inf, g3), axis=1, keepdims=True)
    gscore = (m1 + m2).reshape(N_GROUPS, tm)

    gpos = lax.broadcasted_iota(I32, (N_GROUPS, tm), 0)
    chosen = jnp.zeros((N_GROUPS, tm), F32)
    cur = gscore
    for _ in range(TOPK_GROUPS):
        m = jnp.max(cur, axis=0, keepdims=True)
        pick = gpos == _first_index(cur == m, gpos, N_GROUPS, 0)
        chosen = jnp.where(pick, 1.0, chosen)
        cur = jnp.where(pick, -jnp.inf, cur)
    ok = jnp.broadcast_to(chosen.reshape(N_GROUPS, 1, tm), (N_GROUPS, per, tm)).reshape(N_EXPERTS, tm)

    epos = lax.broadcasted_iota(I32, (N_EXPERTS, tm), 0)
    cur = jnp.where(ok > 0.5, biased, -jnp.inf)
    picks, idx_rows, w_rows = [], [], []
    member = jnp.zeros((N_EXPERTS, tm), F32)
    for _ in range(TOP_K):
        m = jnp.max(cur, axis=0, keepdims=True)
        f = _first_index(cur == m, epos, N_EXPERTS, 0)
        pick = epos == f
        picks.append(pick)
        idx_rows.append(f)
        w_rows.append(jnp.sum(jnp.where(pick, scores, 0.0), axis=0, keepdims=True))
        member = jnp.where(pick, 1.0, member)
        cur = jnp.where(pick, -jnp.inf, cur)
    w_sel = jnp.concatenate(w_rows, axis=0)
    w_ref[...] = ROUTED_SCALE * w_sel / jnp.sum(w_sel, axis=0, keepdims=True)
    idx_ref[...] = jnp.concatenate(idx_rows, axis=0)

    seen = jnp.dot(member.astype(BF16), before_ref[...], preferred_element_type=F32) + run_ref[:, 0:1]
    rank_rows = [jnp.sum(jnp.where(pk, seen, 0.0), axis=0, keepdims=True) for pk in picks]
    rank_ref[...] = jnp.concatenate(rank_rows, axis=0).astype(I32)
    run_ref[...] = run_ref[...] + jnp.sum(member, axis=1, keepdims=True)
    cnt_ref[...] = run_ref[...]


def _router(x, w_router_t, bias, tm=512):
    t = x.shape[0]
    tm = min(tm, t)
    before = jnp.asarray(np.triu(np.ones((tm, tm), np.float32), 1), dtype=BF16)
    slot = lambda i: (0, i)
    const = lambda i: (0, 0)
    return pl.pallas_call(
        functools.partial(_router_body, tm=tm),
        grid=(t // tm,),
        in_specs=[pl.BlockSpec((tm, D_MODEL), lambda i: (i, 0)),
                  pl.BlockSpec((N_EXPERTS, D_MODEL), const),
                  pl.BlockSpec((N_EXPERTS, LANES), const),
                  pl.BlockSpec((tm, tm), const)],
        out_specs=[pl.BlockSpec((TOP_K, tm), slot), pl.BlockSpec((TOP_K, tm), slot),
                   pl.BlockSpec((TOP_K, tm), slot), pl.BlockSpec((N_EXPERTS, LANES), const)],
        out_shape=[jax.ShapeDtypeStruct((TOP_K, t), I32), jax.ShapeDtypeStruct((TOP_K, t), F32),
                   jax.ShapeDtypeStruct((TOP_K, t), I32), jax.ShapeDtypeStruct((N_EXPERTS, LANES), F32)],
        scratch_shapes=[pltpu.VMEM((N_EXPERTS, LANES), F32)],
        compiler_params=_cparams(("arbitrary",)),
        name="router",
    )(x, w_router_t, bias, before)


def _dest_body(start_ref, idx_ref, rank_ref, dest_ref):
    idx = idx_ref[...]
    base = jnp.zeros(idx.shape, I32)
    for e in range(N_EXPERTS):
        base = jnp.where(idx == e, start_ref[e], base)
    dest_ref[...] = base + rank_ref[...]


def _dest_rows(pad_start, idx, rank, tm=2048):
    t = idx.shape[1]
    tm = min(tm, t)
    slot = lambda i, s: (0, i)
    return pl.pallas_call(
        _dest_body,
        grid_spec=pltpu.PrefetchScalarGridSpec(
            num_scalar_prefetch=1, grid=(t // tm,),
            in_specs=[pl.BlockSpec((TOP_K, tm), slot), pl.BlockSpec((TOP_K, tm), slot)],
            out_specs=pl.BlockSpec((TOP_K, tm), slot)),
        out_shape=jax.ShapeDtypeStruct((TOP_K, t), I32),
        compiler_params=_cparams(("parallel",)),
        name="dest_rows",
    )(pad_start, idx, rank)


def _dispatch_body(zb_ref, dest_ref, x_ref, xs_hbm, zero_ref, sem, zsem, *, tm, n_fill):
    @pl.when(pl.program_id(0) == 0)
    def _():
        zero_ref[...] = jnp.zeros_like(zero_ref)

        def fill(i, carry):
            @pl.when(zb_ref[i] >= 0)
            def _():
                r0 = pl.multiple_of(zb_ref[i] * (ROW_BLOCK * ROW_TILES), ROW_BLOCK * ROW_TILES)
                pltpu.make_async_copy(zero_ref, xs_hbm.at[pl.ds(r0, ROW_BLOCK * ROW_TILES)], zsem).start()
            return carry

        def drain(i, carry):
            @pl.when(zb_ref[i] >= 0)
            def _():
                pltpu.make_async_copy(zero_ref, xs_hbm.at[pl.ds(0, ROW_BLOCK * ROW_TILES)], zsem).wait()
            return carry

        lax.fori_loop(0, n_fill, fill, 0)
        lax.fori_loop(0, n_fill, drain, 0)

    def row(t, carry):
        for k in range(TOP_K):
            pltpu.make_async_copy(_row(x_ref, t), _row(xs_hbm, dest_ref[k, t]), sem).start(
                priority=k % 2)
        return carry

    lax.fori_loop(0, tm, row, 0)
    for _ in range(TOP_K):
        pltpu.make_async_copy(x_ref, xs_hbm.at[pl.ds(0, tm * ROW_TILES)], sem).wait()


def _dispatch(zero_blocks, dest, x_packed, n_rows, tm=256):
    t = x_packed.shape[0] // ROW_TILES
    tm = min(tm, t)
    return pl.pallas_call(
        functools.partial(_dispatch_body, tm=tm, n_fill=zero_blocks.shape[0]),
        grid_spec=pltpu.PrefetchScalarGridSpec(
            num_scalar_prefetch=1, grid=(t // tm,),
            in_specs=[pl.BlockSpec((TOP_K, tm), lambda i, zb: (0, i), memory_space=pltpu.SMEM),
                      pl.BlockSpec((tm * ROW_TILES, LANES), lambda i, zb: (i, 0))],
            out_specs=pl.BlockSpec(memory_space=pl.ANY),
            scratch_shapes=[pltpu.VMEM((ROW_BLOCK * ROW_TILES, LANES), U32), pltpu.SemaphoreType.DMA(()),
                            pltpu.SemaphoreType.DMA(())]),
        out_shape=jax.ShapeDtypeStruct((n_rows * ROW_TILES, LANES), U32),
        compiler_params=_cparams(("arbitrary",)),
        name="dispatch",
    )(zero_blocks, dest, x_packed)


def _experts_body(be_ref, nu_ref, nxt_ref, slot_ref, xs_ref, wg_hbm, wu_hbm, wd_hbm, ys_ref,
                  wgf_ref, wuf_ref, wdf_ref, wgb_ref, wub_ref, wdb_ref, sem, *, layer):
    def weight_copies(e, s):
        return (pltpu.make_async_copy(wg_hbm.at[layer, e], wgf_ref.at[s], sem.at[s]),
                pltpu.make_async_copy(wu_hbm.at[layer, e], wuf_ref.at[s], sem.at[s]),
                pltpu.make_async_copy(wd_hbm.at[layer, e], wdf_ref.at[s], sem.at[s]))

    @pl.when(pl.program_id(0) == 0)
    def _():
        for c in weight_copies(be_ref[0], slot_ref[0]):
            c.start()

    def block(b, xs_blk, ys_blk):
        used = b < nu_ref[0]
        new_expert = jnp.logical_or(b == 0, be_ref[b] != be_ref[jnp.maximum(b - 1, 0)])

        @pl.when(jnp.logical_and(used, new_expert))
        def _():
            s = slot_ref[b]
            for c in weight_copies(be_ref[b], s):
                c.wait()

            @pl.when(nxt_ref[b] >= 0)
            def _():
                for c in weight_copies(nxt_ref[b], 1 - s):
                    c.start()

            wgb_ref[...] = wgf_ref[s].astype(BF16)
            wub_ref[...] = wuf_ref[s].astype(BF16)
            wdb_ref[...] = wdf_ref[s].astype(BF16)

        @pl.when(used)
        def _():
            half = D_MODEL // 2
            lo, hi = _unpack_pair(_load_rows(xs_blk))
            lo = lo.astype(BF16)
            hi = hi.astype(BF16)
            gate = (jnp.dot(lo, wgb_ref[:half, :], preferred_element_type=F32)
                    + jnp.dot(hi, wgb_ref[half:, :], preferred_element_type=F32))
            up = (jnp.dot(lo, wub_ref[:half, :], preferred_element_type=F32)
                  + jnp.dot(hi, wub_ref[half:, :], preferred_element_type=F32))
            h = (_silu(gate) * up).astype(BF16)
            y = jnp.dot(h, wdb_ref[...], preferred_element_type=F32)
            _store_rows(ys_blk, _pack_pair(y[:, :half], y[:, half:]))

        @pl.when(jnp.logical_not(used))
        def _():
            ys_blk[...] = jnp.zeros_like(ys_blk)

    rows = ROW_BLOCK * ROW_TILES
    for sub in range(EXPERT_STEP_BLOCKS):
        block(pl.program_id(0) * EXPERT_STEP_BLOCKS + sub,
              xs_ref.at[pl.ds(sub * rows, rows)], ys_ref.at[pl.ds(sub * rows, rows)])


def _experts(block_expert, n_used, next_expert, slot, xs, wg, wu, wd, layer):
    n_rows = xs.shape[0] // ROW_TILES
    n_steps = n_rows // (ROW_BLOCK * EXPERT_STEP_BLOCKS)
    step_rows = EXPERT_STEP_BLOCKS * ROW_BLOCK * ROW_TILES
    blk = lambda i, be, nu, nx, sl: (jnp.minimum(i, (nu[0] - 1) // EXPERT_STEP_BLOCKS), 0)
    out_blk = lambda i, be, nu, nx, sl: (i, 0)
    hbm = pl.BlockSpec(memory_space=pl.ANY)
    return pl.pallas_call(
        functools.partial(_experts_body, layer=layer),
        grid_spec=pltpu.PrefetchScalarGridSpec(
            num_scalar_prefetch=4, grid=(n_steps,),
            in_specs=[pl.BlockSpec((step_rows, LANES), blk), hbm, hbm, hbm],
            out_specs=pl.BlockSpec((step_rows, LANES), out_blk),
            scratch_shapes=[pltpu.VMEM((2, D_MODEL, EXPERT_W), F32), pltpu.VMEM((2, D_MODEL, EXPERT_W), F32),
                            pltpu.VMEM((2, EXPERT_W, D_MODEL), F32),
                            pltpu.VMEM((D_MODEL, EXPERT_W), BF16), pltpu.VMEM((D_MODEL, EXPERT_W), BF16),
                            pltpu.VMEM((EXPERT_W, D_MODEL), BF16), pltpu.SemaphoreType.DMA((2,))]),
        out_shape=jax.ShapeDtypeStruct((n_rows * ROW_TILES, LANES), U32),
        compiler_params=_cparams(("arbitrary",), 56),
        name="experts",
    )(block_expert, n_used, next_expert, slot, xs, wg, wu, wd)


def _combine_body(dest_ref, dnext_ref, w_ref, x_ref, xb_ref, sg_ref, su_ref, sd_ref, lnw_ref, lnb_ref, ys_hbm,
                  xo_ref, xb_out_ref, buf_ref, lo_ref, hi_ref, sem, *, tm, n_steps):
    i = pl.program_id(0)
    cur = i % 2

    def request(d_ref, s):
        def row(t, carry):
            for k in range(TOP_K):
                pltpu.make_async_copy(_row(ys_hbm, d_ref[k, t]), _row(buf_ref.at[s, k], t), sem.at[s]).start(
                    priority=k % 2)
            return carry

        lax.fori_loop(0, tm, row, 0)

    @pl.when(i == 0)
    def _():
        request(dest_ref, 0)

    @pl.when(i + 1 < n_steps)
    def _():
        request(dnext_ref, 1 - cur)

    for k in range(TOP_K):
        pltpu.make_async_copy(ys_hbm.at[pl.ds(0, tm * ROW_TILES)], buf_ref.at[cur, k], sem.at[cur]).wait()

    xb = xb_ref[...]
    h = _silu(jnp.dot(xb, sg_ref[...], preferred_element_type=F32)) * jnp.dot(xb, su_ref[...],
                                                                           preferred_element_type=F32)
    shared = jnp.dot(h.astype(BF16), sd_ref[...], preferred_element_type=F32)

    lo_acc = hi_acc = None
    for k in range(TOP_K):
        lo, hi = _unpack_pair(buf_ref[cur, k])
        wk = w_ref[:, k:k + 1]
        lo_acc = wk * lo if lo_acc is None else lo_acc + wk * lo
        hi_acc = wk * hi if hi_acc is None else hi_acc + wk * hi
    lo_ref[...] = lo_acc
    hi_ref[...] = hi_acc
    routed = jnp.concatenate([_load_rows(lo_ref), _load_rows(hi_ref)], axis=1)
    y = _layer_norm(DEEPNORM_ALPHA * x_ref[...] + routed + shared, lnw_ref[...], lnb_ref[...])
    xo_ref[...] = y
    xb_out_ref[...] = y.astype(BF16)


def _combine(dest, w_sel, x, xb, sg, su, sd, ln_w, ln_b, ys, tm=256):
    t = x.shape[0]
    tm = min(tm, t)
    n_steps = t // tm
    row = lambda i: (i, 0)
    slot = lambda i: (0, i)
    slot_next = lambda i: (0, jnp.minimum(i + 1, n_steps - 1))
    const = lambda i: (0, 0)
    return pl.pallas_call(
        functools.partial(_combine_body, tm=tm, n_steps=n_steps),
        grid=(n_steps,),
        in_specs=[pl.BlockSpec((TOP_K, tm), slot, memory_space=pltpu.SMEM),
                  pl.BlockSpec((TOP_K, tm), slot_next, memory_space=pltpu.SMEM),
                  pl.BlockSpec((tm * ROW_TILES, TOP_K), row),
                  pl.BlockSpec((tm, D_MODEL), row),
                  pl.BlockSpec((tm, D_MODEL), row),
                  pl.BlockSpec((D_MODEL, EXPERT_W), const),
                  pl.BlockSpec((D_MODEL, EXPERT_W), const),
                  pl.BlockSpec((EXPERT_W, D_MODEL), const),
                  pl.BlockSpec((1, D_MODEL), const),
                  pl.BlockSpec((1, D_MODEL), const),
                  pl.BlockSpec(memory_space=pl.ANY)],
        out_specs=[pl.BlockSpec((tm, D_MODEL), row), pl.BlockSpec((tm, D_MODEL), row)],
        out_shape=[jax.ShapeDtypeStruct((t, D_MODEL), F32), jax.ShapeDtypeStruct((t, D_MODEL), BF16)],
        scratch_shapes=[pltpu.VMEM((2, TOP_K, tm * ROW_TILES, LANES), U32),
                        pltpu.VMEM((tm * ROW_TILES, LANES), F32), pltpu.VMEM((tm * ROW_TILES, LANES), F32),
                        pltpu.SemaphoreType.DMA((2,))],
        compiler_params=_cparams(("arbitrary",), 56),
        name="combine_ln",
    )(dest, dest, w_sel, x, xb, sg, su, sd, ln_w, ln_b, ys)


def _pad_cols(a, width):
    return jnp.pad(a, [(0, 0)] * (a.ndim - 1) + [(0, width - a.shape[-1])])


def _pad_rows(a, height):
    return jnp.pad(a, ((0, height - a.shape[0]), (0, 0)))


def _pad_heads(a, heads, dk):
    lead = a.shape[:-1]
    padded = jnp.pad(a.reshape(*lead, heads, dk), [(0, 0)] * (len(lead) + 1) + [(0, HEAD_PAD - dk)])
    return padded.reshape(*lead, heads * HEAD_PAD)


def _relayout_body(w_ref, o_ref):
    o_ref[...] = jnp.zeros_like(o_ref)

    def put(dst, src, width, scale=None):
        v = w_ref[src:src + width, :]
        if scale is not None:
            v = v * scale
        o_ref[dst:dst + width, :] = v.astype(o_ref.dtype)

    put(ZHG_OFF, HG_OFF, 4 * BRANCH_W)
    put(ZRW_OFF, RW_OFF, 3 * BRANCH_W)
    put(ZRW_OFF + 3 * BRANCH_W, RW_OFF + 1536, 32)
    put(ZRW_OFF + 3 * BRANCH_W + LANES, RW_OFF + 1568, 32)
    put(ZRW_OFF + 3 * BRANCH_W + 2 * LANES, RW_OFF + 1600, 96)
    put(ZSB_OFF, SB_OFF, BRANCH_W, SB_DIM ** -0.5)
    put(ZSB_OFF + BRANCH_W, SB_OFF + BRANCH_W, 2 * BRANCH_W)
    for h in range(4):
        put(ZGLA_OFF + h * HEAD_PAD, GLA_OFF + h * GLA_DK, GLA_DK)
        put(ZGLA_OFF + BRANCH_W + h * HEAD_PAD, GLA_OFF + 256 + h * GLA_DK, GLA_DK)
    put(ZGLA_OFF + 2 * BRANCH_W, GLA_OFF + 512, BRANCH_W)
    put(ZGLA_OFF + 3 * BRANCH_W, GLA_OFF + 1040, BRANCH_W)
    put(ZGLA_OFF + 4 * BRANCH_W, GLA_OFF + 1024, 16)
    put(ZGATE_OFF, GATE_OFF, N_BRANCHES * D_MODEL)


def _relayout_in_weight(w_in, tk=128):
    w_in = jnp.swapaxes(w_in, 1, 2)
    n_layers, width, d = w_in.shape
    return pl.pallas_call(
        _relayout_body,
        grid=(n_layers, d // tk),
        in_specs=[pl.BlockSpec((None, width, tk), lambda l, i: (l, 0, i))],
        out_specs=pl.BlockSpec((None, Z_TOTAL, tk), lambda l, i: (l, 0, i)),
        out_shape=jax.ShapeDtypeStruct((n_layers, Z_TOTAL, d), BF16),
        compiler_params=_cparams(("parallel", "parallel")),
        name="relayout_w_in",
    )(w_in)


def _token_mixing(xb, w_all, layer, lower_bound, hg_norm_w, rw_mu, rw_w0, rw_w2, rw_a0, rw_a2, rw_g2, rw_kk,
                  rw_ka, rw_rk, rw_ln_w, rw_ln_b, gla_g2, gla_gb, gla_norm_w):
    row = lambda a: a.reshape(1, -1).astype(F32)

    z_hg = _matmul(xb, w_all, layer, ZHG_OFF, 4 * BRANCH_W, F32, tn=2 * PROJ_TN)
    o_hg = _gated_mixer(z_hg, row(lower_bound), row(hg_norm_w), jnp.zeros((8, BRANCH_W), F32), "hgrn2")

    z_rw = _matmul(xb, w_all, layer, ZRW_OFF, RW_Z, F32, tn=2 * PROJ_TN)
    mu = jnp.concatenate([rw_mu[:1536], jnp.pad(rw_mu[1536:1568], (0, 96)), jnp.pad(rw_mu[1568:1600], (0, 96)),
                          jnp.pad(rw_mu[1600:1696], (0, 32 + RW_Z - 3 * BRANCH_W - 3 * LANES))]).reshape(1, RW_Z)
    vecs = jnp.stack([rw_w0, rw_a0, rw_kk, rw_ka, rw_rk, rw_ln_w, rw_ln_b, jnp.zeros_like(rw_w0)]).astype(F32)
    o_rw = _rwkv_mixer(z_rw, mu, vecs, _pad_rows(rw_w2, LANES), _pad_rows(rw_a2, LANES), _pad_rows(rw_g2, LANES))

    o_sb = _sb_attention(_matmul(xb, w_all, layer, ZSB_OFF, 3 * BRANCH_W, BF16))

    z_gla = _matmul(xb, w_all, layer, ZGLA_OFF, GLA_Z, F32)
    g2p = _pad_rows(_pad_heads(gla_g2, 4, GLA_DK), LANES)
    gbp = _pad_heads(gla_gb.reshape(1, -1), 4, GLA_DK)
    o_gla = _gated_mixer(z_gla, gbp, row(gla_norm_w), g2p, "gla")

    gates = _matmul(xb, w_all, layer, ZGATE_OFF, N_BRANCHES * D_MODEL, BF16, act="sigmoid", tm=2048, tn=1024)
    return (o_hg, o_rw, o_sb, o_gla), gates


def _moe(x, xb, xp, w_router, router_bias, we_gate, we_up, we_down, layer, ws_gate, ws_up, ws_down, ln_w, ln_b):
    t = x.shape[0]
    bias = jnp.broadcast_to(router_bias.astype(F32).reshape(N_EXPERTS, 1), (N_EXPERTS, LANES))
    idx, w_sel, rank, counts = _router(x, w_router.T.astype(F32), bias)

    cnt = counts[:, 0].astype(I32)
    padded = (cnt + ROW_BLOCK - 1) // ROW_BLOCK * ROW_BLOCK
    pad_end = jnp.cumsum(padded)
    pad_start = (pad_end - padded).astype(I32)
    n_blocks = -(-(t * TOP_K // ROW_BLOCK + N_EXPERTS) // EXPERT_STEP_BLOCKS) * EXPERT_STEP_BLOCKS
    n_used = (pad_end[-1:] // ROW_BLOCK).astype(I32)
    first_row = jnp.arange(n_blocks, dtype=I32) * ROW_BLOCK
    block_expert = jnp.minimum(jnp.sum(pad_end[None, :] <= first_row[:, None], axis=1), N_EXPERTS - 1).astype(I32)

    last_block = jnp.where(padded > 0, pad_end // ROW_BLOCK - 1, -1)
    tail = n_used[0] + jnp.arange(N_EXPERTS, dtype=I32)
    zero_blocks = jnp.concatenate([last_block, jnp.where(tail < n_blocks, tail, -1)]).astype(I32)

    has_rows = cnt > 0
    eid = jnp.arange(N_EXPERTS, dtype=I32)
    later = jnp.where(has_rows[None, :] & (eid[None, :] > eid[:, None]), eid[None, :], N_EXPERTS)
    next_used = jnp.min(later, axis=1)
    next_used = jnp.where(next_used < N_EXPERTS, next_used, -1).astype(I32)
    ordinal = jnp.cumsum(has_rows.astype(I32)) - 1
    of_block = block_expert[:, None] == eid[None, :]
    next_expert = jnp.sum(jnp.where(of_block, next_used[None, :], 0), axis=1).astype(I32)
    slot = (jnp.sum(jnp.where(of_block, ordinal[None, :], 0), axis=1) % 2).astype(I32)

    dest = _dest_rows(pad_start, idx, rank)
    xs = _dispatch(zero_blocks, dest, xp, n_blocks * ROW_BLOCK)
    ys = _experts(block_expert, n_used, next_expert, slot, xs, we_gate, we_up, we_down, layer)
    w_rows = jnp.repeat(w_sel.T, ROW_TILES, axis=0)
    return _combine(dest, w_rows, x, xb, ws_gate.astype(BF16), ws_up.astype(BF16), ws_down.astype(BF16),
                    ln_w.reshape(1, -1), ln_b.reshape(1, -1), ys)


def kernel(x, w_in, hg_lb_logits, hg_norm_w, rw_mu, rw_w0, rw_w2, rw_a0, rw_a2, rw_g2, rw_kk, rw_ka, rw_rk,
           rw_ln_w, rw_ln_b, gla_g2, gla_gb, gla_norm_w, w_br, w_out, ln1_w, ln1_b, w_router, router_bias,
           we_gate, we_up, we_down, ws_gate, ws_up, ws_down, ln2_w, ln2_b):
    bsz, t, d = x.shape
    cum = jnp.cumsum(jax.nn.softmax(hg_lb_logits.astype(F32), axis=0), axis=0)
    lower_bounds = cum - cum[0:1]
    w_all = _relayout_in_weight(w_in)
    outs = []
    for bi in range(bsz):
        xf = x[bi].astype(F32)
        xb = xf.astype(BF16)
        for l in range(DEPTH):
            branches, gates = _token_mixing(
                xb, w_all, l, lower_bounds[l], hg_norm_w[l], rw_mu[l], rw_w0[l], rw_w2[l], rw_a0[l], rw_a2[l],
                rw_g2[l], rw_kk[l], rw_ka[l], rw_rk[l], rw_ln_w[l], rw_ln_b[l], gla_g2[l], gla_gb[l],
                gla_norm_w[l])
            xf, xb, xp = _merge(branches, gates, w_br[l].astype(BF16), w_out[l].astype(BF16), xf,
                                ln1_w[l].reshape(1, -1), ln1_b[l].reshape(1, -1))
            xf, xb = _moe(xf, xb, xp, w_router[l], router_bias[l], we_gate, we_up, we_down, l,
                          ws_gate[l], ws_up[l], ws_down[l], ln2_w[l], ln2_b[l])
        outs.append(xf)
    return jnp.stack(outs).astype(x.dtype)
```
